```python
import jax, jax.numpy as jnp
from jax import lax
import numpy as np

D_MODEL = 1024
BATCH = 8
SEQ = 2048
DEPTH = 2
DEC_BATCH = 128
DEC_SEQ = 8
PAST_LEN = 16384
PAGE_SIZE = 128

N_MIXERS = 2
N_CONV_LAYERS = (DEPTH + 1) // 2
N_GDN_LAYERS = DEPTH // 2
D_CONV = D_MODEL
CONV_A_WIDTH = 3
GDN_HEADS = 8
GDN_DK = 128
GDN_DV = 128
GDN_QK = GDN_HEADS * GDN_DK
GDN_VW = GDN_HEADS * GDN_DV
GDN_CONV_CH = 2 * GDN_QK + GDN_VW
GDN_CONV_WIDTH = 4
GDN_IN = GDN_CONV_CH + GDN_VW + 2 * GDN_HEADS
GDN_CHUNK = 64
ALPHA = (2 * DEPTH) ** 0.25
BETA_INIT = (8 * DEPTH) ** -0.25
LN_EPS = 1e-5
NORM_EPS = 1e-6

kernel_name = 'hybrid_shortconv_gdn_adaln_deepnorm_step'


def causal_dwconv(x, buf, w):
    width = w.shape[0]
    T = x.shape[1]
    xp = jnp.concatenate([buf.astype(x.dtype), x], axis=1)
    y = xp[:, 0:T] * w[0]
    for j in range(1, width):
        y = y + xp[:, j:j + T] * w[j]
    return y, xp[:, T:]


def layer_norm(x, g, b):
    xf = x.astype(jnp.float32)
    mu = jnp.mean(xf, axis=-1, keepdims=True)
    var = jnp.mean(jnp.square(xf - mu), axis=-1, keepdims=True)
    return ((xf - mu) * lax.rsqrt(var + LN_EPS) * g.astype(jnp.float32) + b.astype(jnp.float32)).astype(x.dtype)


def l2norm(x):
    return x * lax.rsqrt(jnp.sum(x * x, axis=-1, keepdims=True) + NORM_EPS)


def gated_delta_rule(q, k, v, g, beta, s0):
    bsz, T, H, dk = q.shape
    dv = v.shape[-1]
    C = min(GDN_CHUNK, T)
    n = -(-T // C)
    pad = n * C - T

    def prep(t):
        t = jnp.pad(t, [(0, 0), (0, pad)] + [(0, 0)] * (t.ndim - 2))
        t = t.reshape((bsz, n, C) + t.shape[2:])
        return jnp.moveaxis(t, 3, 1)

    q, k, v, g, beta = [prep(t) for t in (q * dk ** -0.5, k, v, g, beta)]
    G = jnp.cumsum(g, axis=-1)
    idx = jnp.arange(C)
    causal = idx[:, None] >= idx[None, :]
    strict = idx[:, None] > idx[None, :]
    diff = G[..., :, None] - G[..., None, :]
    L = jnp.where(causal, jnp.exp(jnp.where(causal, diff, 0.0)), 0.0)
    kb = k * beta[..., None]
    A = jnp.where(strict, jnp.einsum('bhnid,bhnjd->bhnij', kb, k) * L, 0.0)
    IA = A + jnp.eye(C, dtype=A.dtype)
    rhs = jnp.concatenate([v * beta[..., None], kb * jnp.exp(G)[..., None]], axis=-1)
    sol = lax.linalg.triangular_solve(IA, rhs, left_side=True, lower=True, unit_diagonal=True)
    u, w = sol[..., :dv], sol[..., dv:]
    attn = jnp.einsum('bhnid,bhnjd->bhnij', q, k) * L
    qg = q * jnp.exp(G)[..., None]
    kt = k * jnp.exp(G[..., -1:] - G)[..., None]
    gl = jnp.exp(G[..., -1])
    xs = tuple(jnp.moveaxis(t, 2, 0) for t in (u, w, qg, attn, kt, gl))

    def step(S, inp):
        u_c, w_c, qg_c, attn_c, kt_c, gl_c = inp
        v_new = u_c - jnp.einsum('bhck,bhkv->bhcv', w_c, S)
        o = jnp.einsum('bhck,bhkv->bhcv', qg_c, S) + jnp.einsum('bhij,bhjv->bhiv', attn_c, v_new)
        S = S * gl_c[..., None, None] + jnp.einsum('bhck,bhcv->bhkv', kt_c, v_new)
        return S, o

    S, o = lax.scan(step, s0, xs)
    o = jnp.transpose(o, (1, 0, 3, 2, 4)).reshape(bsz, n * C, H, dv)[:, :T]
    return o, S


def short_conv_mixer(u, buf, w_in, w_conv, w_out):
    p = u @ w_in
    b_gate = p[..., :D_CONV]
    c_gate = p[..., D_CONV:2 * D_CONV]
    h = p[..., 2 * D_CONV:3 * D_CONV]
    z = p[..., 3 * D_CONV:]
    y, new_buf = causal_dwconv(c_gate * h, buf, w_conv)
    return (b_gate * y * jax.nn.silu(z)) @ w_out, new_buf


def gdn_mixer(u, conv_buf, s0, w_in, w_conv, a_log, dt_bias, norm_w, w_out):
    f32 = jnp.float32
    bsz, T, _ = u.shape
    p = u @ w_in
    qkv, new_buf = causal_dwconv(p[..., :GDN_CONV_CH], conv_buf, w_conv)
    qkv = jax.nn.silu(qkv).astype(f32)
    z = p[..., GDN_CONV_CH:GDN_CONV_CH + GDN_VW]
    b = p[..., GDN_CONV_CH + GDN_VW:GDN_CONV_CH + GDN_VW + GDN_HEADS].astype(f32)
    a = p[..., GDN_CONV_CH + GDN_VW + GDN_HEADS:].astype(f32)
    q = l2norm(qkv[..., :GDN_QK].reshape(bsz, T, GDN_HEADS, GDN_DK))
    k = l2norm(qkv[..., GDN_QK:2 * GDN_QK].reshape(bsz, T, GDN_HEADS, GDN_DK))
    v = qkv[..., 2 * GDN_QK:].reshape(bsz, T, GDN_HEADS, GDN_DV)
    beta = jax.nn.sigmoid(b)
    g = -jnp.exp(a_log.astype(f32)) * jax.nn.softplus(a + dt_bias.astype(f32))
    o, S = gated_delta_rule(q, k, v, g, beta, s0.astype(f32))
    o = o * lax.rsqrt(jnp.mean(o * o, axis=-1, keepdims=True) + NORM_EPS) * norm_w.astype(f32)
    o = o.reshape(bsz, T, GDN_VW).astype(u.dtype) * jax.nn.silu(z)
    return o @ w_out, new_buf, S.astype(s0.dtype)


def trunk(x, c, conv_a, conv_b, ssm_b, w_mod, b_mod, ln_g, ln_b, wa_in, wa_conv, wa_out,
          wb_in, wb_conv, wb_a_log, wb_dt_bias, wb_norm, wb_out):
    new_a, new_cb, new_s = [], [], []
    cs = jax.nn.silu(c)
    for l in range(DEPTH):
        mod = cs @ w_mod[l] + b_mod[l]
        shift = mod[:, None, :D_MODEL]
        scale = mod[:, None, D_MODEL:2 * D_MODEL]
        gate = mod[:, None, 2 * D_MODEL:]
        u = x * (1.0 + scale) + shift
        i = l // N_MIXERS
        if l % N_MIXERS == 0:
            out, nb = short_conv_mixer(u, conv_a[i], wa_in[i], wa_conv[i], wa_out[i])
            new_a.append(nb)
        else:
            out, nb, S = gdn_mixer(u, conv_b[i], ssm_b[i], wb_in[i], wb_conv[i], wb_a_log[i],
                                   wb_dt_bias[i], wb_norm[i], wb_out[i])
            new_cb.append(nb)
            new_s.append(S)
        x = layer_norm(ALPHA * x + gate * out, ln_g[l], ln_b[l])
    return x, jnp.stack(new_a), jnp.stack(new_cb), jnp.stack(new_s)


def setup_inputs(seed: int = 0) -> dict:
    key = jax.random.key(seed)
    ks = jax.random.split(key, 24)
    nrm = lambda k, s, sc: jax.random.normal(k, s, jnp.float32) * sc
    dt = jnp.exp(jax.random.uniform(ks[18], (N_GDN_LAYERS, GDN_HEADS), jnp.float32, np.log(1e-3), np.log(1e-1)))
    return {
        'x_prompt': nrm(ks[0], (BATCH, SEQ, D_MODEL), 1.0),
        'x_sample': nrm(ks[1], (DEC_BATCH, DEC_SEQ, D_MODEL), 1.0),
        'state_conv_a': nrm(ks[2], (N_CONV_LAYERS, DEC_BATCH, CONV_A_WIDTH - 1, D_CONV), 0.5),
        'state_conv_b': nrm(ks[3], (N_GDN_LAYERS, DEC_BATCH, GDN_CONV_WIDTH - 1, GDN_CONV_CH), 1.0),
        'state_ssm_b': nrm(ks[4], (N_GDN_LAYERS, DEC_BATCH, GDN_HEADS, GDN_DK, GDN_DV), 0.1),
        'c_prompt': nrm(ks[5], (BATCH, D_MODEL), 1.0),
        'c_sample': nrm(ks[6], (DEC_BATCH, D_MODEL), 1.0),
        'w_mod': nrm(ks[7], (DEPTH, D_MODEL, 3 * D_MODEL), 0.5 * D_MODEL ** -0.5),
        'b_mod': nrm(ks[8], (DEPTH, 3 * D_MODEL), 0.01),
        'ln_g': 1.0 + nrm(ks[9], (DEPTH, D_MODEL), 0.02),
        'ln_b': nrm(ks[10], (DEPTH, D_MODEL), 0.02),
        'wa_in': nrm(ks[11], (N_CONV_LAYERS, D_MODEL, 4 * D_CONV), D_MODEL ** -0.5),
        'wa_conv': nrm(ks[12], (N_CONV_LAYERS, CONV_A_WIDTH, D_CONV), CONV_A_WIDTH ** -0.5),
        'wa_out': nrm(ks[13], (N_CONV_LAYERS, D_CONV, D_MODEL), BETA_INIT * D_CONV ** -0.5),
        'wb_in': nrm(ks[14], (N_GDN_LAYERS, D_MODEL, GDN_IN), D_MODEL ** -0.5),
        'wb_conv': nrm(ks[15], (N_GDN_LAYERS, GDN_CONV_WIDTH, GDN_CONV_CH), GDN_CONV_WIDTH ** -0.5),
        'wb_a_log': jnp.log(jax.random.uniform(ks[16], (N_GDN_LAYERS, GDN_HEADS), jnp.float32, 1.0, 16.0)),
        'wb_dt_bias': dt + jnp.log(-jnp.expm1(-dt)),
        'wb_norm': 1.0 + nrm(ks[17], (N_GDN_LAYERS, GDN_DV), 0.02),
        'wb_out': nrm(ks[19], (N_GDN_LAYERS, GDN_VW, D_MODEL), BETA_INIT * GDN_VW ** -0.5),
    }


def reference(x_prompt, x_sample, state_conv_a, state_conv_b, state_ssm_b, c_prompt, c_sample,
              w_mod, b_mod, ln_g, ln_b, wa_in, wa_conv, wa_out,
              wb_in, wb_conv, wb_a_log, wb_dt_bias, wb_norm, wb_out):
    bp = x_prompt.shape[0]
    dt_ = x_prompt.dtype
    zero_a = jnp.zeros((N_CONV_LAYERS, bp, CONV_A_WIDTH - 1, D_CONV), dt_)
    zero_cb = jnp.zeros((N_GDN_LAYERS, bp, GDN_CONV_WIDTH - 1, GDN_CONV_CH), dt_)
    zero_s = jnp.zeros((N_GDN_LAYERS, bp, GDN_HEADS, GDN_DK, GDN_DV), state_ssm_b.dtype)
    y_prompt, conv_a_p, conv_b_p, ssm_b_p = trunk(
        x_prompt, c_prompt, zero_a, zero_cb, zero_s, w_mod, b_mod, ln_g, ln_b, wa_in, wa_conv, wa_out,
        wb_in, wb_conv, wb_a_log, wb_dt_bias, wb_norm, wb_out)
    y_sample, conv_a_s, conv_b_s, ssm_b_s = trunk(
        x_sample, c_sample, state_conv_a, state_conv_b, state_ssm_b, w_mod, b_mod, ln_g, ln_b,
        wa_in, wa_conv, wa_out, wb_in, wb_conv, wb_a_log, wb_dt_bias, wb_norm, wb_out)
    return (y_prompt, y_sample, conv_a_p, conv_b_p, ssm_b_p, conv_a_s, conv_b_s, ssm_b_s)
```

```python
import functools

import jax
import jax.numpy as jnp
from jax import lax
from jax.experimental import pallas as pl
from jax.experimental.pallas import tpu as pltpu

F32 = jnp.float32
BF16 = jnp.bfloat16

LN_EPS = 1e-5
NORM_EPS = 1e-6
SUBLANES = 8
LANES = 128
HALO = SUBLANES
GDN_ROWS = 64
VMEM_LIMIT = 56 * 1024 * 1024


def _dot(a, b):
    return jnp.dot(a.astype(BF16), b.astype(BF16), preferred_element_type=F32)


def _dot_nt(a, b):
    return lax.dot_general(a.astype(BF16), b.astype(BF16), (((1,), (1,)), ((), ())),
                           preferred_element_type=F32)


def _dot_tn(a, b):
    return lax.dot_general(a.astype(BF16), b.astype(BF16), (((0,), (0,)), ((), ())),
                           preferred_element_type=F32)


def _silu(x):
    return x * jax.nn.sigmoid(x)


def _split3(x):
    hi = x.astype(BF16)
    r = x - hi.astype(F32)
    mid = r.astype(BF16)
    lo = (r - mid.astype(F32)).astype(BF16)
    return hi, mid, lo


def _layer_norm(r, g, b):
    mu = jnp.mean(r, axis=-1, keepdims=True)
    d = r - mu
    var = jnp.mean(d * d, axis=-1, keepdims=True)
    return d * lax.rsqrt(var + LN_EPS) * g + b


def _mod_kernel(c_ref, w_ref, b_ref, o_ref):
    cs = _silu(c_ref[...])
    o_ref[0] = _dot(cs, w_ref[0]) + b_ref[0]


def _modulation(c_all, w_mod, b_mod):
    depth, d, n3 = w_mod.shape
    rows = c_all.shape[0]
    nb = d
    return pl.pallas_call(
        _mod_kernel,
        grid=(depth, n3 // nb),
        in_specs=[
            pl.BlockSpec((rows, d), lambda l, j: (0, 0)),
            pl.BlockSpec((1, d, nb), lambda l, j: (l, 0, j)),
            pl.BlockSpec((1, 1, nb), lambda l, j: (l, 0, j)),
        ],
        out_specs=pl.BlockSpec((1, rows, nb), lambda l, j: (l, 0, j)),
        out_shape=jax.ShapeDtypeStruct((depth, rows, n3), F32),
        compiler_params=pltpu.CompilerParams(
            dimension_semantics=("arbitrary", "arbitrary"), vmem_limit_bytes=VMEM_LIMIT),
        name="modulation",
    )(c_all, w_mod, b_mod.reshape(depth, 1, n3))


def _modulate(x_ref, mod_ref):
    x = x_ref[...]
    shift = mod_ref[:, 0:1, :]
    scale = mod_ref[:, 1:2, :]
    return x, x * (1.0 + scale) + shift


def _init_conv_halo(conv_scr, st_ref, width):
    bb, _, ch = conv_scr.shape
    if st_ref is None:
        conv_scr[:, 0:HALO, :] = jnp.zeros((bb, HALO, ch), F32)
    else:
        conv_scr[:, HALO - (width - 1):HALO, :] = st_ref[...]


def _causal_conv(conv_scr, wconv_ref, width, tt, sl):
    acc = None
    for j in range(width):
        start = HALO - (width - 1) + j
        term = conv_scr[:, start:start + tt, sl] * wconv_ref[j:j + 1, sl]
        acc = term if acc is None else acc + term
    return acc


def _finish(x, out, mod_ref, lng_ref, lnb_ref, o_ref, alpha):
    bb, tt, d = x.shape
    gate = mod_ref[:, 2:3, :]
    r = alpha * x + gate * out.reshape(bb, tt, d)
    o_ref[...] = _layer_norm(r, lng_ref[...], lnb_ref[...])


def _conv_layer_kernel(*refs, has_state, width, col_w, alpha):
    if has_state:
        (x_ref, mod_ref, st_ref, win_ref, wconv_ref, wout_ref, lng_ref, lnb_ref,
         o_ref, nst_ref, conv_scr, y_scr) = refs
    else:
        (x_ref, mod_ref, win_ref, wconv_ref, wout_ref, lng_ref, lnb_ref,
         o_ref, nst_ref, conv_scr, y_scr) = refs
        st_ref = None
    bb, tt, d = x_ref.shape
    rows = bb * tt
    dc = wconv_ref.shape[1]

    @pl.when(pl.program_id(1) == 0)
    def _():
        _init_conv_halo(conv_scr, st_ref, width)

    x, u = _modulate(x_ref, mod_ref)
    u = u.reshape(rows, d).astype(BF16)

    for j in range(dc // col_w):
        sl = slice(j * col_w, (j + 1) * col_w)
        pc = _dot(u, win_ref[:, dc + j * col_w:dc + (j + 1) * col_w])
        ph = _dot(u, win_ref[:, 2 * dc + j * col_w:2 * dc + (j + 1) * col_w])
        conv_scr[:, HALO:HALO + tt, sl] = (pc * ph).reshape(bb, tt, col_w)
        conv = _causal_conv(conv_scr, wconv_ref, width, tt, sl).reshape(rows, col_w)
        pb = _dot(u, win_ref[:, sl])
        pz = _dot(u, win_ref[:, 3 * dc + j * col_w:3 * dc + (j + 1) * col_w])
        y_scr[:, sl] = (pb * conv * _silu(pz)).astype(BF16)

    out = _dot(y_scr[...], wout_ref[...])
    _finish(x, out, mod_ref, lng_ref, lnb_ref, o_ref, alpha)

    nst_ref[...] = conv_scr[:, HALO + tt - (width - 1):HALO + tt, :]
    conv_scr[:, 0:HALO, :] = conv_scr[:, tt:tt + HALO, :]


def _const_spec(shape):
    zeros = (0,) * len(shape)
    return pl.BlockSpec(shape, lambda b, t: zeros)


def _conv_layer(x, mod, state, w_in, w_conv, w_out, ln_g, ln_b, *, bb, tt, alpha):
    bsz, seq, d = x.shape
    width, dc = w_conv.shape
    has_state = state is not None
    rows = bb * tt
    seq_spec = lambda n: pl.BlockSpec((bb, n, d if n == 3 else dc), lambda b, t: (b, 0, 0))
    in_specs = [pl.BlockSpec((bb, tt, d), lambda b, t: (b, t, 0)), seq_spec(3)]
    args = [x, mod]
    if has_state:
        in_specs.append(seq_spec(width - 1))
        args.append(state)
    in_specs += [_const_spec(w_in.shape), _const_spec(w_conv.shape), _const_spec(w_out.shape),
                 _const_spec((1, d)), _const_spec((1, d))]
    args += [w_in, w_conv, w_out, ln_g.reshape(1, d), ln_b.reshape(1, d)]
    kern = functools.partial(_conv_layer_kernel, has_state=has_state, width=width,
                             col_w=min(dc, 512), alpha=alpha)
    return pl.pallas_call(
        kern,
        grid=(bsz // bb, seq // tt),
        in_specs=in_specs,
        out_specs=[pl.BlockSpec((bb, tt, d), lambda b, t: (b, t, 0)), seq_spec(width - 1)],
        out_shape=[jax.ShapeDtypeStruct((bsz, seq, d), F32),
                   jax.ShapeDtypeStruct((bsz, width - 1, dc), F32)],
        scratch_shapes=[pltpu.VMEM((bb, HALO + tt, dc), F32), pltpu.VMEM((rows, dc), BF16)],
        compiler_params=pltpu.CompilerParams(
            dimension_semantics=("arbitrary", "arbitrary"), vmem_limit_bytes=VMEM_LIMIT),
        name="conv_layer",
    )(*args)


def _tri_masks(n, seq_len):
    i = lax.broadcasted_iota(jnp.int32, (n, n), 0)
    j = lax.broadcasted_iota(jnp.int32, (n, n), 1)
    same = (i // seq_len) == (j // seq_len)
    causal = same & (i >= j)
    strict = same & (i > j)
    eye = (i == j).astype(F32)
    levels = []
    s = 1
    while s < seq_len:
        levels.append(((i // (2 * s)) == (j // (2 * s))) & ((i % (2 * s)) >= s) & ((j % (2 * s)) < s))
        s *= 2
    return causal, strict, eye, levels


def _unit_lower_inverse(a, eye, levels):
    t = eye - jnp.where(levels[0], a, 0.0)
    for m in levels[1:]:
        t = t - _dot(t, _dot(jnp.where(m, a, 0.0), t))
    return t


def _gdn_layer_kernel(*refs, has_state, width, n_heads, dk, dv, seq_len, col_w, alpha):
    if has_state:
        (x_ref, mod_ref, cst_ref, s0_ref, win_ref, wab_ref, wconv_ref, alog_ref, dtb_ref, nw_ref,
         wout_ref, lng_ref, lnb_ref, o_ref, ncst_ref, ns_ref,
         conv_scr, qkv_scr, z_scr, g_scr, beta_scr, oh_scr) = refs
        s_scr = None
    else:
        (x_ref, mod_ref, win_ref, wab_ref, wconv_ref, alog_ref, dtb_ref, nw_ref,
         wout_ref, lng_ref, lnb_ref, o_ref, ncst_ref, ns_ref,
         conv_scr, qkv_scr, z_scr, g_scr, beta_scr, oh_scr, s_scr) = refs
        cst_ref = s0_ref = None
    bb, tt, d = x_ref.shape
    rows = bb * tt
    ch = wconv_ref.shape[1]
    qk = n_heads * dk
    vw = n_heads * dv
    n = min(GDN_ROWS, rows)
    n_seq = n // seq_len
    n_chunks = rows // n

    @pl.when(pl.program_id(1) == 0)
    def _():
        _init_conv_halo(conv_scr, cst_ref, width)
        if s_scr is not None:
            s_scr[...] = jnp.zeros(s_scr.shape, F32)

    x, u = _modulate(x_ref, mod_ref)
    u = u.reshape(rows, d).astype(BF16)

    for j in range(ch // col_w):
        sl = slice(j * col_w, (j + 1) * col_w)
        conv_scr[:, HALO:HALO + tt, sl] = _dot(u, win_ref[:, sl]).reshape(bb, tt, col_w)
        conv = _causal_conv(conv_scr, wconv_ref, width, tt, sl).reshape(rows, col_w)
        qkv_scr[:, sl] = _silu(conv)
    ncst_ref[...] = conv_scr[:, HALO + tt - (width - 1):HALO + tt, :]
    conv_scr[:, 0:HALO, :] = conv_scr[:, tt:tt + HALO, :]

    z_scr[...] = _silu(_dot(u, win_ref[:, ch:ch + vw]))

    ab = _dot(u, wab_ref[...])
    lane = lax.broadcasted_iota(jnp.int32, (rows, LANES), 1)
    a = ab[:, :LANES] + dtb_ref[...]
    softplus = jnp.maximum(a, 0.0) + jnp.log(1.0 + jnp.exp(-jnp.abs(a)))
    g = jnp.where(lane < n_heads, -jnp.exp(alog_ref[...]) * softplus, 0.0)
    beta_scr[...] = jax.nn.sigmoid(ab[:, LANES:])
    ri = lax.broadcasted_iota(jnp.int32, (rows, rows), 0)
    rj = lax.broadcasted_iota(jnp.int32, (rows, rows), 1)
    span = min(seq_len, n)
    tri = (((ri // span) == (rj // span)) & (ri >= rj)).astype(BF16)
    g_hi, g_mid, g_lo = _split3(g)
    g_scr[...] = (jnp.dot(tri, g_hi, preferred_element_type=F32)
                  + jnp.dot(tri, g_mid, preferred_element_type=F32)
                  + jnp.dot(tri, g_lo, preferred_element_type=F32))

    for h in range(n_heads):
        for base, mult in ((0, dk ** -0.5), (qk, 1.0)):
            sl = slice(base + h * dk, base + (h + 1) * dk)
            t = qkv_scr[:, sl]
            qkv_scr[:, sl] = t * (lax.rsqrt(jnp.sum(t * t, axis=-1, keepdims=True) + NORM_EPS) * mult)

    causal, strict, eye, levels = _tri_masks(n, span)
    sel_r = lax.broadcasted_iota(jnp.int32, (n_heads * n, LANES), 0)
    sel_l = lax.broadcasted_iota(jnp.int32, (n_heads * n, LANES), 1)
    sel = ((sel_r // n) == sel_l).astype(BF16)
    seq_of_row = lax.broadcasted_iota(jnp.int32, (n, 1), 0) // span

    def chunk_body(c, carry):
        r0 = pl.multiple_of(c * n, n)
        rs = pl.ds(r0, n)
        gc = g_scr[rs, :]
        gh, gm, gl = _split3(gc)
        nt = lambda p: lax.dot_general(sel, p, (((1,), (1,)), ((), ())), preferred_element_type=F32)
        g_rows = nt(gh) + nt(gm) + nt(gl)
        for h in range(n_heads):
            q = qkv_scr[rs, h * dk:(h + 1) * dk]
            k = qkv_scr[rs, qk + h * dk:qk + (h + 1) * dk]
            v = qkv_scr[rs, 2 * qk + h * dv:2 * qk + (h + 1) * dv]
            g_col = g_scr[rs, h:h + 1]
            beta = beta_scr[rs, h:h + 1]
            diff = g_col - g_rows[h * n:(h + 1) * n, :]
            decay = jnp.where(causal, jnp.exp(jnp.where(causal, diff, 0.0)), 0.0)
            kb = k * beta
            e_col = jnp.exp(g_col)
            kk = _dot_nt(jnp.concatenate([kb, q], axis=0), k)
            a_mat = jnp.where(strict, kk[:n] * decay, 0.0)
            attn = kk[n:] * decay
            t_inv = _unit_lower_inverse(a_mat, eye, levels)
            sol = _dot(t_inv, jnp.concatenate([v * beta, kb * e_col], axis=1))
            u_c, w_c = sol[:, :dv], sol[:, dv:]
            qg = q * e_col
            g_end = jnp.concatenate(
                [jnp.broadcast_to(g_col[(s + 1) * span - 1:(s + 1) * span, :], (span, 1))
                 for s in range(n_seq)], axis=0)
            to_end = jnp.exp(g_end - g_col)
            state_terms = []
            for s in range(n_seq):
                srows = slice(s * span, (s + 1) * span)
                if has_state:
                    s_prev = s0_ref[c * n_seq + s, h]
                else:
                    s_prev = s_scr[h]
                ws = _dot(jnp.concatenate([w_c[srows], qg[srows]], axis=0), s_prev)
                state_terms.append((s_prev, ws))
            v_new = u_c - jnp.concatenate([ws[:span] for _, ws in state_terms], axis=0)
            o_c = (jnp.concatenate([ws[span:] for _, ws in state_terms], axis=0)
                   + _dot(attn, v_new))
            oh_scr[rs, h * dv:(h + 1) * dv] = o_c
            v_scaled = v_new * to_end
            for s in range(n_seq):
                s_prev = state_terms[s][0]
                vs = v_scaled if n_seq == 1 else jnp.where(seq_of_row == s, v_scaled, 0.0)
                g_last = g_col[(s + 1) * span - 1:(s + 1) * span, :]
                s_new = s_prev * jnp.exp(g_last) + _dot_tn(k, vs)
                if has_state:
                    ns_ref[c * n_seq + s, h] = s_new
                else:
                    s_scr[h] = s_new
        return carry

    lax.fori_loop(0, n_chunks, chunk_body, 0)
    if not has_state:
        ns_ref[0] = s_scr[...]

    for h in range(n_heads):
        sl = slice(h * dv, (h + 1) * dv)
        o = oh_scr[:, sl]
        o = o * lax.rsqrt(jnp.mean(o * o, axis=-1, keepdims=True) + NORM_EPS) * nw_ref[...]
        oh_scr[:, sl] = o * z_scr[:, sl]
    out = _dot(oh_scr[...], wout_ref[...])
    _finish(x, out, mod_ref, lng_ref, lnb_ref, o_ref, alpha)


def _gdn_layer(x, mod, conv_state, ssm_state, w_in, w_ab, w_conv, a_log, dt_bias, norm_w, w_out,
               ln_g, ln_b, *, bb, tt, alpha):
    bsz, seq, d = x.shape
    width, ch = w_conv.shape
    n_heads = a_log.shape[0]
    vw = w_out.shape[0]
    dv = vw // n_heads
    dk = (ch - vw) // 2 // n_heads
    has_state = conv_state is not None
    rows = bb * tt
    seq_len = min(seq, GDN_ROWS)
    assert rows % min(GDN_ROWS, rows) == 0 and min(GDN_ROWS, rows) % seq_len == 0
    assert has_state == (tt == seq), "a carried state needs whole sequences per block"

    pad = lambda vec: jnp.zeros((1, LANES), F32).at[0, :n_heads].set(vec)
    in_specs = [pl.BlockSpec((bb, tt, d), lambda b, t: (b, t, 0)),
                pl.BlockSpec((bb, 3, d), lambda b, t: (b, 0, 0))]
    args = [x, mod]
    if has_state:
        in_specs += [pl.BlockSpec((bb, width - 1, ch), lambda b, t: (b, 0, 0)),
                     pl.BlockSpec((bb, n_heads, dk, dv), lambda b, t: (b, 0, 0, 0))]
        args += [conv_state, ssm_state]
    in_specs += [_const_spec(w_in.shape), _const_spec(w_ab.shape), _const_spec(w_conv.shape),
                 _const_spec((1, LANES)), _const_spec((1, LANES)), _const_spec((1, dv)),
                 _const_spec(w_out.shape), _const_spec((1, d)), _const_spec((1, d))]
    args += [w_in, w_ab, w_conv, pad(a_log), pad(dt_bias), norm_w.reshape(1, dv), w_out,
             ln_g.reshape(1, d), ln_b.reshape(1, d)]
    scratch = [pltpu.VMEM((bb, HALO + tt, ch), F32), pltpu.VMEM((rows, ch), F32),
               pltpu.VMEM((rows, vw), F32), pltpu.VMEM((rows, LANES), F32),
               pltpu.VMEM((rows, LANES), F32), pltpu.VMEM((rows, vw), F32)]
    if not has_state:
        scratch.append(pltpu.VMEM((n_heads, dk, dv), F32))
    kern = functools.partial(_gdn_layer_kernel, has_state=has_state, width=width, n_heads=n_heads,
                             dk=dk, dv=dv, seq_len=seq_len, col_w=min(ch, 512), alpha=alpha)
    return pl.pallas_call(
        kern,
        grid=(bsz // bb, seq // tt),
        in_specs=in_specs,
        out_specs=[pl.BlockSpec((bb, tt, d), lambda b, t: (b, t, 0)),
                   pl.BlockSpec((bb, width - 1, ch), lambda b, t: (b, 0, 0)),
                   pl.BlockSpec((bb, n_heads, dk, dv), lambda b, t: (b, 0, 0, 0))],
        out_shape=[jax.ShapeDtypeStruct((bsz, seq, d), F32),
                   jax.ShapeDtypeStruct((bsz, width - 1, ch), F32),
                   jax.ShapeDtypeStruct((bsz, n_heads, dk, dv), F32)],
        scratch_shapes=scratch,
        compiler_params=pltpu.CompilerParams(
            dimension_semantics=("arbitrary", "arbitrary"), vmem_limit_bytes=VMEM_LIMIT),
        name="gdn_layer",
    )(*args)


def _block_rows(bsz, seq, target):
    if seq >= target:
        return 1, target
    return max(1, min(bsz, target // seq)), seq


def _trunk(x, mod, conv_a, conv_b, ssm_b, ln_g, ln_b, wa, wb, alpha):
    bsz, seq, _ = x.shape
    new_a, new_cb, new_s = [], [], []
    depth = mod.shape[0]
    for l in range(depth):
        i = l // 2
        if l % 2 == 0:
            bb, tt = _block_rows(bsz, seq, 512)
            x, nb = _conv_layer(x, mod[l], None if conv_a is None else conv_a[i],
                                wa["in"][i], wa["conv"][i], wa["out"][i], ln_g[l], ln_b[l],
                                bb=bb, tt=tt, alpha=alpha)
            new_a.append(nb)
        else:
            bb, tt = _block_rows(bsz, seq, 256 if conv_b is None else GDN_ROWS)
            x, nb, s = _gdn_layer(x, mod[l], None if conv_b is None else conv_b[i],
                                  None if ssm_b is None else ssm_b[i],
                                  wb["in"][i], wb["ab"][i], wb["conv"][i], wb["a_log"][i],
                                  wb["dt_bias"][i], wb["norm"][i], wb["out"][i], ln_g[l], ln_b[l],
                                  bb=bb, tt=tt, alpha=alpha)
            new_cb.append(nb)
            new_s.append(s)
    return x, jnp.stack(new_a), jnp.stack(new_cb), jnp.stack(new_s)


def kernel(x_prompt, x_sample, state_conv_a, state_conv_b, state_ssm_b, c_prompt, c_sample, w_mod, b_mod, ln_g, ln_b, wa_in, wa_conv, wa_out, wb_in, wb_conv, wb_a_log, wb_dt_bias, wb_norm, wb_out):
    depth, d, _ = w_mod.shape
    alpha = (2 * depth) ** 0.25
    bp = x_prompt.shape[0]
    n_heads = wb_a_log.shape[1]
    ch = wb_conv.shape[2]
    vw = wb_out.shape[1]

    mod = _modulation(jnp.concatenate([c_prompt, c_sample], axis=0), w_mod, b_mod)
    mod = mod.reshape(depth, -1, 3, d)
    mod_p, mod_s = mod[:, :bp], mod[:, bp:]

    n_gdn = wb_in.shape[0]
    w_ab = jnp.zeros((n_gdn, d, 2 * LANES), F32)
    w_ab = w_ab.at[:, :, :n_heads].set(wb_in[:, :, ch + vw + n_heads:])
    w_ab = w_ab.at[:, :, LANES:LANES + n_heads].set(wb_in[:, :, ch + vw:ch + vw + n_heads])
    wa = {"in": wa_in.astype(BF16), "conv": wa_conv, "out": wa_out.astype(BF16)}
    wb = {"in": wb_in[:, :, :ch + vw].astype(BF16), "ab": w_ab.astype(BF16), "conv": wb_conv,
          "a_log": wb_a_log, "dt_bias": wb_dt_bias, "norm": wb_norm, "out": wb_out.astype(BF16)}

    y_p, ca_p, cb_p, s_p = _trunk(x_prompt, mod_p, None, None, None, ln_g, ln_b, wa, wb, alpha)
    y_s, ca_s, cb_s, s_s = _trunk(x_sample, mod_s, state_conv_a, state_conv_b, state_ssm_b,
                                  ln_g, ln_b, wa, wb, alpha)
    return (y_p, y_s, ca_p, cb_p, s_p, ca_s, cb_s, s_s)
```

```python
import functools

import jax
import jax.numpy as jnp
from jax import lax
from jax.experimental import pallas as pl
from jax.experimental.pallas import tpu as pltpu

F32 = jnp.float32
BF16 = jnp.bfloat16

LN_EPS = 1e-5
NORM_EPS = 1e-6
SUBLANES = 8
LANES = 128
HALO = SUBLANES
GDN_ROWS = 64
VMEM_LIMIT = 56 * 1024 * 1024


def _dot(a, b):
    return jnp.dot(a.astype(BF16), b.astype(BF16), preferred_element_type=F32)


def _dot_nt(a, b):
    return lax.dot_general(a.astype(BF16), b.astype(BF16), (((1,), (1,)), ((), ())),
                           preferred_element_type=F32)


def _dot_tn(a, b):
    return lax.dot_general(a.astype(BF16), b.astype(BF16), (((0,), (0,)), ((), ())),
                           preferred_element_type=F32)


def _silu(x):
    return x * jax.nn.sigmoid(x)


def _split3(x):
    hi = x.astype(BF16)
    r = x - hi.astype(F32)
    mid = r.astype(BF16)
    lo = (r - mid.astype(F32)).astype(BF16)
    return hi, mid, lo


def _layer_norm(r, g, b):
    mu = jnp.mean(r, axis=-1, keepdims=True)
    d = r - mu
    var = jnp.mean(d * d, axis=-1, keepdims=True)
    return d * lax.rsqrt(var + LN_EPS) * g + b


def _mod_kernel(c_ref, w_ref, b_ref, o_ref):
    cs = _silu(c_ref[...])
    o_ref[0] = _dot(cs, w_ref[0]) + b_ref[0]


def _modulation(c_all, w_mod, b_mod):
    depth, d, n3 = w_mod.shape
    rows = c_all.shape[0]
    nb = d
    return pl.pallas_call(
        _mod_kernel,
        grid=(depth, n3 // nb),
        in_specs=[
            pl.BlockSpec((rows, d), lambda l, j: (0, 0)),
            pl.BlockSpec((1, d, nb), lambda l, j: (l, 0, j)),
            pl.BlockSpec((1, 1, nb), lambda l, j: (l, 0, j)),
        ],
        out_specs=pl.BlockSpec((1, rows, nb), lambda l, j: (l, 0, j)),
        out_shape=jax.ShapeDtypeStruct((depth, rows, n3), F32),
        compiler_params=pltpu.CompilerParams(
            dimension_semantics=("arbitrary", "arbitrary"), vmem_limit_bytes=VMEM_LIMIT),
        name="modulation",
    )(c_all, w_mod, b_mod.reshape(depth, 1, n3))


def _modulate(x_ref, mod_ref):
    x = x_ref[...]
    shift = mod_ref[:, 0:1, :]
    scale = mod_ref[:, 1:2, :]
    return x, x * (1.0 + scale) + shift


def _init_conv_halo(conv_scr, st_ref, width):
    bb, _, ch = conv_scr.shape
    if st_ref is None:
        conv_scr[:, 0:HALO, :] = jnp.zeros((bb, HALO, ch), F32)
    else:
        conv_scr[:, HALO - (width - 1):HALO, :] = st_ref[...]


def _causal_conv(conv_scr, wconv_ref, width, tt, sl):
    acc = None
    for j in range(width):
        start = HALO - (width - 1) + j
        term = conv_scr[:, start:start + tt, sl] * wconv_ref[j:j + 1, sl]
        acc = term if acc is None else acc + term
    return acc


def _finish(x, out, mod_ref, lng_ref, lnb_ref, o_ref, alpha):
    bb, tt, d = x.shape
    gate = mod_ref[:, 2:3, :]
    r = alpha * x + gate * out.reshape(bb, tt, d)
    o_ref[...] = _layer_norm(r, lng_ref[...], lnb_ref[...])


def _conv_layer_kernel(*refs, has_state, width, col_w, alpha):
    if has_state:
        (x_ref, mod_ref, st_ref, win_ref, wconv_ref, wout_ref, lng_ref, lnb_ref,
         o_ref, nst_ref, conv_scr, y_scr) = refs
    else:
        (x_ref, mod_ref, win_ref, wconv_ref, wout_ref, lng_ref, lnb_ref,
         o_ref, nst_ref, conv_scr, y_scr) = refs
        st_ref = None
    bb, tt, d = x_ref.shape
    rows = bb * tt
    dc = wconv_ref.shape[1]

    @pl.when(pl.program_id(1) == 0)
    def _():
        _init_conv_halo(conv_scr, st_ref, width)

    x, u = _modulate(x_ref, mod_ref)
    u = u.reshape(rows, d).astype(BF16)

    for j in range(dc // col_w):
        sl = slice(j * col_w, (j + 1) * col_w)
        pc = _dot(u, win_ref[:, dc + j * col_w:dc + (j + 1) * col_w])
        ph = _dot(u, win_ref[:, 2 * dc + j * col_w:2 * dc + (j + 1) * col_w])
        conv_scr[:, HALO:HALO + tt, sl] = (pc * ph).reshape(bb, tt, col_w)
        conv = _causal_conv(conv_scr, wconv_ref, width, tt, sl).reshape(rows, col_w)
        pb = _dot(u, win_ref[:, sl])
        pz = _dot(u, win_ref[:, 3 * dc + j * col_w:3 * dc + (j + 1) * col_w])
        y_scr[:, sl] = (pb * conv * _silu(pz)).astype(BF16)

    out = _dot(y_scr[...], wout_ref[...])
    _finish(x, out, mod_ref, lng_ref, lnb_ref, o_ref, alpha)

    nst_ref[...] = conv_scr[:, HALO + tt - (width - 1):HALO + tt, :]
    conv_scr[:, 0:HALO, :] = conv_scr[:, tt:tt + HALO, :]


def _const_spec(shape):
    zeros = (0,) * len(shape)
    return pl.BlockSpec(shape, lambda b, t: zeros)


def _conv_layer(x, mod, state, w_in, w_conv, w_out, ln_g, ln_b, *, bb, tt, alpha):
    bsz, seq, d = x.shape
    width, dc = w_conv.shape
    has_state = state is not None
    rows = bb * tt
    seq_spec = lambda n: pl.BlockSpec((bb, n, d if n == 3 else dc), lambda b, t: (b, 0, 0))
    in_specs = [pl.BlockSpec((bb, tt, d), lambda b, t: (b, t, 0)), seq_spec(3)]
    args = [x, mod]
    if has_state:
        in_specs.append(seq_spec(width - 1))
        args.append(state)
    in_specs += [_const_spec(w_in.shape), _const_spec(w_conv.shape), _const_spec(w_out.shape),
                 _const_spec((1, d)), _const_spec((1, d))]
    args += [w_in, w_conv, w_out, ln_g.reshape(1, d), ln_b.reshape(1, d)]
    kern = functools.partial(_conv_layer_kernel, has_state=has_state, width=width,
                             col_w=min(dc, 512), alpha=alpha)
    return pl.pallas_call(
        kern,
        grid=(bsz // bb, seq // tt),
        in_specs=in_specs,
        out_specs=[pl.BlockSpec((bb, tt, d), lambda b, t: (b, t, 0)), seq_spec(width - 1)],
        out_shape=[jax.ShapeDtypeStruct((bsz, seq, d), F32),
                   jax.ShapeDtypeStruct((bsz, width - 1, dc), F32)],
        scratch_shapes=[pltpu.VMEM((bb, HALO + tt, dc), F32), pltpu.VMEM((rows, dc), BF16)],
        compiler_params=pltpu.CompilerParams(
            dimension_semantics=("arbitrary", "arbitrary"), vmem_limit_bytes=VMEM_LIMIT),
        name="conv_layer",
    )(*args)


def _tri_masks(n, seq_len):
    i = lax.broadcasted_iota(jnp.int32, (n, n), 0)
    j = lax.broadcasted_iota(jnp.int32, (n, n), 1)
    same = (i // seq_len) == (j // seq_len)
    causal = same & (i >= j)
    strict = same & (i > j)
    eye = (i == j).astype(F32)
    levels = []
    s = 1
    while s < seq_len:
        levels.append(((i // (2 * s)) == (j // (2 * s))) & ((i % (2 * s)) >= s) & ((j % (2 * s)) < s))
        s *= 2
    return causal, strict, eye, levels


def _gdn_layer_kernel(*refs, has_state, width, n_heads, dk, dv, seq_len, col_w, group, alpha):
    if has_state:
        (x_ref, mod_ref, cst_ref, s0_ref, win_ref, wab_ref, wconv_ref, alog_ref, dtb_ref, nw_ref,
         wout_ref, lng_ref, lnb_ref, o_ref, ncst_ref, ns_ref,
         conv_scr, qkv_scr, z_scr, g_scr, gx_scr, kb_scr, u_scr, wq_scr, attn_scr, oh_scr) = refs
        s_scr = None
    else:
        (x_ref, mod_ref, win_ref, wab_ref, wconv_ref, alog_ref, dtb_ref, nw_ref,
         wout_ref, lng_ref, lnb_ref, o_ref, ncst_ref, ns_ref,
         conv_scr, qkv_scr, z_scr, g_scr, gx_scr, kb_scr, u_scr, wq_scr, attn_scr, oh_scr,
         s_scr) = refs
        cst_ref = s0_ref = None
    bb, tt, d = x_ref.shape
    rows = bb * tt
    ch = wconv_ref.shape[1]
    qk = n_heads * dk
    vw = n_heads * dv
    n = min(GDN_ROWS, rows)
    span = min(seq_len, n)
    n_seq = n // span
    n_chunks = rows // n
    heads = range(n_heads)
    q_sl = lambda h: slice(h * dk, (h + 1) * dk)
    k_sl = lambda h: slice(qk + h * dk, qk + (h + 1) * dk)
    v_sl = lambda h: slice(2 * qk + h * dv, 2 * qk + (h + 1) * dv)
    o_sl = lambda h: slice(h * dv, (h + 1) * dv)

    @pl.when(pl.program_id(1) == 0)
    def _():
        _init_conv_halo(conv_scr, cst_ref, width)
        if s_scr is not None:
            s_scr[...] = jnp.zeros(s_scr.shape, F32)

    x, u = _modulate(x_ref, mod_ref)
    u = u.reshape(rows, d).astype(BF16)

    for j in range(ch // col_w):
        sl = slice(j * col_w, (j + 1) * col_w)
        conv_scr[:, HALO:HALO + tt, sl] = _dot(u, win_ref[:, sl]).reshape(bb, tt, col_w)
        conv = _causal_conv(conv_scr, wconv_ref, width, tt, sl).reshape(rows, col_w)
        qkv_scr[:, sl] = _silu(conv)
    ncst_ref[...] = conv_scr[:, HALO + tt - (width - 1):HALO + tt, :]
    conv_scr[:, 0:HALO, :] = conv_scr[:, tt:tt + HALO, :]

    z_scr[...] = _silu(_dot(u, win_ref[:, ch:ch + vw]))

    ab = _dot(u, wab_ref[...])
    lane = lax.broadcasted_iota(jnp.int32, (rows, LANES), 1)
    a = ab[:, :LANES] + dtb_ref[...]
    softplus = jnp.maximum(a, 0.0) + jnp.log(1.0 + jnp.exp(-jnp.abs(a)))
    g = jnp.where(lane < n_heads, -jnp.exp(alog_ref[...]) * softplus, 0.0)
    beta = jax.nn.sigmoid(ab[:, LANES:])
    ri = lax.broadcasted_iota(jnp.int32, (rows, rows), 0)
    rj = lax.broadcasted_iota(jnp.int32, (rows, rows), 1)
    tri = (((ri // span) == (rj // span)) & (ri >= rj)).astype(BF16)
    g_hi, g_mid, g_lo = _split3(g)
    g_cum = (jnp.dot(tri, g_hi, preferred_element_type=F32)
             + jnp.dot(tri, g_mid, preferred_element_type=F32)
             + jnp.dot(tri, g_lo, preferred_element_type=F32))
    g_scr[...] = g_cum

    for h in heads:
        t = qkv_scr[:, q_sl(h)]
        qkv_scr[:, q_sl(h)] = t * (lax.rsqrt(jnp.sum(t * t, axis=-1, keepdims=True) + NORM_EPS)
                                   * dk ** -0.5)
        t = qkv_scr[:, k_sl(h)]
        t = t * lax.rsqrt(jnp.sum(t * t, axis=-1, keepdims=True) + NORM_EPS)
        qkv_scr[:, k_sl(h)] = t
        beta_h = beta[:, h:h + 1]
        kb_scr[:, o_sl(h)] = t * beta_h
        qkv_scr[:, v_sl(h)] = qkv_scr[:, v_sl(h)] * beta_h
        gx_scr[:, o_sl(h)] = jnp.broadcast_to(g_cum[:, h:h + 1], (rows, dk))

    causal, strict, eye, levels = _tri_masks(n, span)
    sel_r = lax.broadcasted_iota(jnp.int32, (n_heads * n, LANES), 0)
    sel_l = lax.broadcasted_iota(jnp.int32, (n_heads * n, LANES), 1)
    sel = ((sel_r // n) == sel_l).astype(BF16)
    row_seq = lax.broadcasted_iota(jnp.int32, (n, 1), 0) // span
    row_seq2 = (lax.broadcasted_iota(jnp.int32, (2 * n, 1), 0) % n) // span

    def local_body(gi, carry):
        items = []
        for ci in range(group):
            c = gi * group + ci
            rs = pl.ds(pl.multiple_of(c * n, n), n)
            gh, gm, gl = _split3(g_scr[rs, :])
            nt = lambda p: lax.dot_general(sel, p, (((1,), (1,)), ((), ())),
                                           preferred_element_type=F32)
            g_rows = nt(gh) + nt(gm) + nt(gl)
            for h in heads:
                items.append((c, rs, h, g_rows[h * n:(h + 1) * n, :]))
        decay, a_mat, t_inv = [], [], []
        for c, rs, h, g_row in items:
            diff = gx_scr[rs, h * dv:h * dv + n] - g_row
            decay.append(jnp.where(causal, jnp.exp(jnp.where(causal, diff, 0.0)), 0.0))
        for (c, rs, h, _), dec in zip(items, decay):
            k = qkv_scr[rs, k_sl(h)]
            a_mat.append(jnp.where(strict, _dot_nt(kb_scr[rs, o_sl(h)], k) * dec, 0.0))
        for (c, rs, h, _), dec in zip(items, decay):
            attn_scr[rs, h * dv:h * dv + n] = (
                _dot_nt(qkv_scr[rs, q_sl(h)], qkv_scr[rs, k_sl(h)]) * dec)
        t_inv = [eye - jnp.where(levels[0], a, 0.0) for a in a_mat]
        for m in levels[1:]:
            y = [_dot(jnp.where(m, a, 0.0), t) for a, t in zip(a_mat, t_inv)]
            xs = [_dot(t, yy) for t, yy in zip(t_inv, y)]
            t_inv = [t - xx for t, xx in zip(t_inv, xs)]
        for (c, rs, h, _), t in zip(items, t_inv):
            e_x = jnp.exp(gx_scr[rs, o_sl(h)])
            rhs = jnp.concatenate([qkv_scr[rs, v_sl(h)], kb_scr[rs, o_sl(h)] * e_x], axis=1)
            sol = _dot(t, rhs)
            u_scr[rs, o_sl(h)] = sol[:, :dv]
            wq_scr[c, 0:n, o_sl(h)] = sol[:, dv:]
            wq_scr[c, n:2 * n, o_sl(h)] = qkv_scr[rs, q_sl(h)] * e_x
        return carry

    lax.fori_loop(0, n_chunks // group, local_body, 0)

    def state_body(c, carry):
        rs = pl.ds(pl.multiple_of(c * n, n), n)
        pairs = [(h, s) for h in heads for s in range(n_seq)]
        masks = [None if n_seq == 1 else (row_seq == s) for s in range(n_seq)]
        masks2 = [None if n_seq == 1 else (row_seq2 == s) for s in range(n_seq)]
        if has_state:
            s_prev = {(h, s): s0_ref[c * n_seq + s, h] for h, s in pairs}
        else:
            s_prev = {(h, 0): s_scr[h] for h in heads}
        ws = {(h, s): _dot(wq_scr[c, :, o_sl(h)], s_prev[(h, s)]) for h, s in pairs}
        v_new, v_dec, o_part, g_last = {}, {}, {}, {}
        for h in heads:
            acc = ws[(h, 0)] if n_seq == 1 else sum(
                jnp.where(masks2[s], ws[(h, s)], 0.0) for s in range(n_seq))
            v_new[h] = u_scr[rs, o_sl(h)] - acc[:n]
            o_part[h] = acc[n:]
            gx = gx_scr[rs, o_sl(h)]
            for s in range(n_seq):
                g_last[(h, s)] = gx[(s + 1) * span - 1:(s + 1) * span, :]
            g_end = jnp.concatenate(
                [jnp.broadcast_to(g_last[(h, s)], (span, dv)) for s in range(n_seq)], axis=0)
            v_dec[h] = v_new[h] * jnp.exp(g_end - gx)
        for h in heads:
            oh_scr[rs, o_sl(h)] = o_part[h] + _dot(attn_scr[rs, h * dv:h * dv + n], v_new[h])
        for h, s in pairs:
            vs = v_dec[h] if n_seq == 1 else jnp.where(masks[s], v_dec[h], 0.0)
            s_new = (s_prev[(h, s)] * jnp.exp(g_last[(h, s)])
                     + _dot_tn(qkv_scr[rs, k_sl(h)], vs))
            if has_state:
                ns_ref[c * n_seq + s, h] = s_new
            else:
                s_scr[h] = s_new
        return carry

    lax.fori_loop(0, n_chunks, state_body, 0)
    if not has_state:
        ns_ref[0] = s_scr[...]

    for h in heads:
        o = oh_scr[:, o_sl(h)]
        o = o * lax.rsqrt(jnp.mean(o * o, axis=-1, keepdims=True) + NORM_EPS) * nw_ref[...]
        oh_scr[:, o_sl(h)] = o * z_scr[:, o_sl(h)]
    out = _dot(oh_scr[...], wout_ref[...])
    _finish(x, out, mod_ref, lng_ref, lnb_ref, o_ref, alpha)


def _gdn_layer(x, mod, conv_state, ssm_state, w_in, w_ab, w_conv, a_log, dt_bias, norm_w, w_out,
               ln_g, ln_b, *, bb, tt, alpha):
    bsz, seq, d = x.shape
    width, ch = w_conv.shape
    n_heads = a_log.shape[0]
    vw = w_out.shape[0]
    dv = vw // n_heads
    dk = (ch - vw) // 2 // n_heads
    qk = n_heads * dk
    has_state = conv_state is not None
    rows = bb * tt
    seq_len = min(seq, GDN_ROWS)
    n = min(GDN_ROWS, rows)
    n_chunks = rows // n
    assert dk == dv == LANES, "one head per lane tile"
    assert rows % n == 0 and n % seq_len == 0
    assert has_state == (tt == seq), "a carried state needs whole sequences per block"
    group = 2 if n_chunks % 2 == 0 else 1

    pad = lambda vec: jnp.zeros((1, LANES), F32).at[0, :n_heads].set(vec)
    in_specs = [pl.BlockSpec((bb, tt, d), lambda b, t: (b, t, 0)),
                pl.BlockSpec((bb, 3, d), lambda b, t: (b, 0, 0))]
    args = [x, mod]
    if has_state:
        in_specs += [pl.BlockSpec((bb, width - 1, ch), lambda b, t: (b, 0, 0)),
                     pl.BlockSpec((bb, n_heads, dk, dv), lambda b, t: (b, 0, 0, 0))]
        args += [conv_state, ssm_state]
    in_specs += [_const_spec(w_in.shape), _const_spec(w_ab.shape), _const_spec(w_conv.shape),
                 _const_spec((1, LANES)), _const_spec((1, LANES)), _const_spec((1, dv)),
                 _const_spec(w_out.shape), _const_spec((1, d)), _const_spec((1, d))]
    args += [w_in, w_ab, w_conv, pad(a_log), pad(dt_bias), norm_w.reshape(1, dv), w_out,
             ln_g.reshape(1, d), ln_b.reshape(1, d)]
    scratch = [pltpu.VMEM((bb, HALO + tt, ch), F32),
               pltpu.VMEM((rows, ch), F32),
               pltpu.VMEM((rows, vw), F32),
               pltpu.VMEM((rows, LANES), F32),
               pltpu.VMEM((rows, vw), F32),
               pltpu.VMEM((rows, qk), F32),
               pltpu.VMEM((rows, vw), F32),
               pltpu.VMEM((n_chunks, 2 * n, qk), F32),
               pltpu.VMEM((rows, vw), F32),
               pltpu.VMEM((rows, vw), F32)]
    if not has_state:
        scratch.append(pltpu.VMEM((n_heads, dk, dv), F32))
    kern = functools.partial(_gdn_layer_kernel, has_state=has_state, width=width, n_heads=n_heads,
                             dk=dk, dv=dv, seq_len=seq_len, col_w=min(ch, 512), group=group,
                             alpha=alpha)
    return pl.pallas_call(
        kern,
        grid=(bsz // bb, seq // tt),
        in_specs=in_specs,
        out_specs=[pl.BlockSpec((bb, tt, d), lambda b, t: (b, t, 0)),
                   pl.BlockSpec((bb, width - 1, ch), lambda b, t: (b, 0, 0)),
                   pl.BlockSpec((bb, n_heads, dk, dv), lambda b, t: (b, 0, 0, 0))],
        out_shape=[jax.ShapeDtypeStruct((bsz, seq, d), F32),
                   jax.ShapeDtypeStruct((bsz, width - 1, ch), F32),
                   jax.ShapeDtypeStruct((bsz, n_heads, dk, dv), F32)],
        scratch_shapes=scratch,
        compiler_params=pltpu.CompilerParams(
            dimension_semantics=("arbitrary", "arbitrary"), vmem_limit_bytes=VMEM_LIMIT),
        name="gdn_layer",
    )(*args)


def _block_rows(bsz, seq, target):
    if seq >= target:
        return 1, target
    return max(1, min(bsz, target // seq)), seq


def _trunk(x, mod, conv_a, conv_b, ssm_b, ln_g, ln_b, wa, wb, alpha):
    bsz, seq, _ = x.shape
    new_a, new_cb, new_s = [], [], []
    depth = mod.shape[0]
    for l in range(depth):
        i = l // 2
        if l % 2 == 0:
            bb, tt = _block_rows(bsz, seq, 512)
            x, nb = _conv_layer(x, mod[l], None if conv_a is None else conv_a[i],
                                wa["in"][i], wa["conv"][i], wa["out"][i], ln_g[l], ln_b[l],
                                bb=bb, tt=tt, alpha=alpha)
            new_a.append(nb)
        else:
            bb, tt = _block_rows(bsz, seq, 256 if conv_b is None else GDN_ROWS)
            x, nb, s = _gdn_layer(x, mod[l], None if conv_b is None else conv_b[i],
                                  None if ssm_b is None else ssm_b[i],
                                  wb["in"][i], wb["ab"][i], wb["conv"][i], wb["a_log"][i],
                                  wb["dt_bias"][i], wb["norm"][i], wb["out"][i], ln_g[l], ln_b[l],
                                  bb=bb, tt=tt, alpha=alpha)
            new_cb.append(nb)
            new_s.append(s)
    return x, jnp.stack(new_a), jnp.stack(new_cb), jnp.stack(new_s)


def kernel(x_prompt, x_sample, state_conv_a, state_conv_b, state_ssm_b, c_prompt, c_sample, w_mod, b_mod, ln_g, ln_b, wa_in, wa_conv, wa_out, wb_in, wb_conv, wb_a_log, wb_dt_bias, wb_norm, wb_out):
    depth, d, _ = w_mod.shape
    alpha = (2 * depth) ** 0.25
    bp = x_prompt.shape[0]
    n_heads = wb_a_log.shape[1]
    ch = wb_conv.shape[2]
    vw = wb_out.shape[1]

    mod = _modulation(jnp.concatenate([c_prompt, c_sample], axis=0), w_mod, b_mod)
    mod = mod.reshape(depth, -1, 3, d)
    mod_p, mod_s = mod[:, :bp], mod[:, bp:]

    n_gdn = wb_in.shape[0]
    w_ab = jnp.zeros((n_gdn, d, 2 * LANES), F32)
    w_ab = w_ab.at[:, :, :n_heads].set(wb_in[:, :, ch + vw + n_heads:])
    w_ab = w_ab.at[:, :, LANES:LANES + n_heads].set(wb_in[:, :, ch + vw:ch + vw + n_heads])
    wa = {"in": wa_in.astype(BF16), "conv": wa_conv, "out": wa_out.astype(BF16)}
    wb = {"in": wb_in[:, :, :ch + vw].astype(BF16), "ab": w_ab.astype(BF16), "conv": wb_conv,
          "a_log": wb_a_log, "dt_bias": wb_dt_bias, "norm": wb_norm, "out": wb_out.astype(BF16)}

    y_p, ca_p, cb_p, s_p = _trunk(x_prompt, mod_p, None, None, None, ln_g, ln_b, wa, wb, alpha)
    y_s, ca_s, cb_s, s_s = _trunk(x_sample, mod_s, state_conv_a, state_conv_b, state_ssm_b,
                                  ln_g, ln_b, wa, wb, alpha)
    return (y_p, y_s, ca_p, cb_p, s_p, ca_s, cb_s, s_s)
```

```python
import functools

import jax
import jax.numpy as jnp
from jax import lax
from jax.experimental import pallas as pl
from jax.experimental.pallas import tpu as pltpu

F32 = jnp.float32
BF16 = jnp.bfloat16

LN_EPS = 1e-5
NORM_EPS = 1e-6
SUBLANES = 8
LANES = 128
HALO = SUBLANES
GDN_ROWS = 64
VMEM_LIMIT = 56 * 1024 * 1024


def _dot(a, b):
    return jnp.dot(a.astype(BF16), b.astype(BF16), preferred_element_type=F32)


def _dot_nt(a, b):
    return lax.dot_general(a.astype(BF16), b.astype(BF16), (((1,), (1,)), ((), ())),
                           preferred_element_type=F32)


def _dot_tn(a, b):
    return lax.dot_general(a.astype(BF16), b.astype(BF16), (((0,), (0,)), ((), ())),
                           preferred_element_type=F32)


def _silu(x):
    return x * jax.nn.sigmoid(x)


def _split3(x):
    hi = x.astype(BF16)
    r = x - hi.astype(F32)
    mid = r.astype(BF16)
    lo = (r - mid.astype(F32)).astype(BF16)
    return hi, mid, lo


def _layer_norm(r, g, b):
    mu = jnp.mean(r, axis=-1, keepdims=True)
    d = r - mu
    var = jnp.mean(d * d, axis=-1, keepdims=True)
    return d * lax.rsqrt(var + LN_EPS) * g + b


def _mod_kernel(c_ref, w_ref, b_ref, o_ref):
    cs = _silu(c_ref[...])
    o_ref[0] = _dot(cs, w_ref[0]) + b_ref[0]


def _modulation(c_all, w_mod, b_mod):
    depth, d, n3 = w_mod.shape
    rows = c_all.shape[0]
    nb = d
    return pl.pallas_call(
        _mod_kernel,
        grid=(depth, n3 // nb),
        in_specs=[
            pl.BlockSpec((rows, d), lambda l, j: (0, 0)),
            pl.BlockSpec((1, d, nb), lambda l, j: (l, 0, j)),
            pl.BlockSpec((1, 1, nb), lambda l, j: (l, 0, j)),
        ],
        out_specs=pl.BlockSpec((1, rows, nb), lambda l, j: (l, 0, j)),
        out_shape=jax.ShapeDtypeStruct((depth, rows, n3), F32),
        compiler_params=pltpu.CompilerParams(
            dimension_semantics=("arbitrary", "arbitrary"), vmem_limit_bytes=VMEM_LIMIT),
        name="modulation",
    )(c_all, w_mod, b_mod.reshape(depth, 1, n3))


def _modulate(x_ref, mod_ref):
    x = x_ref[...]
    shift = mod_ref[:, 0:1, :]
    scale = mod_ref[:, 1:2, :]
    return x, x * (1.0 + scale) + shift


def _init_conv_halo(conv_scr, st_ref, width):
    bb, _, ch = conv_scr.shape
    if st_ref is None:
        conv_scr[:, 0:HALO, :] = jnp.zeros((bb, HALO, ch), F32)
    else:
        conv_scr[:, HALO - (width - 1):HALO, :] = st_ref[...]


def _causal_conv(conv_scr, wconv_ref, width, tt, sl):
    acc = None
    for j in range(width):
        start = HALO - (width - 1) + j
        term = conv_scr[:, start:start + tt, sl] * wconv_ref[j:j + 1, sl]
        acc = term if acc is None else acc + term
    return acc


def _finish(x, out, mod_ref, lng_ref, lnb_ref, o_ref, alpha):
    bb, tt, d = x.shape
    gate = mod_ref[:, 2:3, :]
    r = alpha * x + gate * out.reshape(bb, tt, d)
    o_ref[...] = _layer_norm(r, lng_ref[...], lnb_ref[...])


def _conv_layer_kernel(*refs, has_state, width, col_w, alpha):
    if has_state:
        (x_ref, mod_ref, st_ref, win_ref, wconv_ref, wout_ref, lng_ref, lnb_ref,
         o_ref, nst_ref, conv_scr, y_scr) = refs
    else:
        (x_ref, mod_ref, win_ref, wconv_ref, wout_ref, lng_ref, lnb_ref,
         o_ref, nst_ref, conv_scr, y_scr) = refs
        st_ref = None
    bb, tt, d = x_ref.shape
    rows = bb * tt
    dc = wconv_ref.shape[1]

    @pl.when(pl.program_id(1) == 0)
    def _():
        _init_conv_halo(conv_scr, st_ref, width)

    x, u = _modulate(x_ref, mod_ref)
    u = u.reshape(rows, d).astype(BF16)

    for j in range(dc // col_w):
        sl = slice(j * col_w, (j + 1) * col_w)
        pc = _dot(u, win_ref[:, dc + j * col_w:dc + (j + 1) * col_w])
        ph = _dot(u, win_ref[:, 2 * dc + j * col_w:2 * dc + (j + 1) * col_w])
        conv_scr[:, HALO:HALO + tt, sl] = (pc * ph).reshape(bb, tt, col_w)
        conv = _causal_conv(conv_scr, wconv_ref, width, tt, sl).reshape(rows, col_w)
        pb = _dot(u, win_ref[:, sl])
        pz = _dot(u, win_ref[:, 3 * dc + j * col_w:3 * dc + (j + 1) * col_w])
        y_scr[:, sl] = (pb * conv * _silu(pz)).astype(BF16)

    out = _dot(y_scr[...], wout_ref[...])
    _finish(x, out, mod_ref, lng_ref, lnb_ref, o_ref, alpha)

    nst_ref[...] = conv_scr[:, HALO + tt - (width - 1):HALO + tt, :]
    conv_scr[:, 0:HALO, :] = conv_scr[:, tt:tt + HALO, :]


def _const_spec(shape):
    zeros = (0,) * len(shape)
    return pl.BlockSpec(shape, lambda b, t: zeros)


def _conv_layer(x, mod, state, w_in, w_conv, w_out, ln_g, ln_b, *, bb, tt, alpha):
    bsz, seq, d = x.shape
    width, dc = w_conv.shape
    has_state = state is not None
    rows = bb * tt
    seq_spec = lambda n: pl.BlockSpec((bb, n, d if n == 3 else dc), lambda b, t: (b, 0, 0))
    in_specs = [pl.BlockSpec((bb, tt, d), lambda b, t: (b, t, 0)), seq_spec(3)]
    args = [x, mod]
    if has_state:
        in_specs.append(seq_spec(width - 1))
        args.append(state)
    in_specs += [_const_spec(w_in.shape), _const_spec(w_conv.shape), _const_spec(w_out.shape),
                 _const_spec((1, d)), _const_spec((1, d))]
    args += [w_in, w_conv, w_out, ln_g.reshape(1, d), ln_b.reshape(1, d)]
    kern = functools.partial(_conv_layer_kernel, has_state=has_state, width=width,
                             col_w=min(dc, 512), alpha=alpha)
    return pl.pallas_call(
        kern,
        grid=(bsz // bb, seq // tt),
        in_specs=in_specs,
        out_specs=[pl.BlockSpec((bb, tt, d), lambda b, t: (b, t, 0)), seq_spec(width - 1)],
        out_shape=[jax.ShapeDtypeStruct((bsz, seq, d), F32),
                   jax.ShapeDtypeStruct((bsz, width - 1, dc), F32)],
        scratch_shapes=[pltpu.VMEM((bb, HALO + tt, dc), F32), pltpu.VMEM((rows, dc), BF16)],
        compiler_params=pltpu.CompilerParams(
            dimension_semantics=("arbitrary", "arbitrary"), vmem_limit_bytes=VMEM_LIMIT),
        name="conv_layer",
    )(*args)


def _tri_masks(n, seq_len):
    i = lax.broadcasted_iota(jnp.int32, (n, n), 0)
    j = lax.broadcasted_iota(jnp.int32, (n, n), 1)
    same = (i // seq_len) == (j // seq_len)
    causal = same & (i >= j)
    strict = same & (i > j)
    eye = (i == j).astype(F32)
    levels = []
    s = 1
    while s < seq_len:
        levels.append(((i // (2 * s)) == (j // (2 * s))) & ((i % (2 * s)) >= s) & ((j % (2 * s)) < s))
        s *= 2
    return causal, strict, eye, levels


def _gdn_layer_kernel(*refs, has_state, width, n_heads, dk, dv, seq_len, col_w, group, alpha):
    if has_state:
        (x_ref, mod_ref, cst_ref, s0_ref, win_ref, wab_ref, wconv_ref, alog_ref, dtb_ref, nw_ref,
         wout_ref, lng_ref, lnb_ref, o_ref, ncst_ref, ns_ref,
         conv_scr, qkv_scr, z_scr, gt_scr, gx_scr, kb_scr, u_scr, wq_scr, attn_scr, oh_scr) = refs
        s_scr = None
    else:
        (x_ref, mod_ref, win_ref, wab_ref, wconv_ref, alog_ref, dtb_ref, nw_ref,
         wout_ref, lng_ref, lnb_ref, o_ref, ncst_ref, ns_ref,
         conv_scr, qkv_scr, z_scr, gt_scr, gx_scr, kb_scr, u_scr, wq_scr, attn_scr, oh_scr,
         s_scr) = refs
        cst_ref = s0_ref = None
    bb, tt, d = x_ref.shape
    rows = bb * tt
    ch = wconv_ref.shape[1]
    qk = n_heads * dk
    vw = n_heads * dv
    n = min(GDN_ROWS, rows)
    span = min(seq_len, n)
    n_seq = n // span
    n_chunks = rows // n
    heads = range(n_heads)
    q_sl = lambda h: slice(h * dk, (h + 1) * dk)
    k_sl = lambda h: slice(qk + h * dk, qk + (h + 1) * dk)
    v_sl = lambda h: slice(2 * qk + h * dv, 2 * qk + (h + 1) * dv)
    o_sl = lambda h: slice(h * dv, (h + 1) * dv)

    @pl.when(pl.program_id(1) == 0)
    def _():
        _init_conv_halo(conv_scr, cst_ref, width)
        if s_scr is not None:
            s_scr[...] = jnp.zeros(s_scr.shape, F32)

    x, u = _modulate(x_ref, mod_ref)
    u = u.reshape(rows, d).astype(BF16)

    for j in range(ch // col_w):
        sl = slice(j * col_w, (j + 1) * col_w)
        conv_scr[:, HALO:HALO + tt, sl] = _dot(u, win_ref[:, sl]).reshape(bb, tt, col_w)
        conv = _causal_conv(conv_scr, wconv_ref, width, tt, sl).reshape(rows, col_w)
        qkv_scr[:, sl] = _silu(conv)
    ncst_ref[...] = conv_scr[:, HALO + tt - (width - 1):HALO + tt, :]
    conv_scr[:, 0:HALO, :] = conv_scr[:, tt:tt + HALO, :]

    z_scr[...] = _silu(_dot(u, win_ref[:, ch:ch + vw]))

    ab = _dot(u, wab_ref[...])
    lane = lax.broadcasted_iota(jnp.int32, (rows, LANES), 1)
    a = ab[:, :LANES] + dtb_ref[...]
    softplus = jnp.maximum(a, 0.0) + jnp.log(1.0 + jnp.exp(-jnp.abs(a)))
    g = jnp.where(lane < n_heads, -jnp.exp(alog_ref[...]) * softplus, 0.0)
    beta = jax.nn.sigmoid(ab[:, LANES:])
    ri = lax.broadcasted_iota(jnp.int32, (rows, rows), 0)
    rj = lax.broadcasted_iota(jnp.int32, (rows, rows), 1)
    tri = (((ri // span) == (rj // span)) & (ri >= rj)).astype(BF16)
    g_hi, g_mid, g_lo = _split3(g)
    g_cum = (jnp.dot(tri, g_hi, preferred_element_type=F32)
             + jnp.dot(tri, g_mid, preferred_element_type=F32)
             + jnp.dot(tri, g_lo, preferred_element_type=F32))
    for c in range(n_chunks):
        gt_scr[c] = g_cum[c * n:(c + 1) * n, :].T

    for h in heads:
        t = qkv_scr[:, q_sl(h)]
        qkv_scr[:, q_sl(h)] = t * (lax.rsqrt(jnp.sum(t * t, axis=-1, keepdims=True) + NORM_EPS)
                                   * dk ** -0.5)
        t = qkv_scr[:, k_sl(h)]
        t = t * lax.rsqrt(jnp.sum(t * t, axis=-1, keepdims=True) + NORM_EPS)
        qkv_scr[:, k_sl(h)] = t
        beta_h = beta[:, h:h + 1]
        kb_scr[:, o_sl(h)] = t * beta_h
        qkv_scr[:, v_sl(h)] = qkv_scr[:, v_sl(h)] * beta_h
        gx_scr[:, o_sl(h)] = jnp.broadcast_to(g_cum[:, h:h + 1], (rows, dk))

    causal, strict, eye, levels = _tri_masks(n, span)
    row_seq = lax.broadcasted_iota(jnp.int32, (n, 1), 0) // span
    row_seq2 = (lax.broadcasted_iota(jnp.int32, (2 * n, 1), 0) % n) // span

    def local_body(gi, carry):
        items = []
        for ci in range(group):
            c = gi * group + ci
            rs = pl.ds(pl.multiple_of(c * n, n), n)
            items += [(c, rs, h) for h in heads]
        decay, a_mat, t_inv = [], [], []
        for c, rs, h in items:
            diff = gx_scr[rs, h * dv:h * dv + n] - gt_scr[c, h:h + 1, :]
            decay.append(jnp.where(causal, jnp.exp(jnp.where(causal, diff, 0.0)), 0.0))
        for (c, rs, h), dec in zip(items, decay):
            k = qkv_scr[rs, k_sl(h)]
            a_mat.append(jnp.where(strict, _dot_nt(kb_scr[rs, o_sl(h)], k) * dec, 0.0))
        for (c, rs, h), dec in zip(items, decay):
            attn_scr[rs, h * dv:h * dv + n] = (
                _dot_nt(qkv_scr[rs, q_sl(h)], qkv_scr[rs, k_sl(h)]) * dec)
        t_inv = [eye - jnp.where(levels[0], a, 0.0) for a in a_mat]
        for m in levels[1:]:
            y = [_dot(jnp.where(m, a, 0.0), t) for a, t in zip(a_mat, t_inv)]
            xs = [_dot(t, yy) for t, yy in zip(t_inv, y)]
            t_inv = [t - xx for t, xx in zip(t_inv, xs)]
        for (c, rs, h), t in zip(items, t_inv):
            e_x = jnp.exp(gx_scr[rs, o_sl(h)])
            rhs = jnp.concatenate([qkv_scr[rs, v_sl(h)], kb_scr[rs, o_sl(h)] * e_x], axis=1)
            sol = _dot(t, rhs)
            u_scr[rs, o_sl(h)] = sol[:, :dv]
            wq_scr[c, 0:n, o_sl(h)] = sol[:, dv:]
            wq_scr[c, n:2 * n, o_sl(h)] = qkv_scr[rs, q_sl(h)] * e_x
        return carry

    lax.fori_loop(0, n_chunks // group, local_body, 0)

    def state_body(c, carry):
        rs = pl.ds(pl.multiple_of(c * n, n), n)
        pairs = [(h, s) for h in heads for s in range(n_seq)]
        masks = [None if n_seq == 1 else (row_seq == s) for s in range(n_seq)]
        masks2 = [None if n_seq == 1 else (row_seq2 == s) for s in range(n_seq)]
        if has_state:
            s_prev = {(h, s): s0_ref[c * n_seq + s, h] for h, s in pairs}
        else:
            s_prev = {(h, 0): s_scr[h] for h in heads}
        ws = {(h, s): _dot(wq_scr[c, :, o_sl(h)], s_prev[(h, s)]) for h, s in pairs}
        v_new, v_dec, o_part, g_last = {}, {}, {}, {}
        for h in heads:
            acc = ws[(h, 0)] if n_seq == 1 else sum(
                jnp.where(masks2[s], ws[(h, s)], 0.0) for s in range(n_seq))
            v_new[h] = u_scr[rs, o_sl(h)] - acc[:n]
            o_part[h] = acc[n:]
            gx = gx_scr[rs, o_sl(h)]
            for s in range(n_seq):
                g_last[(h, s)] = gx[(s + 1) * span - 1:(s + 1) * span, :]
            g_end = jnp.concatenate(
                [jnp.broadcast_to(g_last[(h, s)], (span, dv)) for s in range(n_seq)], axis=0)
            v_dec[h] = v_new[h] * jnp.exp(g_end - gx)
        for h in heads:
            oh_scr[rs, o_sl(h)] = o_part[h] + _dot(attn_scr[rs, h * dv:h * dv + n], v_new[h])
        for h, s in pairs:
            vs = v_dec[h] if n_seq == 1 else jnp.where(masks[s], v_dec[h], 0.0)
            s_new = (s_prev[(h, s)] * jnp.exp(g_last[(h, s)])
                     + _dot_tn(qkv_scr[rs, k_sl(h)], vs))
            if has_state:
                ns_ref[c * n_seq + s, h] = s_new
            else:
                s_scr[h] = s_new
        return carry

    lax.fori_loop(0, n_chunks, state_body, 0)
    if not has_state:
        ns_ref[0] = s_scr[...]

    for h in heads:
        o = oh_scr[:, o_sl(h)]
        o = o * lax.rsqrt(jnp.mean(o * o, axis=-1, keepdims=True) + NORM_EPS) * nw_ref[...]
        oh_scr[:, o_sl(h)] = o * z_scr[:, o_sl(h)]
    out = _dot(oh_scr[...], wout_ref[...])
    _finish(x, out, mod_ref, lng_ref, lnb_ref, o_ref, alpha)


def _gdn_layer(x, mod, conv_state, ssm_state, w_in, w_ab, w_conv, a_log, dt_bias, norm_w, w_out,
               ln_g, ln_b, *, bb, tt, alpha):
    bsz, seq, d = x.shape
    width, ch = w_conv.shape
    n_heads = a_log.shape[0]
    vw = w_out.shape[0]
    dv = vw // n_heads
    dk = (ch - vw) // 2 // n_heads
    qk = n_heads * dk
    has_state = conv_state is not None
    rows = bb * tt
    seq_len = min(seq, GDN_ROWS)
    n = min(GDN_ROWS, rows)
    n_chunks = rows // n
    assert dk == dv == LANES, "one head per lane tile"
    assert rows % n == 0 and n % seq_len == 0
    assert has_state == (tt == seq), "a carried state needs whole sequences per block"
    group = 4 if n_chunks % 4 == 0 else 1

    pad = lambda vec: jnp.zeros((1, LANES), F32).at[0, :n_heads].set(vec)
    in_specs = [pl.BlockSpec((bb, tt, d), lambda b, t: (b, t, 0)),
                pl.BlockSpec((bb, 3, d), lambda b, t: (b, 0, 0))]
    args = [x, mod]
    if has_state:
        in_specs += [pl.BlockSpec((bb, width - 1, ch), lambda b, t: (b, 0, 0)),
                     pl.BlockSpec((bb, n_heads, dk, dv), lambda b, t: (b, 0, 0, 0))]
        args += [conv_state, ssm_state]
    in_specs += [_const_spec(w_in.shape), _const_spec(w_ab.shape), _const_spec(w_conv.shape),
                 _const_spec((1, LANES)), _const_spec((1, LANES)), _const_spec((1, dv)),
                 _const_spec(w_out.shape), _const_spec((1, d)), _const_spec((1, d))]
    args += [w_in, w_ab, w_conv, pad(a_log), pad(dt_bias), norm_w.reshape(1, dv), w_out,
             ln_g.reshape(1, d), ln_b.reshape(1, d)]
    scratch = [pltpu.VMEM((bb, HALO + tt, ch), F32),
               pltpu.VMEM((rows, ch), F32),
               pltpu.VMEM((rows, vw), F32),
               pltpu.VMEM((n_chunks, LANES, n), F32),
               pltpu.VMEM((rows, vw), F32),
               pltpu.VMEM((rows, qk), F32),
               pltpu.VMEM((rows, vw), F32),
               pltpu.VMEM((n_chunks, 2 * n, qk), F32),
               pltpu.VMEM((rows, vw), F32),
               pltpu.VMEM((rows, vw), F32)]
    if not has_state:
        scratch.append(pltpu.VMEM((n_heads, dk, dv), F32))
    kern = functools.partial(_gdn_layer_kernel, has_state=has_state, width=width, n_heads=n_heads,
                             dk=dk, dv=dv, seq_len=seq_len, col_w=min(ch, 512), group=group,
                             alpha=alpha)
    return pl.pallas_call(
        kern,
        grid=(bsz // bb, seq // tt),
        in_specs=in_specs,
        out_specs=[pl.BlockSpec((bb, tt, d), lambda b, t: (b, t, 0)),
                   pl.BlockSpec((bb, width - 1, ch), lambda b, t: (b, 0, 0)),
                   pl.BlockSpec((bb, n_heads, dk, dv), lambda b, t: (b, 0, 0, 0))],
        out_shape=[jax.ShapeDtypeStruct((bsz, seq, d), F32),
                   jax.ShapeDtypeStruct((bsz, width - 1, ch), F32),
                   jax.ShapeDtypeStruct((bsz, n_heads, dk, dv), F32)],
        scratch_shapes=scratch,
        compiler_params=pltpu.CompilerParams(
            dimension_semantics=("arbitrary", "arbitrary"), vmem_limit_bytes=VMEM_LIMIT),
        name="gdn_layer",
    )(*args)


def _block_rows(bsz, seq, target):
    if seq >= target:
        return 1, target
    return max(1, min(bsz, target // seq)), seq


def _trunk(x, mod, conv_a, conv_b, ssm_b, ln_g, ln_b, wa, wb, alpha):
    bsz, seq, _ = x.shape
    new_a, new_cb, new_s = [], [], []
    depth = mod.shape[0]
    for l in range(depth):
        i = l // 2
        if l % 2 == 0:
            bb, tt = _block_rows(bsz, seq, 512)
            x, nb = _conv_layer(x, mod[l], None if conv_a is None else conv_a[i],
                                wa["in"][i], wa["conv"][i], wa["out"][i], ln_g[l], ln_b[l],
                                bb=bb, tt=tt, alpha=alpha)
            new_a.append(nb)
        else:
            bb, tt = _block_rows(bsz, seq, 256 if conv_b is None else GDN_ROWS)
            x, nb, s = _gdn_layer(x, mod[l], None if conv_b is None else conv_b[i],
                                  None if ssm_b is None else ssm_b[i],
                                  wb["in"][i], wb["ab"][i], wb["conv"][i], wb["a_log"][i],
                                  wb["dt_bias"][i], wb["norm"][i], wb["out"][i], ln_g[l], ln_b[l],
                                  bb=bb, tt=tt, alpha=alpha)
            new_cb.append(nb)
            new_s.append(s)
    return x, jnp.stack(new_a), jnp.stack(new_cb), jnp.stack(new_s)


def kernel(x_prompt, x_sample, state_conv_a, state_conv_b, state_ssm_b, c_prompt, c_sample, w_mod, b_mod, ln_g, ln_b, wa_in, wa_conv, wa_out, wb_in, wb_conv, wb_a_log, wb_dt_bias, wb_norm, wb_out):
    depth, d, _ = w_mod.shape
    alpha = (2 * depth) ** 0.25
    bp = x_prompt.shape[0]
    n_heads = wb_a_log.shape[1]
    ch = wb_conv.shape[2]
    vw = wb_out.shape[1]

    mod = _modulation(jnp.concatenate([c_prompt, c_sample], axis=0), w_mod, b_mod)
    mod = mod.reshape(depth, -1, 3, d)
    mod_p, mod_s = mod[:, :bp], mod[:, bp:]

    n_gdn = wb_in.shape[0]
    w_ab = jnp.zeros((n_gdn, d, 2 * LANES), F32)
    w_ab = w_ab.at[:, :, :n_heads].set(wb_in[:, :, ch + vw + n_heads:])
    w_ab = w_ab.at[:, :, LANES:LANES + n_heads].set(wb_in[:, :, ch + vw:ch + vw + n_heads])
    wa = {"in": wa_in.astype(BF16), "conv": wa_conv, "out": wa_out.astype(BF16)}
    wb = {"in": wb_in[:, :, :ch + vw].astype(BF16), "ab": w_ab.astype(BF16), "conv": wb_conv,
          "a_log": wb_a_log, "dt_bias": wb_dt_bias, "norm": wb_norm, "out": wb_out.astype(BF16)}

    y_p, ca_p, cb_p, s_p = _trunk(x_prompt, mod_p, None, None, None, ln_g, ln_b, wa, wb, alpha)
    y_s, ca_s, cb_s, s_s = _trunk(x_sample, mod_s, state_conv_a, state_conv_b, state_ssm_b,
                                  ln_g, ln_b, wa, wb, alpha)
    return (y_p, y_s, ca_p, cb_p, s_p, ca_s, cb_s, s_s)
```

```python
import functools

import jax
import jax.numpy as jnp
from jax import lax
from jax.experimental import pallas as pl
from jax.experimental.pallas import tpu as pltpu

F32 = jnp.float32
BF16 = jnp.bfloat16

LN_EPS = 1e-5
NORM_EPS = 1e-6
SUBLANES = 8
LANES = 128
HALO = SUBLANES
GDN_ROWS = 64
VMEM_LIMIT = 56 * 1024 * 1024


def _dot(a, b):
    return jnp.dot(a.astype(BF16), b.astype(BF16), preferred_element_type=F32)


def _dot_nt(a, b):
    return lax.dot_general(a.astype(BF16), b.astype(BF16), (((1,), (1,)), ((), ())),
                           preferred_element_type=F32)


def _dot_tn(a, b):
    return lax.dot_general(a.astype(BF16), b.astype(BF16), (((0,), (0,)), ((), ())),
                           preferred_element_type=F32)


def _silu(x):
    return x * jax.nn.sigmoid(x)


def _split3(x):
    hi = x.astype(BF16)
    r = x - hi.astype(F32)
    mid = r.astype(BF16)
    lo = (r - mid.astype(F32)).astype(BF16)
    return hi, mid, lo


def _layer_norm(r, g, b):
    mu = jnp.mean(r, axis=-1, keepdims=True)
    d = r - mu
    var = jnp.mean(d * d, axis=-1, keepdims=True)
    return d * lax.rsqrt(var + LN_EPS) * g + b


def _mod_kernel(c_ref, w_ref, b_ref, o_ref):
    cs = _silu(c_ref[...])
    o_ref[0] = _dot(cs, w_ref[0]) + b_ref[0]


def _modulation(c_all, w_mod, b_mod):
    depth, d, n3 = w_mod.shape
    rows = c_all.shape[0]
    nb = d
    return pl.pallas_call(
        _mod_kernel,
        grid=(depth, n3 // nb),
        in_specs=[
            pl.BlockSpec((rows, d), lambda l, j: (0, 0)),
            pl.BlockSpec((1, d, nb), lambda l, j: (l, 0, j)),
            pl.BlockSpec((1, 1, nb), lambda l, j: (l, 0, j)),
        ],
        out_specs=pl.BlockSpec((1, rows, nb), lambda l, j: (l, 0, j)),
        out_shape=jax.ShapeDtypeStruct((depth, rows, n3), F32),
        compiler_params=pltpu.CompilerParams(
            dimension_semantics=("arbitrary", "arbitrary"), vmem_limit_bytes=VMEM_LIMIT),
        name="modulation",
    )(c_all, w_mod, b_mod.reshape(depth, 1, n3))


def _modulate(x_ref, mod_ref):
    x = x_ref[...]
    shift = mod_ref[:, 0:1, :]
    scale = mod_ref[:, 1:2, :]
    return x, x * (1.0 + scale) + shift


def _init_conv_halo(halo_scr, st_ref, width):
    halo_scr[...] = jnp.zeros(halo_scr.shape, F32)
    if st_ref is not None:
        halo_scr[:, HALO - (width - 1):HALO, :] = st_ref[...]


def _causal_conv(halo_scr, nst_ref, p, wconv_ref, width, sl):
    bb, tt, cw = p.shape
    ext = jnp.concatenate([halo_scr[:, :, sl], p], axis=1)
    halo_scr[:, :, sl] = ext[:, tt:, :]
    nst_ref[:, :, sl] = ext[:, HALO + tt - (width - 1):, :]
    e = ext.reshape(bb * (HALO + tt), cw)
    e1 = pltpu.roll(e, 1, axis=0)
    tap = lambda k: wconv_ref[width - 1 - k:width - k, sl]
    y = None
    for m in range((width + 1) // 2):
        pair = e * tap(2 * m)
        if 2 * m + 1 < width:
            pair = pair + e1 * tap(2 * m + 1)
        if m:
            pair = pltpu.roll(pair, 2 * m, axis=0)
        y = pair if y is None else y + pair
    return y.reshape(bb, HALO + tt, cw)[:, HALO:, :].reshape(bb * tt, cw)


def _residual_norm(x, out, mod_ref, lng_ref, lnb_ref, o_ref, alpha):
    bb, tt, d = o_ref.shape
    gate = mod_ref[:, 2:3, :]
    r = alpha * x.reshape(bb, tt, d) + gate * out.reshape(bb, tt, d)
    o_ref[...] = _layer_norm(r, lng_ref[...], lnb_ref[...])


def _const_spec(shape):
    zeros = (0,) * len(shape)
    return pl.BlockSpec(shape, lambda *_: zeros)


def _conv_layer_kernel(*refs, has_state, width, col_w, alpha):
    if has_state:
        (x_ref, mod_ref, st_ref, win_ref, wconv_ref, wout_ref, lng_ref, lnb_ref,
         o_ref, nst_ref, halo_scr, y_scr) = refs
    else:
        (x_ref, mod_ref, win_ref, wconv_ref, wout_ref, lng_ref, lnb_ref,
         o_ref, nst_ref, halo_scr, y_scr) = refs
        st_ref = None
    bb, tt, d = x_ref.shape
    rows = bb * tt
    dc = wconv_ref.shape[1]

    @pl.when(pl.program_id(1) == 0)
    def _():
        _init_conv_halo(halo_scr, st_ref, width)

    x, u = _modulate(x_ref, mod_ref)
    u = u.reshape(rows, d).astype(BF16)

    for j in range(dc // col_w):
        sl = slice(j * col_w, (j + 1) * col_w)
        pc = _dot(u, win_ref[:, dc + j * col_w:dc + (j + 1) * col_w])
        ph = _dot(u, win_ref[:, 2 * dc + j * col_w:2 * dc + (j + 1) * col_w])
        conv = _causal_conv(halo_scr, nst_ref, (pc * ph).reshape(bb, tt, col_w), wconv_ref, width, sl)
        pb = _dot(u, win_ref[:, sl])
        pz = _dot(u, win_ref[:, 3 * dc + j * col_w:3 * dc + (j + 1) * col_w])
        y_scr[:, sl] = (pb * conv * _silu(pz)).astype(BF16)

    out = _dot(y_scr[...], wout_ref[...])
    _residual_norm(x, out, mod_ref, lng_ref, lnb_ref, o_ref, alpha)


def _conv_layer(x, mod, state, w_in, w_conv, w_out, ln_g, ln_b, *, bb, tt, alpha):
    bsz, seq, d = x.shape
    width, dc = w_conv.shape
    has_state = state is not None
    rows = bb * tt
    seq_spec = lambda n: pl.BlockSpec((bb, n, d if n == 3 else dc), lambda b, t: (b, 0, 0))
    in_specs = [pl.BlockSpec((bb, tt, d), lambda b, t: (b, t, 0)), seq_spec(3)]
    args = [x, mod]
    if has_state:
        in_specs.append(seq_spec(width - 1))
        args.append(state)
    in_specs += [_const_spec(w_in.shape), _const_spec(w_conv.shape), _const_spec(w_out.shape),
                 _const_spec((1, d)), _const_spec((1, d))]
    args += [w_in, w_conv, w_out, ln_g.reshape(1, d), ln_b.reshape(1, d)]
    kern = functools.partial(_conv_layer_kernel, has_state=has_state, width=width,
                             col_w=min(dc, 256), alpha=alpha)
    return pl.pallas_call(
        kern,
        grid=(bsz // bb, seq // tt),
        in_specs=in_specs,
        out_specs=[pl.BlockSpec((bb, tt, d), lambda b, t: (b, t, 0)), seq_spec(width - 1)],
        out_shape=[jax.ShapeDtypeStruct((bsz, seq, d), F32),
                   jax.ShapeDtypeStruct((bsz, width - 1, dc), F32)],
        scratch_shapes=[pltpu.VMEM((bb, HALO, dc), F32), pltpu.VMEM((rows, dc), BF16)],
        compiler_params=pltpu.CompilerParams(
            dimension_semantics=("arbitrary", "arbitrary"), vmem_limit_bytes=VMEM_LIMIT),
        name="conv_layer",
    )(*args)


def _tri_masks(n, seq_len):
    i = lax.broadcasted_iota(jnp.int32, (n, n), 0)
    j = lax.broadcasted_iota(jnp.int32, (n, n), 1)
    same = (i // seq_len) == (j // seq_len)
    causal = same & (i >= j)
    strict = same & (i > j)
    eye = (i == j).astype(F32)
    levels = []
    s = 1
    while s < seq_len:
        levels.append(((i // (2 * s)) == (j // (2 * s))) & ((i % (2 * s)) >= s) & ((j % (2 * s)) < s))
        s *= 2
    return causal, strict, eye, levels


def _gdn_layer_kernel(*refs, has_state, width, n_heads, dk, dv, seq_len, col_w, group, alpha):
    if has_state:
        (x_ref, mod_ref, cst_ref, s0_ref, win_ref, wab_ref, wconv_ref, alog_ref, dtb_ref, nw_ref,
         wout_ref, lng_ref, lnb_ref, o_ref, ncst_ref, ns_ref,
         halo_scr, qkv_scr, z_scr, gt_scr, gx_scr, kb_scr, u_scr, wq_scr, attn_scr, oh_scr) = refs
        s_scr = None
    else:
        (x_ref, mod_ref, win_ref, wab_ref, wconv_ref, alog_ref, dtb_ref, nw_ref,
         wout_ref, lng_ref, lnb_ref, o_ref, ncst_ref, ns_ref,
         halo_scr, qkv_scr, z_scr, gt_scr, gx_scr, kb_scr, u_scr, wq_scr, attn_scr, oh_scr,
         s_scr) = refs
        cst_ref = s0_ref = None
    bb, tt, d = x_ref.shape
    rows = bb * tt
    ch = wconv_ref.shape[1]
    qk = n_heads * dk
    vw = n_heads * dv
    n = min(GDN_ROWS, rows)
    span = min(seq_len, n)
    n_seq = n // span
    n_chunks = rows // n
    heads = range(n_heads)
    q_sl = lambda h: slice(h * dk, (h + 1) * dk)
    k_sl = lambda h: slice(qk + h * dk, qk + (h + 1) * dk)
    v_sl = lambda h: slice(2 * qk + h * dv, 2 * qk + (h + 1) * dv)
    o_sl = lambda h: slice(h * dv, (h + 1) * dv)

    @pl.when(pl.program_id(1) == 0)
    def _():
        _init_conv_halo(halo_scr, cst_ref, width)
        if s_scr is not None:
            s_scr[...] = jnp.zeros(s_scr.shape, F32)

    x, u = _modulate(x_ref, mod_ref)
    u = u.reshape(rows, d).astype(BF16)

    for j in range(ch // col_w):
        sl = slice(j * col_w, (j + 1) * col_w)
        p = _dot(u, win_ref[:, sl]).reshape(bb, tt, col_w)
        qkv_scr[:, sl] = _silu(_causal_conv(halo_scr, ncst_ref, p, wconv_ref, width, sl))

    z_scr[...] = _silu(_dot(u, win_ref[:, ch:ch + vw]))

    ab = _dot(u, wab_ref[...])
    lane = lax.broadcasted_iota(jnp.int32, (rows, LANES), 1)
    a = ab[:, :LANES] + dtb_ref[...]
    softplus = jnp.maximum(a, 0.0) + jnp.log(1.0 + jnp.exp(-jnp.abs(a)))
    g = jnp.where(lane < n_heads, -jnp.exp(alog_ref[...]) * softplus, 0.0)
    beta = jax.nn.sigmoid(ab[:, LANES:])
    ri = lax.broadcasted_iota(jnp.int32, (rows, rows), 0)
    rj = lax.broadcasted_iota(jnp.int32, (rows, rows), 1)
    tri = (((ri // span) == (rj // span)) & (ri >= rj)).astype(BF16)
    g_hi, g_mid, g_lo = _split3(g)
    g_cum = (jnp.dot(tri, g_hi, preferred_element_type=F32)
             + jnp.dot(tri, g_mid, preferred_element_type=F32)
             + jnp.dot(tri, g_lo, preferred_element_type=F32))
    for c in range(n_chunks):
        gt_scr[c] = g_cum[c * n:(c + 1) * n, :].T

    for h in heads:
        t = qkv_scr[:, q_sl(h)]
        qkv_scr[:, q_sl(h)] = t * (lax.rsqrt(jnp.sum(t * t, axis=-1, keepdims=True) + NORM_EPS)
                                   * dk ** -0.5)
        t = qkv_scr[:, k_sl(h)]
        t = t * lax.rsqrt(jnp.sum(t * t, axis=-1, keepdims=True) + NORM_EPS)
        qkv_scr[:, k_sl(h)] = t
        beta_h = beta[:, h:h + 1]
        kb_scr[:, o_sl(h)] = t * beta_h
        qkv_scr[:, v_sl(h)] = qkv_scr[:, v_sl(h)] * beta_h
        gx_scr[:, o_sl(h)] = jnp.broadcast_to(g_cum[:, h:h + 1], (rows, dk))

    causal, strict, eye, levels = _tri_masks(n, span)
    row_seq = lax.broadcasted_iota(jnp.int32, (n, 1), 0) // span
    row_seq2 = (lax.broadcasted_iota(jnp.int32, (2 * n, 1), 0) % n) // span

    def local_body(gi, carry):
        items = []
        for ci in range(group):
            c = gi * group + ci
            rs = pl.ds(pl.multiple_of(c * n, n), n)
            items += [(c, rs, h) for h in heads]
        decay, a_mat, t_inv = [], [], []
        for c, rs, h in items:
            diff = gx_scr[rs, h * dv:h * dv + n] - gt_scr[c, h:h + 1, :]
            decay.append(jnp.where(causal, jnp.exp(jnp.where(causal, diff, 0.0)), 0.0))
        for (c, rs, h), dec in zip(items, decay):
            k = qkv_scr[rs, k_sl(h)]
            a_mat.append(jnp.where(strict, _dot_nt(kb_scr[rs, o_sl(h)], k) * dec, 0.0))
        for (c, rs, h), dec in zip(items, decay):
            attn_scr[rs, h * dv:h * dv + n] = (
                _dot_nt(qkv_scr[rs, q_sl(h)], qkv_scr[rs, k_sl(h)]) * dec)
        t_inv = [eye - jnp.where(levels[0], a, 0.0) for a in a_mat]
        for m in levels[1:]:
            y = [_dot(jnp.where(m, a, 0.0), t) for a, t in zip(a_mat, t_inv)]
            xs = [_dot(t, yy) for t, yy in zip(t_inv, y)]
            t_inv = [t - xx for t, xx in zip(t_inv, xs)]
        for (c, rs, h), t in zip(items, t_inv):
            e_x = jnp.exp(gx_scr[rs, o_sl(h)])
            rhs = jnp.concatenate([qkv_scr[rs, v_sl(h)], kb_scr[rs, o_sl(h)] * e_x], axis=1)
            sol = _dot(t, rhs)
            u_scr[rs, o_sl(h)] = sol[:, :dv]
            wq_scr[c, 0:n, o_sl(h)] = sol[:, dv:]
            wq_scr[c, n:2 * n, o_sl(h)] = qkv_scr[rs, q_sl(h)] * e_x
        return carry

    lax.fori_loop(0, n_chunks // group, local_body, 0)

    def state_body(c, carry):
        rs = pl.ds(pl.multiple_of(c * n, n), n)
        pairs = [(h, s) for h in heads for s in range(n_seq)]
        masks = [None if n_seq == 1 else (row_seq == s) for s in range(n_seq)]
        masks2 = [None if n_seq == 1 else (row_seq2 == s) for s in range(n_seq)]
        if has_state:
            s_prev = {(h, s): s0_ref[c * n_seq + s, h] for h, s in pairs}
        else:
            s_prev = {(h, 0): s_scr[h] for h in heads}
        ws = {(h, s): _dot(wq_scr[c, :, o_sl(h)], s_prev[(h, s)]) for h, s in pairs}
        v_new, v_dec, o_part, g_last = {}, {}, {}, {}
        for h in heads:
            acc = ws[(h, 0)] if n_seq == 1 else sum(
                jnp.where(masks2[s], ws[(h, s)], 0.0) for s in range(n_seq))
            v_new[h] = u_scr[rs, o_sl(h)] - acc[:n]
            o_part[h] = acc[n:]
            gx = gx_scr[rs, o_sl(h)]
            for s in range(n_seq):
                g_last[(h, s)] = gx[(s + 1) * span - 1:(s + 1) * span, :]
            g_end = jnp.concatenate(
                [jnp.broadcast_to(g_last[(h, s)], (span, dv)) for s in range(n_seq)], axis=0)
            v_dec[h] = v_new[h] * jnp.exp(g_end - gx)
        for h in heads:
            oh_scr[rs, o_sl(h)] = o_part[h] + _dot(attn_scr[rs, h * dv:h * dv + n], v_new[h])
        for h, s in pairs:
            vs = v_dec[h] if n_seq == 1 else jnp.where(masks[s], v_dec[h], 0.0)
            s_new = (s_prev[(h, s)] * jnp.exp(g_last[(h, s)])
                     + _dot_tn(qkv_scr[rs, k_sl(h)], vs))
            if has_state:
                ns_ref[c * n_seq + s, h] = s_new
            else:
                s_scr[h] = s_new
        return carry

    lax.fori_loop(0, n_chunks, state_body, 0)
    if not has_state:
        ns_ref[0] = s_scr[...]

    for h in heads:
        o = oh_scr[:, o_sl(h)]
        o = o * lax.rsqrt(jnp.mean(o * o, axis=-1, keepdims=True) + NORM_EPS) * nw_ref[...]
        oh_scr[:, o_sl(h)] = o * z_scr[:, o_sl(h)]
    out = _dot(oh_scr[...], wout_ref[...])
    _residual_norm(x, out, mod_ref, lng_ref, lnb_ref, o_ref, alpha)


def _gdn_layer(x, mod, conv_state, ssm_state, w_in, w_ab, w_conv, a_log, dt_bias, norm_w, w_out,
               ln_g, ln_b, *, bb, tt, alpha):
    bsz, seq, d = x.shape
    width, ch = w_conv.shape
    n_heads = a_log.shape[0]
    vw = w_out.shape[0]
    dv = vw // n_heads
    dk = (ch - vw) // 2 // n_heads
    qk = n_heads * dk
    has_state = conv_state is not None
    rows = bb * tt
    seq_len = min(seq, GDN_ROWS)
    n = min(GDN_ROWS, rows)
    n_chunks = rows // n
    assert dk == dv == LANES, "one head per lane tile"
    assert rows % n == 0 and n % seq_len == 0
    assert has_state == (tt == seq), "a carried state needs whole sequences per block"
    group = 4 if n_chunks % 4 == 0 else 1

    pad = lambda vec: jnp.zeros((1, LANES), F32).at[0, :n_heads].set(vec)
    in_specs = [pl.BlockSpec((bb, tt, d), lambda b, t: (b, t, 0)),
                pl.BlockSpec((bb, 3, d), lambda b, t: (b, 0, 0))]
    args = [x, mod]
    if has_state:
        in_specs += [pl.BlockSpec((bb, width - 1, ch), lambda b, t: (b, 0, 0)),
                     pl.BlockSpec((bb, n_heads, dk, dv), lambda b, t: (b, 0, 0, 0))]
        args += [conv_state, ssm_state]
    in_specs += [_const_spec(w_in.shape), _const_spec(w_ab.shape), _const_spec(w_conv.shape),
                 _const_spec((1, LANES)), _const_spec((1, LANES)), _const_spec((1, dv)),
                 _const_spec(w_out.shape), _const_spec((1, d)), _const_spec((1, d))]
    args += [w_in, w_ab, w_conv, pad(a_log), pad(dt_bias), norm_w.reshape(1, dv), w_out,
             ln_g.reshape(1, d), ln_b.reshape(1, d)]
    scratch = [pltpu.VMEM((bb, HALO, ch), F32),
               pltpu.VMEM((rows, ch), F32),
               pltpu.VMEM((rows, vw), F32),
               pltpu.VMEM((n_chunks, LANES, n), F32),
               pltpu.VMEM((rows, vw), F32),
               pltpu.VMEM((rows, qk), F32),
               pltpu.VMEM((rows, vw), F32),
               pltpu.VMEM((n_chunks, 2 * n, qk), F32),
               pltpu.VMEM((rows, vw), F32),
               pltpu.VMEM((rows, vw), F32)]
    if not has_state:
        scratch.append(pltpu.VMEM((n_heads, dk, dv), F32))
    kern = functools.partial(_gdn_layer_kernel, has_state=has_state, width=width, n_heads=n_heads,
                             dk=dk, dv=dv, seq_len=seq_len, col_w=min(ch, 512), group=group,
                             alpha=alpha)
    return pl.pallas_call(
        kern,
        grid=(bsz // bb, seq // tt),
        in_specs=in_specs,
        out_specs=[pl.BlockSpec((bb, tt, d), lambda b, t: (b, t, 0)),
                   pl.BlockSpec((bb, width - 1, ch), lambda b, t: (b, 0, 0)),
                   pl.BlockSpec((bb, n_heads, dk, dv), lambda b, t: (b, 0, 0, 0))],
        out_shape=[jax.ShapeDtypeStruct((bsz, seq, d), F32),
                   jax.ShapeDtypeStruct((bsz, width - 1, ch), F32),
                   jax.ShapeDtypeStruct((bsz, n_heads, dk, dv), F32)],
        scratch_shapes=scratch,
        compiler_params=pltpu.CompilerParams(
            dimension_semantics=("arbitrary", "arbitrary"), vmem_limit_bytes=VMEM_LIMIT),
        name="gdn_layer",
    )(*args)


def _block_rows(bsz, seq, target):
    if seq >= target:
        return 1, target
    return max(1, min(bsz, target // seq)), seq


def _trunk(x, mod, conv_a, conv_b, ssm_b, ln_g, ln_b, wa, wb, alpha):
    bsz, seq, _ = x.shape
    new_a, new_cb, new_s = [], [], []
    depth = mod.shape[0]
    for l in range(depth):
        i = l // 2
        if l % 2 == 0:
            bb, tt = _block_rows(bsz, seq, 512)
            x, nb = _conv_layer(x, mod[l], None if conv_a is None else conv_a[i],
                                wa["in"][i], wa["conv"][i], wa["out"][i], ln_g[l], ln_b[l],
                                bb=bb, tt=tt, alpha=alpha)
            new_a.append(nb)
        else:
            bb, tt = _block_rows(bsz, seq, 256 if conv_b is None else GDN_ROWS)
            x, nb, s = _gdn_layer(x, mod[l], None if conv_b is None else conv_b[i],
                                  None if ssm_b is None else ssm_b[i],
                                  wb["in"][i], wb["ab"][i], wb["conv"][i], wb["a_log"][i],
                                  wb["dt_bias"][i], wb["norm"][i], wb["out"][i], ln_g[l], ln_b[l],
                                  bb=bb, tt=tt, alpha=alpha)
            new_cb.append(nb)
            new_s.append(s)
    return x, jnp.stack(new_a), jnp.stack(new_cb), jnp.stack(new_s)


def kernel(x_prompt, x_sample, state_conv_a, state_conv_b, state_ssm_b, c_prompt, c_sample, w_mod, b_mod, ln_g, ln_b, wa_in, wa_conv, wa_out, wb_in, wb_conv, wb_a_log, wb_dt_bias, wb_norm, wb_out):
    depth, d, _ = w_mod.shape
    alpha = (2 * depth) ** 0.25
    bp = x_prompt.shape[0]
    n_heads = wb_a_log.shape[1]
    ch = wb_conv.shape[2]
    vw = wb_out.shape[1]

    mod = _modulation(jnp.concatenate([c_prompt, c_sample], axis=0), w_mod, b_mod)
    mod = mod.reshape(depth, -1, 3, d)
    mod_p, mod_s = mod[:, :bp], mod[:, bp:]

    n_gdn = wb_in.shape[0]
    w_ab = jnp.zeros((n_gdn, d, 2 * LANES), F32)
    w_ab = w_ab.at[:, :, :n_heads].set(wb_in[:, :, ch + vw + n_heads:])
    w_ab = w_ab.at[:, :, LANES:LANES + n_heads].set(wb_in[:, :, ch + vw:ch + vw + n_heads])
    wa = {"in": wa_in.astype(BF16), "conv": wa_conv, "out": wa_out.astype(BF16)}
    wb = {"in": wb_in[:, :, :ch + vw].astype(BF16), "ab": w_ab.astype(BF16), "conv": wb_conv,
          "a_log": wb_a_log, "dt_bias": wb_dt_bias, "norm": wb_norm, "out": wb_out.astype(BF16)}

    y_p, ca_p, cb_p, s_p = _trunk(x_prompt, mod_p, None, None, None, ln_g, ln_b, wa, wb, alpha)
    y_s, ca_s, cb_s, s_s = _trunk(x_sample, mod_s, state_conv_a, state_conv_b, state_ssm_b,
                                  ln_g, ln_b, wa, wb, alpha)
    return (y_p, y_s, ca_p, cb_p, s_p, ca_s, cb_s, s_s)
```

```python
import functools

import jax
import jax.numpy as jnp
from jax import lax
from jax.experimental import pallas as pl
from jax.experimental.pallas import tpu as pltpu

F32 = jnp.float32
BF16 = jnp.bfloat16

LN_EPS = 1e-5
NORM_EPS = 1e-6
SUBLANES = 8
LANES = 128
HALO = SUBLANES
GDN_ROWS = 64
VMEM_LIMIT = 56 * 1024 * 1024


def _dot(a, b):
    return jnp.dot(a.astype(BF16), b.astype(BF16), preferred_element_type=F32)


def _dot_nt(a, b):
    return lax.dot_general(a.astype(BF16), b.astype(BF16), (((1,), (1,)), ((), ())),
                           preferred_element_type=F32)


def _dot_tn(a, b):
    return lax.dot_general(a.astype(BF16), b.astype(BF16), (((0,), (0,)), ((), ())),
                           preferred_element_type=F32)


def _silu(x):
    return x * jax.nn.sigmoid(x)


def _split3(x):
    hi = x.astype(BF16)
    r = x - hi.astype(F32)
    mid = r.astype(BF16)
    lo = (r - mid.astype(F32)).astype(BF16)
    return hi, mid, lo


def _layer_norm(r, g, b):
    mu = jnp.mean(r, axis=-1, keepdims=True)
    d = r - mu
    var = jnp.mean(d * d, axis=-1, keepdims=True)
    return d * lax.rsqrt(var + LN_EPS) * g + b


def _mod_kernel(c_ref, w_ref, b_ref, o_ref):
    cs = _silu(c_ref[...])
    o_ref[0] = _dot(cs, w_ref[0]) + b_ref[0]


def _modulation(c_all, w_mod, b_mod):
    depth, d, n3 = w_mod.shape
    rows = c_all.shape[0]
    nb = d
    return pl.pallas_call(
        _mod_kernel,
        grid=(depth, n3 // nb),
        in_specs=[
            pl.BlockSpec((rows, d), lambda l, j: (0, 0)),
            pl.BlockSpec((1, d, nb), lambda l, j: (l, 0, j)),
            pl.BlockSpec((1, 1, nb), lambda l, j: (l, 0, j)),
        ],
        out_specs=pl.BlockSpec((1, rows, nb), lambda l, j: (l, 0, j)),
        out_shape=jax.ShapeDtypeStruct((depth, rows, n3), F32),
        compiler_params=pltpu.CompilerParams(
            dimension_semantics=("arbitrary", "arbitrary"), vmem_limit_bytes=VMEM_LIMIT),
        name="modulation",
    )(c_all, w_mod, b_mod.reshape(depth, 1, n3))


def _modulate(x_ref, mod_ref):
    x = x_ref[...]
    shift = mod_ref[:, 0:1, :]
    scale = mod_ref[:, 1:2, :]
    return x, x * (1.0 + scale) + shift


def _init_conv_halo(halo_scr, st_ref, width):
    halo_scr[...] = jnp.zeros(halo_scr.shape, F32)
    if st_ref is not None:
        halo_scr[:, HALO - (width - 1):HALO, :] = st_ref[...]


def _causal_conv(halo_scr, nst_ref, p, wconv_ref, width, sl):
    bb, tt, cw = p.shape
    ext = jnp.concatenate([halo_scr[:, :, sl], p], axis=1)
    halo_scr[:, :, sl] = ext[:, tt:, :]
    nst_ref[:, :, sl] = ext[:, HALO + tt - (width - 1):, :]
    e = ext.reshape(bb * (HALO + tt), cw)
    e1 = pltpu.roll(e, 1, axis=0)
    tap = lambda k: wconv_ref[width - 1 - k:width - k, sl]
    y = None
    for m in range((width + 1) // 2):
        pair = e * tap(2 * m)
        if 2 * m + 1 < width:
            pair = pair + e1 * tap(2 * m + 1)
        if m:
            pair = pltpu.roll(pair, 2 * m, axis=0)
        y = pair if y is None else y + pair
    return y.reshape(bb, HALO + tt, cw)[:, HALO:, :].reshape(bb * tt, cw)


def _residual_norm(x, out, mod_ref, lng_ref, lnb_ref, o_ref, alpha):
    bb, tt, d = o_ref.shape
    gate = mod_ref[:, 2:3, :]
    r = alpha * x.reshape(bb, tt, d) + gate * out.reshape(bb, tt, d)
    o_ref[...] = _layer_norm(r, lng_ref[...], lnb_ref[...])


def _const_spec(shape):
    zeros = (0,) * len(shape)
    return pl.BlockSpec(shape, lambda *_: zeros, pipeline_mode=pl.Buffered(1))


def _conv_layer_kernel(*refs, has_state, width, col_w, alpha):
    if has_state:
        (x_ref, mod_ref, st_ref, win_ref, wconv_ref, wout_ref, lng_ref, lnb_ref,
         o_ref, nst_ref, halo_scr, y_scr) = refs
    else:
        (x_ref, mod_ref, win_ref, wconv_ref, wout_ref, lng_ref, lnb_ref,
         o_ref, nst_ref, halo_scr, y_scr) = refs
        st_ref = None
    bb, tt, d = x_ref.shape
    rows = bb * tt
    dc = wconv_ref.shape[1]

    @pl.when(pl.program_id(1) == 0)
    def _():
        _init_conv_halo(halo_scr, st_ref, width)

    x, u = _modulate(x_ref, mod_ref)
    u = u.reshape(rows, d).astype(BF16)

    for j in range(dc // col_w):
        sl = slice(j * col_w, (j + 1) * col_w)
        pc = _dot(u, win_ref[:, dc + j * col_w:dc + (j + 1) * col_w])
        ph = _dot(u, win_ref[:, 2 * dc + j * col_w:2 * dc + (j + 1) * col_w])
        conv = _causal_conv(halo_scr, nst_ref, (pc * ph).reshape(bb, tt, col_w), wconv_ref, width, sl)
        pb = _dot(u, win_ref[:, sl])
        pz = _dot(u, win_ref[:, 3 * dc + j * col_w:3 * dc + (j + 1) * col_w])
        y_scr[:, sl] = (pb * conv * _silu(pz)).astype(BF16)

    out = _dot(y_scr[...], wout_ref[...])
    _residual_norm(x, out, mod_ref, lng_ref, lnb_ref, o_ref, alpha)


def _conv_layer(x, mod, state, w_in, w_conv, w_out, ln_g, ln_b, *, bb, tt, alpha):
    bsz, seq, d = x.shape
    width, dc = w_conv.shape
    has_state = state is not None
    rows = bb * tt
    seq_spec = lambda n: pl.BlockSpec((bb, n, d if n == 3 else dc), lambda b, t: (b, 0, 0))
    in_specs = [pl.BlockSpec((bb, tt, d), lambda b, t: (b, t, 0)), seq_spec(3)]
    args = [x, mod]
    if has_state:
        in_specs.append(seq_spec(width - 1))
        args.append(state)
    in_specs += [_const_spec(w_in.shape), _const_spec(w_conv.shape), _const_spec(w_out.shape),
                 _const_spec((1, d)), _const_spec((1, d))]
    args += [w_in, w_conv, w_out, ln_g.reshape(1, d), ln_b.reshape(1, d)]
    kern = functools.partial(_conv_layer_kernel, has_state=has_state, width=width,
                             col_w=min(dc, 256), alpha=alpha)
    return pl.pallas_call(
        kern,
        grid=(bsz // bb, seq // tt),
        in_specs=in_specs,
        out_specs=[pl.BlockSpec((bb, tt, d), lambda b, t: (b, t, 0)), seq_spec(width - 1)],
        out_shape=[jax.ShapeDtypeStruct((bsz, seq, d), F32),
                   jax.ShapeDtypeStruct((bsz, width - 1, dc), F32)],
        scratch_shapes=[pltpu.VMEM((bb, HALO, dc), F32), pltpu.VMEM((rows, dc), BF16)],
        compiler_params=pltpu.CompilerParams(
            dimension_semantics=("arbitrary", "arbitrary"), vmem_limit_bytes=VMEM_LIMIT),
        name="conv_layer",
    )(*args)


def _tri_masks(n, seq_len):
    i = lax.broadcasted_iota(jnp.int32, (n, n), 0)
    j = lax.broadcasted_iota(jnp.int32, (n, n), 1)
    same = (i // seq_len) == (j // seq_len)
    causal = same & (i >= j)
    strict = same & (i > j)
    eye = (i == j).astype(F32)
    levels = []
    s = 1
    while s < seq_len:
        levels.append(((i // (2 * s)) == (j // (2 * s))) & ((i % (2 * s)) >= s) & ((j % (2 * s)) < s))
        s *= 2
    return causal, strict, eye, levels


def _gdn_layer_kernel(*refs, has_state, width, n_heads, dk, dv, seq_len, col_w, group, alpha):
    if has_state:
        (x_ref, mod_ref, cst_ref, s0_ref, win_ref, wab_ref, wconv_ref, alog_ref, dtb_ref, nw_ref,
         wout_ref, lng_ref, lnb_ref, o_ref, ncst_ref, ns_ref,
         halo_scr, qkv_scr, z_scr, gt_scr, gx_scr, kb_scr, u_scr, wq_scr, attn_scr, oh_scr) = refs
        s_scr = None
    else:
        (x_ref, mod_ref, win_ref, wab_ref, wconv_ref, alog_ref, dtb_ref, nw_ref,
         wout_ref, lng_ref, lnb_ref, o_ref, ncst_ref, ns_ref,
         halo_scr, qkv_scr, z_scr, gt_scr, gx_scr, kb_scr, u_scr, wq_scr, attn_scr, oh_scr,
         s_scr) = refs
        cst_ref = s0_ref = None
    bb, tt, d = x_ref.shape
    rows = bb * tt
    ch = wconv_ref.shape[1]
    qk = n_heads * dk
    vw = n_heads * dv
    n = min(GDN_ROWS, rows)
    span = min(seq_len, n)
    n_seq = n // span
    n_chunks = rows // n
    heads = range(n_heads)
    q_sl = lambda h: slice(h * dk, (h + 1) * dk)
    k_sl = lambda h: slice(qk + h * dk, qk + (h + 1) * dk)
    v_sl = lambda h: slice(2 * qk + h * dv, 2 * qk + (h + 1) * dv)
    o_sl = lambda h: slice(h * dv, (h + 1) * dv)

    @pl.when(pl.program_id(1) == 0)
    def _():
        _init_conv_halo(halo_scr, cst_ref, width)
        if s_scr is not None:
            s_scr[...] = jnp.zeros(s_scr.shape, F32)

    x, u = _modulate(x_ref, mod_ref)
    u = u.reshape(rows, d).astype(BF16)

    for j in range(ch // col_w):
        sl = slice(j * col_w, (j + 1) * col_w)
        p = _dot(u, win_ref[:, sl]).reshape(bb, tt, col_w)
        qkv_scr[:, sl] = _silu(_causal_conv(halo_scr, ncst_ref, p, wconv_ref, width, sl))

    z_scr[...] = _silu(_dot(u, win_ref[:, ch:ch + vw]))

    ab = _dot(u, wab_ref[...])
    lane = lax.broadcasted_iota(jnp.int32, (rows, LANES), 1)
    a = ab[:, :LANES] + dtb_ref[...]
    softplus = jnp.maximum(a, 0.0) + jnp.log(1.0 + jnp.exp(-jnp.abs(a)))
    g = jnp.where(lane < n_heads, -jnp.exp(alog_ref[...]) * softplus, 0.0)
    beta = jax.nn.sigmoid(ab[:, LANES:])
    ri = lax.broadcasted_iota(jnp.int32, (rows, rows), 0)
    rj = lax.broadcasted_iota(jnp.int32, (rows, rows), 1)
    tri = (((ri // span) == (rj // span)) & (ri >= rj)).astype(BF16)
    g_hi, g_mid, g_lo = _split3(g)
    g_cum = (jnp.dot(tri, g_hi, preferred_element_type=F32)
             + jnp.dot(tri, g_mid, preferred_element_type=F32)
             + jnp.dot(tri, g_lo, preferred_element_type=F32))
    for c in range(n_chunks):
        gt_scr[c] = g_cum[c * n:(c + 1) * n, :].T

    for h in heads:
        t = qkv_scr[:, q_sl(h)]
        qkv_scr[:, q_sl(h)] = t * (lax.rsqrt(jnp.sum(t * t, axis=-1, keepdims=True) + NORM_EPS)
                                   * dk ** -0.5)
        t = qkv_scr[:, k_sl(h)]
        t = t * lax.rsqrt(jnp.sum(t * t, axis=-1, keepdims=True) + NORM_EPS)
        qkv_scr[:, k_sl(h)] = t
        beta_h = beta[:, h:h + 1]
        kb_scr[:, o_sl(h)] = t * beta_h
        qkv_scr[:, v_sl(h)] = qkv_scr[:, v_sl(h)] * beta_h
        gx_scr[:, o_sl(h)] = jnp.broadcast_to(g_cum[:, h:h + 1], (rows, dk))

    causal, strict, eye, levels = _tri_masks(n, span)
    row_seq = lax.broadcasted_iota(jnp.int32, (n, 1), 0) // span
    row_seq2 = (lax.broadcasted_iota(jnp.int32, (2 * n, 1), 0) % n) // span

    def local_body(gi, carry):
        items = []
        for ci in range(group):
            c = gi * group + ci
            rs = pl.ds(pl.multiple_of(c * n, n), n)
            items += [(c, rs, h) for h in heads]
        decay, a_mat, t_inv = [], [], []
        for c, rs, h in items:
            diff = gx_scr[rs, h * dv:h * dv + n] - gt_scr[c, h:h + 1, :]
            decay.append(jnp.where(causal, jnp.exp(jnp.where(causal, diff, 0.0)), 0.0))
        for (c, rs, h), dec in zip(items, decay):
            k = qkv_scr[rs, k_sl(h)]
            a_mat.append(jnp.where(strict, _dot_nt(kb_scr[rs, o_sl(h)], k) * dec, 0.0))
        for (c, rs, h), dec in zip(items, decay):
            attn_scr[rs, h * dv:h * dv + n] = (
                _dot_nt(qkv_scr[rs, q_sl(h)], qkv_scr[rs, k_sl(h)]) * dec)
        t_inv = [eye - jnp.where(levels[0], a, 0.0) for a in a_mat]
        for m in levels[1:]:
            y = [_dot(jnp.where(m, a, 0.0), t) for a, t in zip(a_mat, t_inv)]
            xs = [_dot(t, yy) for t, yy in zip(t_inv, y)]
            t_inv = [t - xx for t, xx in zip(t_inv, xs)]
        for (c, rs, h), t in zip(items, t_inv):
            e_x = jnp.exp(gx_scr[rs, o_sl(h)])
            rhs = jnp.concatenate([qkv_scr[rs, v_sl(h)], kb_scr[rs, o_sl(h)] * e_x], axis=1)
            sol = _dot(t, rhs)
            u_scr[rs, o_sl(h)] = sol[:, :dv]
            wq_scr[c, 0:n, o_sl(h)] = sol[:, dv:]
            wq_scr[c, n:2 * n, o_sl(h)] = qkv_scr[rs, q_sl(h)] * e_x
        return carry

    lax.fori_loop(0, n_chunks // group, local_body, 0)

    def state_body(c, carry):
        rs = pl.ds(pl.multiple_of(c * n, n), n)
        pairs = [(h, s) for h in heads for s in range(n_seq)]
        masks = [None if n_seq == 1 else (row_seq == s) for s in range(n_seq)]
        masks2 = [None if n_seq == 1 else (row_seq2 == s) for s in range(n_seq)]
        if has_state:
            s_prev = {(h, s): s0_ref[c * n_seq + s, h] for h, s in pairs}
        else:
            s_prev = {(h, 0): s_scr[h] for h in heads}
        ws = {(h, s): _dot(wq_scr[c, :, o_sl(h)], s_prev[(h, s)]) for h, s in pairs}
        v_new, v_dec, o_part, g_last = {}, {}, {}, {}
        for h in heads:
            acc = ws[(h, 0)] if n_seq == 1 else sum(
                jnp.where(masks2[s], ws[(h, s)], 0.0) for s in range(n_seq))
            v_new[h] = u_scr[rs, o_sl(h)] - acc[:n]
            o_part[h] = acc[n:]
            gx = gx_scr[rs, o_sl(h)]
            for s in range(n_seq):
                g_last[(h, s)] = gx[(s + 1) * span - 1:(s + 1) * span, :]
            g_end = jnp.concatenate(
                [jnp.broadcast_to(g_last[(h, s)], (span, dv)) for s in range(n_seq)], axis=0)
            v_dec[h] = v_new[h] * jnp.exp(g_end - gx)
        for h in heads:
            oh_scr[rs, o_sl(h)] = o_part[h] + _dot(attn_scr[rs, h * dv:h * dv + n], v_new[h])
        for h, s in pairs:
            vs = v_dec[h] if n_seq == 1 else jnp.where(masks[s], v_dec[h], 0.0)
            s_new = (s_prev[(h, s)] * jnp.exp(g_last[(h, s)])
                     + _dot_tn(qkv_scr[rs, k_sl(h)], vs))
            if has_state:
                ns_ref[c * n_seq + s, h] = s_new
            else:
                s_scr[h] = s_new
        return carry

    lax.fori_loop(0, n_chunks, state_body, 0)
    if not has_state:
        ns_ref[0] = s_scr[...]

    for h in heads:
        o = oh_scr[:, o_sl(h)]
        o = o * lax.rsqrt(jnp.mean(o * o, axis=-1, keepdims=True) + NORM_EPS) * nw_ref[...]
        oh_scr[:, o_sl(h)] = o * z_scr[:, o_sl(h)]
    out = _dot(oh_scr[...], wout_ref[...])
    _residual_norm(x, out, mod_ref, lng_ref, lnb_ref, o_ref, alpha)


def _gdn_layer(x, mod, conv_state, ssm_state, w_in, w_ab, w_conv, a_log, dt_bias, norm_w, w_out,
               ln_g, ln_b, *, bb, tt, alpha):
    bsz, seq, d = x.shape
    width, ch = w_conv.shape
    n_heads = a_log.shape[0]
    vw = w_out.shape[0]
    dv = vw // n_heads
    dk = (ch - vw) // 2 // n_heads
    qk = n_heads * dk
    has_state = conv_state is not None
    rows = bb * tt
    seq_len = min(seq, GDN_ROWS)
    n = min(GDN_ROWS, rows)
    n_chunks = rows // n
    assert dk == dv == LANES, "one head per lane tile"
    assert rows % n == 0 and n % seq_len == 0
    assert has_state == (tt == seq), "a carried state needs whole sequences per block"
    group = 4 if n_chunks % 4 == 0 else 1

    pad = lambda vec: jnp.zeros((1, LANES), F32).at[0, :n_heads].set(vec)
    in_specs = [pl.BlockSpec((bb, tt, d), lambda b, t: (b, t, 0)),
                pl.BlockSpec((bb, 3, d), lambda b, t: (b, 0, 0))]
    args = [x, mod]
    if has_state:
        in_specs += [pl.BlockSpec((bb, width - 1, ch), lambda b, t: (b, 0, 0)),
                     pl.BlockSpec((bb, n_heads, dk, dv), lambda b, t: (b, 0, 0, 0))]
        args += [conv_state, ssm_state]
    in_specs += [_const_spec(w_in.shape), _const_spec(w_ab.shape), _const_spec(w_conv.shape),
                 _const_spec((1, LANES)), _const_spec((1, LANES)), _const_spec((1, dv)),
                 _const_spec(w_out.shape), _const_spec((1, d)), _const_spec((1, d))]
    args += [w_in, w_ab, w_conv, pad(a_log), pad(dt_bias), norm_w.reshape(1, dv), w_out,
             ln_g.reshape(1, d), ln_b.reshape(1, d)]
    scratch = [pltpu.VMEM((bb, HALO, ch), F32),
               pltpu.VMEM((rows, ch), F32),
               pltpu.VMEM((rows, vw), F32),
               pltpu.VMEM((n_chunks, LANES, n), F32),
               pltpu.VMEM((rows, vw), F32),
               pltpu.VMEM((rows, qk), F32),
               pltpu.VMEM((rows, vw), F32),
               pltpu.VMEM((n_chunks, 2 * n, qk), F32),
               pltpu.VMEM((rows, vw), F32),
               pltpu.VMEM((rows, vw), F32)]
    if not has_state:
        scratch.append(pltpu.VMEM((n_heads, dk, dv), F32))
    kern = functools.partial(_gdn_layer_kernel, has_state=has_state, width=width, n_heads=n_heads,
                             dk=dk, dv=dv, seq_len=seq_len, col_w=min(ch, 512), group=group,
                             alpha=alpha)
    return pl.pallas_call(
        kern,
        grid=(bsz // bb, seq // tt),
        in_specs=in_specs,
        out_specs=[pl.BlockSpec((bb, tt, d), lambda b, t: (b, t, 0)),
                   pl.BlockSpec((bb, width - 1, ch), lambda b, t: (b, 0, 0)),
                   pl.BlockSpec((bb, n_heads, dk, dv), lambda b, t: (b, 0, 0, 0))],
        out_shape=[jax.ShapeDtypeStruct((bsz, seq, d), F32),
                   jax.ShapeDtypeStruct((bsz, width - 1, ch), F32),
                   jax.ShapeDtypeStruct((bsz, n_heads, dk, dv), F32)],
        scratch_shapes=scratch,
        compiler_params=pltpu.CompilerParams(
            dimension_semantics=("arbitrary", "arbitrary"), vmem_limit_bytes=VMEM_LIMIT),
        name="gdn_layer",
    )(*args)


def _block_rows(bsz, seq, target):
    if seq >= target:
        return 1, target
    return max(1, min(bsz, target // seq)), seq


def _trunk(x, mod, conv_a, conv_b, ssm_b, ln_g, ln_b, wa, wb, alpha):
    bsz, seq, _ = x.shape
    new_a, new_cb, new_s = [], [], []
    depth = mod.shape[0]
    for l in range(depth):
        i = l // 2
        if l % 2 == 0:
            bb, tt = _block_rows(bsz, seq, 512)
            x, nb = _conv_layer(x, mod[l], None if conv_a is None else conv_a[i],
                                wa["in"][i], wa["conv"][i], wa["out"][i], ln_g[l], ln_b[l],
                                bb=bb, tt=tt, alpha=alpha)
            new_a.append(nb)
        else:
            bb, tt = _block_rows(bsz, seq, 512 if conv_b is None else GDN_ROWS)
            x, nb, s = _gdn_layer(x, mod[l], None if conv_b is None else conv_b[i],
                                  None if ssm_b is None else ssm_b[i],
                                  wb["in"][i], wb["ab"][i], wb["conv"][i], wb["a_log"][i],
                                  wb["dt_bias"][i], wb["norm"][i], wb["out"][i], ln_g[l], ln_b[l],
                                  bb=bb, tt=tt, alpha=alpha)
            new_cb.append(nb)
            new_s.append(s)
    return x, jnp.stack(new_a), jnp.stack(new_cb), jnp.stack(new_s)


def kernel(x_prompt, x_sample, state_conv_a, state_conv_b, state_ssm_b, c_prompt, c_sample, w_mod, b_mod, ln_g, ln_b, wa_in, wa_conv, wa_out, wb_in, wb_conv, wb_a_log, wb_dt_bias, wb_norm, wb_out):
    depth, d, _ = w_mod.shape
    alpha = (2 * depth) ** 0.25
    bp = x_prompt.shape[0]
    n_heads = wb_a_log.shape[1]
    ch = wb_conv.shape[2]
    vw = wb_out.shape[1]

    mod = _modulation(jnp.concatenate([c_prompt, c_sample], axis=0), w_mod, b_mod)
    mod = mod.reshape(depth, -1, 3, d)
    mod_p, mod_s = mod[:, :bp], mod[:, bp:]

    n_gdn = wb_in.shape[0]
    w_ab = jnp.zeros((n_gdn, d, 2 * LANES), F32)
    w_ab = w_ab.at[:, :, :n_heads].set(wb_in[:, :, ch + vw + n_heads:])
    w_ab = w_ab.at[:, :, LANES:LANES + n_heads].set(wb_in[:, :, ch + vw:ch + vw + n_heads])
    wa = {"in": wa_in.astype(BF16), "conv": wa_conv, "out": wa_out.astype(BF16)}
    wb = {"in": wb_in[:, :, :ch + vw].astype(BF16), "ab": w_ab.astype(BF16), "conv": wb_conv,
          "a_log": wb_a_log, "dt_bias": wb_dt_bias, "norm": wb_norm, "out": wb_out.astype(BF16)}

    y_p, ca_p, cb_p, s_p = _trunk(x_prompt, mod_p, None, None, None, ln_g, ln_b, wa, wb, alpha)
    y_s, ca_s, cb_s, s_s = _trunk(x_sample, mod_s, state_conv_a, state_conv_b, state_ssm_b,
                                  ln_g, ln_b, wa, wb, alpha)
    return (y_p, y_s, ca_p, cb_p, s_p, ca_s, cb_s, s_s)
```

```python
import functools

import jax
import jax.numpy as jnp
from jax import lax
from jax.experimental import pallas as pl
from jax.experimental.pallas import tpu as pltpu

F32 = jnp.float32
BF16 = jnp.bfloat16

LN_EPS = 1e-5
NORM_EPS = 1e-6
SUBLANES = 8
LANES = 128
HALO = SUBLANES
GDN_ROWS = 64
VMEM_LIMIT = 56 * 1024 * 1024


def _dot(a, b):
    return jnp.dot(a.astype(BF16), b.astype(BF16), preferred_element_type=F32)


def _dot_nt(a, b):
    return lax.dot_general(a.astype(BF16), b.astype(BF16), (((1,), (1,)), ((), ())),
                           preferred_element_type=F32)


def _dot_tn(a, b):
    return lax.dot_general(a.astype(BF16), b.astype(BF16), (((0,), (0,)), ((), ())),
                           preferred_element_type=F32)


def _silu(x):
    return x * jax.nn.sigmoid(x)


def _split3(x):
    hi = x.astype(BF16)
    r = x - hi.astype(F32)
    mid = r.astype(BF16)
    lo = (r - mid.astype(F32)).astype(BF16)
    return hi, mid, lo


def _layer_norm(r, g, b):
    mu = jnp.mean(r, axis=-1, keepdims=True)
    d = r - mu
    var = jnp.mean(d * d, axis=-1, keepdims=True)
    return d * lax.rsqrt(var + LN_EPS) * g + b


def _mod_kernel(c_ref, w_ref, b_ref, o_ref):
    cs = _silu(c_ref[...])
    o_ref[0] = _dot(cs, w_ref[0]) + b_ref[0]


def _modulation(c_all, w_mod, b_mod):
    depth, d, n3 = w_mod.shape
    rows = c_all.shape[0]
    nb = d
    return pl.pallas_call(
        _mod_kernel,
        grid=(depth, n3 // nb),
        in_specs=[
            pl.BlockSpec((rows, d), lambda l, j: (0, 0)),
            pl.BlockSpec((1, d, nb), lambda l, j: (l, 0, j)),
            pl.BlockSpec((1, 1, nb), lambda l, j: (l, 0, j)),
        ],
        out_specs=pl.BlockSpec((1, rows, nb), lambda l, j: (l, 0, j)),
        out_shape=jax.ShapeDtypeStruct((depth, rows, n3), F32),
        compiler_params=pltpu.CompilerParams(
            dimension_semantics=("arbitrary", "arbitrary"), vmem_limit_bytes=VMEM_LIMIT),
        name="modulation",
    )(c_all, w_mod, b_mod.reshape(depth, 1, n3))


def _modulate(x_ref, mod_ref):
    x = x_ref[...]
    shift = mod_ref[:, 0:1, :]
    scale = mod_ref[:, 1:2, :]
    return x, x * (1.0 + scale) + shift


def _init_conv_halo(halo_scr, st_ref, width):
    halo_scr[...] = jnp.zeros(halo_scr.shape, F32)
    if st_ref is not None:
        halo_scr[:, HALO - (width - 1):HALO, :] = st_ref[...]


def _causal_conv(halo_scr, nst_ref, p, wconv_ref, width, sl):
    bb, tt, cw = p.shape
    ext = jnp.concatenate([halo_scr[:, :, sl], p], axis=1)
    halo_scr[:, :, sl] = ext[:, tt:, :]
    nst_ref[:, :, sl] = ext[:, HALO + tt - (width - 1):, :]
    e = ext.reshape(bb * (HALO + tt), cw)
    e1 = pltpu.roll(e, 1, axis=0)
    tap = lambda k: wconv_ref[width - 1 - k:width - k, sl]
    y = None
    for m in range((width + 1) // 2):
        pair = e * tap(2 * m)
        if 2 * m + 1 < width:
            pair = pair + e1 * tap(2 * m + 1)
        if m:
            pair = pltpu.roll(pair, 2 * m, axis=0)
        y = pair if y is None else y + pair
    return y.reshape(bb, HALO + tt, cw)[:, HALO:, :].reshape(bb * tt, cw)


def _residual_norm(x, out, mod_ref, lng_ref, lnb_ref, o_ref, alpha):
    bb, tt, d = o_ref.shape
    gate = mod_ref[:, 2:3, :]
    r = alpha * x.reshape(bb, tt, d) + gate * out.reshape(bb, tt, d)
    o_ref[...] = _layer_norm(r, lng_ref[...], lnb_ref[...])


def _const_spec(shape):
    zeros = (0,) * len(shape)
    return pl.BlockSpec(shape, lambda *_: zeros, pipeline_mode=pl.Buffered(1))


def _conv_layer_kernel(*refs, has_state, width, col_w, alpha):
    if has_state:
        (x_ref, mod_ref, st_ref, win_ref, wconv_ref, wout_ref, lng_ref, lnb_ref,
         o_ref, nst_ref, halo_scr, y_scr) = refs
    else:
        (x_ref, mod_ref, win_ref, wconv_ref, wout_ref, lng_ref, lnb_ref,
         o_ref, nst_ref, halo_scr, y_scr) = refs
        st_ref = None
    bb, tt, d = x_ref.shape
    rows = bb * tt
    dc = wconv_ref.shape[1]

    @pl.when(pl.program_id(1) == 0)
    def _():
        _init_conv_halo(halo_scr, st_ref, width)

    x, u = _modulate(x_ref, mod_ref)
    u = u.reshape(rows, d).astype(BF16)

    for j in range(dc // col_w):
        sl = slice(j * col_w, (j + 1) * col_w)
        pc = _dot(u, win_ref[:, dc + j * col_w:dc + (j + 1) * col_w])
        ph = _dot(u, win_ref[:, 2 * dc + j * col_w:2 * dc + (j + 1) * col_w])
        conv = _causal_conv(halo_scr, nst_ref, (pc * ph).reshape(bb, tt, col_w), wconv_ref, width, sl)
        pb = _dot(u, win_ref[:, sl])
        pz = _dot(u, win_ref[:, 3 * dc + j * col_w:3 * dc + (j + 1) * col_w])
        y_scr[:, sl] = (pb * conv * _silu(pz)).astype(BF16)

    out = _dot(y_scr[...], wout_ref[...])
    _residual_norm(x, out, mod_ref, lng_ref, lnb_ref, o_ref, alpha)


def _conv_layer(x, mod, state, w_in, w_conv, w_out, ln_g, ln_b, *, bb, tt, alpha):
    bsz, seq, d = x.shape
    width, dc = w_conv.shape
    has_state = state is not None
    rows = bb * tt
    seq_spec = lambda n: pl.BlockSpec((bb, n, d if n == 3 else dc), lambda b, t: (b, 0, 0))
    in_specs = [pl.BlockSpec((bb, tt, d), lambda b, t: (b, t, 0)), seq_spec(3)]
    args = [x, mod]
    if has_state:
        in_specs.append(seq_spec(width - 1))
        args.append(state)
    in_specs += [_const_spec(w_in.shape), _const_spec(w_conv.shape), _const_spec(w_out.shape),
                 _const_spec((1, d)), _const_spec((1, d))]
    args += [w_in, w_conv, w_out, ln_g.reshape(1, d), ln_b.reshape(1, d)]
    kern = functools.partial(_conv_layer_kernel, has_state=has_state, width=width,
                             col_w=min(dc, 256), alpha=alpha)
    return pl.pallas_call(
        kern,
        grid=(bsz // bb, seq // tt),
        in_specs=in_specs,
        out_specs=[pl.BlockSpec((bb, tt, d), lambda b, t: (b, t, 0)), seq_spec(width - 1)],
        out_shape=[jax.ShapeDtypeStruct((bsz, seq, d), F32),
                   jax.ShapeDtypeStruct((bsz, width - 1, dc), F32)],
        scratch_shapes=[pltpu.VMEM((bb, HALO, dc), F32), pltpu.VMEM((rows, dc), BF16)],
        compiler_params=pltpu.CompilerParams(
            dimension_semantics=("arbitrary", "arbitrary"), vmem_limit_bytes=VMEM_LIMIT),
        name="conv_layer",
    )(*args)


def _tri_masks(n, seq_len):
    i = lax.broadcasted_iota(jnp.int32, (n, n), 0)
    j = lax.broadcasted_iota(jnp.int32, (n, n), 1)
    same = (i // seq_len) == (j // seq_len)
    causal = same & (i >= j)
    strict = same & (i > j)
    eye = (i == j).astype(F32)
    levels = []
    s = 1
    while s < seq_len:
        levels.append(((i // (2 * s)) == (j // (2 * s))) & ((i % (2 * s)) >= s) & ((j % (2 * s)) < s))
        s *= 2
    return causal, strict, eye, levels


def _gdn_layer_kernel(*refs, has_state, width, n_heads, dk, dv, seq_len, col_w, group, alpha):
    if has_state:
        (x_ref, mod_ref, cst_ref, s0_ref, win_ref, wab_ref, wconv_ref, alog_ref, dtb_ref, nw_ref,
         wout_ref, lng_ref, lnb_ref, o_ref, ncst_ref, ns_ref,
         halo_scr, qkv_scr, z_scr, gt_scr, gx_scr, kb_scr, u_scr, wq_scr, attn_scr, oh_scr) = refs
        s_scr = None
    else:
        (x_ref, mod_ref, win_ref, wab_ref, wconv_ref, alog_ref, dtb_ref, nw_ref,
         wout_ref, lng_ref, lnb_ref, o_ref, ncst_ref, ns_ref,
         halo_scr, qkv_scr, z_scr, gt_scr, gx_scr, kb_scr, u_scr, wq_scr, attn_scr, oh_scr,
         s_scr) = refs
        cst_ref = s0_ref = None
    bb, tt, d = x_ref.shape
    rows = bb * tt
    ch = wconv_ref.shape[1]
    qk = n_heads * dk
    vw = n_heads * dv
    n = min(GDN_ROWS, rows)
    span = min(seq_len, n)
    n_seq = n // span
    n_chunks = rows // n
    heads = range(n_heads)
    q_sl = lambda h: slice(h * dk, (h + 1) * dk)
    k_sl = lambda h: slice(qk + h * dk, qk + (h + 1) * dk)
    v_sl = lambda h: slice(2 * qk + h * dv, 2 * qk + (h + 1) * dv)
    o_sl = lambda h: slice(h * dv, (h + 1) * dv)

    @pl.when(pl.program_id(1) == 0)
    def _():
        _init_conv_halo(halo_scr, cst_ref, width)
        if s_scr is not None:
            s_scr[...] = jnp.zeros(s_scr.shape, F32)

    x, u = _modulate(x_ref, mod_ref)
    u = u.reshape(rows, d).astype(BF16)

    for j in range(ch // col_w):
        sl = slice(j * col_w, (j + 1) * col_w)
        p = _dot(u, win_ref[:, sl]).reshape(bb, tt, col_w)
        qkv_scr[:, sl] = _silu(_causal_conv(halo_scr, ncst_ref, p, wconv_ref, width, sl))

    z_scr[...] = _silu(_dot(u, win_ref[:, ch:ch + vw]))

    ab = _dot(u, wab_ref[...])
    lane = lax.broadcasted_iota(jnp.int32, (rows, LANES), 1)
    a = ab[:, :LANES] + dtb_ref[...]
    softplus = jnp.maximum(a, 0.0) + jnp.log(1.0 + jnp.exp(-jnp.abs(a)))
    g = jnp.where(lane < n_heads, -jnp.exp(alog_ref[...]) * softplus, 0.0)
    beta = jax.nn.sigmoid(ab[:, LANES:])
    ri = lax.broadcasted_iota(jnp.int32, (rows, rows), 0)
    rj = lax.broadcasted_iota(jnp.int32, (rows, rows), 1)
    tri = (((ri // span) == (rj // span)) & (ri >= rj)).astype(BF16)
    g_hi, g_mid, g_lo = _split3(g)
    g_cum = (jnp.dot(tri, g_hi, preferred_element_type=F32)
             + jnp.dot(tri, g_mid, preferred_element_type=F32)
             + jnp.dot(tri, g_lo, preferred_element_type=F32))
    for c in range(n_chunks):
        gt_scr[c] = g_cum[c * n:(c + 1) * n, :].T

    for h in heads:
        t = qkv_scr[:, q_sl(h)]
        qkv_scr[:, q_sl(h)] = t * (lax.rsqrt(jnp.sum(t * t, axis=-1, keepdims=True) + NORM_EPS)
                                   * dk ** -0.5)
        t = qkv_scr[:, k_sl(h)]
        t = t * lax.rsqrt(jnp.sum(t * t, axis=-1, keepdims=True) + NORM_EPS)
        qkv_scr[:, k_sl(h)] = t
        beta_h = beta[:, h:h + 1]
        kb_scr[:, o_sl(h)] = t * beta_h
        qkv_scr[:, v_sl(h)] = qkv_scr[:, v_sl(h)] * beta_h
        gx_scr[:, o_sl(h)] = jnp.broadcast_to(g_cum[:, h:h + 1], (rows, dk))

    causal, strict, eye, levels = _tri_masks(n, span)
    row_seq = lax.broadcasted_iota(jnp.int32, (n, 1), 0) // span
    row_seq2 = (lax.broadcasted_iota(jnp.int32, (2 * n, 1), 0) % n) // span

    def local_body(gi, carry):
        items = []
        for ci in range(group):
            c = gi * group + ci
            rs = pl.ds(pl.multiple_of(c * n, n), n)
            items += [(c, rs, h) for h in heads]
        decay, a_mat, t_inv = [], [], []
        for c, rs, h in items:
            diff = gx_scr[rs, h * dv:h * dv + n] - gt_scr[c, h:h + 1, :]
            decay.append(jnp.where(causal, jnp.exp(jnp.where(causal, diff, 0.0)), 0.0))
        for (c, rs, h), dec in zip(items, decay):
            k = qkv_scr[rs, k_sl(h)]
            a_mat.append(jnp.where(strict, _dot_nt(kb_scr[rs, o_sl(h)], k) * dec, 0.0))
        for (c, rs, h), dec in zip(items, decay):
            attn_scr[rs, h * dv:h * dv + n] = (
                _dot_nt(qkv_scr[rs, q_sl(h)], qkv_scr[rs, k_sl(h)]) * dec)
        t_inv = [eye - jnp.where(levels[0], a, 0.0) for a in a_mat]
        for m in levels[1:]:
            y = [_dot(jnp.where(m, a, 0.0), t) for a, t in zip(a_mat, t_inv)]
            xs = [_dot(t, yy) for t, yy in zip(t_inv, y)]
            t_inv = [t - xx for t, xx in zip(t_inv, xs)]
        for (c, rs, h), t in zip(items, t_inv):
            e_x = jnp.exp(gx_scr[rs, o_sl(h)])
            rhs = jnp.concatenate([qkv_scr[rs, v_sl(h)], kb_scr[rs, o_sl(h)] * e_x], axis=1)
            sol = _dot(t, rhs)
            u_scr[rs, o_sl(h)] = sol[:, :dv]
            wq_scr[c, 0:n, o_sl(h)] = sol[:, dv:]
            wq_scr[c, n:2 * n, o_sl(h)] = qkv_scr[rs, q_sl(h)] * e_x
        return carry

    lax.fori_loop(0, n_chunks // group, local_body, 0)

    chunks_per_seq = tt // n if not has_state else n_chunks

    def state_body(c, carry):
        if has_state:
            lanes = [(0, s) for s in range(n_seq)]
            rows_of = lambda q: pl.ds(pl.multiple_of(c * n, n), n)
            chunk_of = lambda q: c
        else:
            lanes = [(q, 0) for q in range(bb)]
            rows_of = lambda q: pl.ds(pl.multiple_of(q * tt + c * n, n), n)
            chunk_of = lambda q: q * chunks_per_seq + c
        groups = sorted({q for q, _ in lanes})
        masks = [None if n_seq == 1 else (row_seq == s) for s in range(n_seq)]
        masks2 = [None if n_seq == 1 else (row_seq2 == s) for s in range(n_seq)]
        if has_state:
            s_prev = {(h, q, s): s0_ref[c * n_seq + s, h] for h in heads for q, s in lanes}
        else:
            s_prev = {(h, q, s): s_scr[q, h] for h in heads for q, s in lanes}
        ws = {(h, q, s): _dot(wq_scr[chunk_of(q), :, o_sl(h)], s_prev[(h, q, s)])
              for h in heads for q, s in lanes}
        v_new, v_dec, o_part, g_last = {}, {}, {}, {}
        for h in heads:
            for q in groups:
                acc = ws[(h, q, 0)] if n_seq == 1 else sum(
                    jnp.where(masks2[s], ws[(h, q, s)], 0.0) for s in range(n_seq))
                v_new[h, q] = u_scr[rows_of(q), o_sl(h)] - acc[:n]
                o_part[h, q] = acc[n:]
                gx = gx_scr[rows_of(q), o_sl(h)]
                for s in range(n_seq):
                    g_last[h, q, s] = gx[(s + 1) * span - 1:(s + 1) * span, :]
                g_end = jnp.concatenate(
                    [jnp.broadcast_to(g_last[h, q, s], (span, dv)) for s in range(n_seq)], axis=0)
                v_dec[h, q] = v_new[h, q] * jnp.exp(g_end - gx)
        for h in heads:
            for q in groups:
                oh_scr[rows_of(q), o_sl(h)] = o_part[h, q] + _dot(
                    attn_scr[rows_of(q), h * dv:h * dv + n], v_new[h, q])
        for h in heads:
            for q, s in lanes:
                vs = v_dec[h, q] if n_seq == 1 else jnp.where(masks[s], v_dec[h, q], 0.0)
                s_new = (s_prev[(h, q, s)] * jnp.exp(g_last[h, q, s])
                         + _dot_tn(qkv_scr[rows_of(q), k_sl(h)], vs))
                if has_state:
                    ns_ref[c * n_seq + s, h] = s_new
                else:
                    s_scr[q, h] = s_new
        return carry

    lax.fori_loop(0, chunks_per_seq, state_body, 0)
    if not has_state:
        ns_ref[...] = s_scr[...]

    for h in heads:
        o = oh_scr[:, o_sl(h)]
        o = o * lax.rsqrt(jnp.mean(o * o, axis=-1, keepdims=True) + NORM_EPS) * nw_ref[...]
        oh_scr[:, o_sl(h)] = o * z_scr[:, o_sl(h)]
    out = _dot(oh_scr[...], wout_ref[...])
    _residual_norm(x, out, mod_ref, lng_ref, lnb_ref, o_ref, alpha)


def _gdn_layer(x, mod, conv_state, ssm_state, w_in, w_ab, w_conv, a_log, dt_bias, norm_w, w_out,
               ln_g, ln_b, *, bb, tt, alpha):
    bsz, seq, d = x.shape
    width, ch = w_conv.shape
    n_heads = a_log.shape[0]
    vw = w_out.shape[0]
    dv = vw // n_heads
    dk = (ch - vw) // 2 // n_heads
    qk = n_heads * dk
    has_state = conv_state is not None
    rows = bb * tt
    seq_len = min(seq, GDN_ROWS)
    n = min(GDN_ROWS, rows)
    n_chunks = rows // n
    assert dk == dv == LANES, "one head per lane tile"
    assert rows % n == 0 and n % seq_len == 0
    assert has_state == (tt == seq), "a carried state needs whole sequences per block"
    group = 4 if n_chunks % 4 == 0 else 1

    pad = lambda vec: jnp.zeros((1, LANES), F32).at[0, :n_heads].set(vec)
    in_specs = [pl.BlockSpec((bb, tt, d), lambda b, t: (b, t, 0)),
                pl.BlockSpec((bb, 3, d), lambda b, t: (b, 0, 0))]
    args = [x, mod]
    if has_state:
        in_specs += [pl.BlockSpec((bb, width - 1, ch), lambda b, t: (b, 0, 0)),
                     pl.BlockSpec((bb, n_heads, dk, dv), lambda b, t: (b, 0, 0, 0))]
        args += [conv_state, ssm_state]
    in_specs += [_const_spec(w_in.shape), _const_spec(w_ab.shape), _const_spec(w_conv.shape),
                 _const_spec((1, LANES)), _const_spec((1, LANES)), _const_spec((1, dv)),
                 _const_spec(w_out.shape), _const_spec((1, d)), _const_spec((1, d))]
    args += [w_in, w_ab, w_conv, pad(a_log), pad(dt_bias), norm_w.reshape(1, dv), w_out,
             ln_g.reshape(1, d), ln_b.reshape(1, d)]
    scratch = [pltpu.VMEM((bb, HALO, ch), F32),
               pltpu.VMEM((rows, ch), F32),
               pltpu.VMEM((rows, vw), F32),
               pltpu.VMEM((n_chunks, LANES, n), F32),
               pltpu.VMEM((rows, vw), F32),
               pltpu.VMEM((rows, qk), F32),
               pltpu.VMEM((rows, vw), F32),
               pltpu.VMEM((n_chunks, 2 * n, qk), F32),
               pltpu.VMEM((rows, vw), F32),
               pltpu.VMEM((rows, vw), F32)]
    if not has_state:
        scratch.append(pltpu.VMEM((bb, n_heads, dk, dv), F32))
    kern = functools.partial(_gdn_layer_kernel, has_state=has_state, width=width, n_heads=n_heads,
                             dk=dk, dv=dv, seq_len=seq_len, col_w=min(ch, 512), group=group,
                             alpha=alpha)
    return pl.pallas_call(
        kern,
        grid=(bsz // bb, seq // tt),
        in_specs=in_specs,
        out_specs=[pl.BlockSpec((bb, tt, d), lambda b, t: (b, t, 0)),
                   pl.BlockSpec((bb, width - 1, ch), lambda b, t: (b, 0, 0)),
                   pl.BlockSpec((bb, n_heads, dk, dv), lambda b, t: (b, 0, 0, 0))],
        out_shape=[jax.ShapeDtypeStruct((bsz, seq, d), F32),
                   jax.ShapeDtypeStruct((bsz, width - 1, ch), F32),
                   jax.ShapeDtypeStruct((bsz, n_heads, dk, dv), F32)],
        scratch_shapes=scratch,
        compiler_params=pltpu.CompilerParams(
            dimension_semantics=("arbitrary", "arbitrary"), vmem_limit_bytes=VMEM_LIMIT),
        name="gdn_layer",
    )(*args)


def _block_rows(bsz, seq, target, long_seqs=1):
    if seq * long_seqs >= target:
        return long_seqs, target // long_seqs
    return max(1, min(bsz, target // seq)), seq


def _trunk(x, mod, conv_a, conv_b, ssm_b, ln_g, ln_b, wa, wb, alpha):
    bsz, seq, _ = x.shape
    new_a, new_cb, new_s = [], [], []
    depth = mod.shape[0]
    for l in range(depth):
        i = l // 2
        if l % 2 == 0:
            bb, tt = _block_rows(bsz, seq, 512)
            x, nb = _conv_layer(x, mod[l], None if conv_a is None else conv_a[i],
                                wa["in"][i], wa["conv"][i], wa["out"][i], ln_g[l], ln_b[l],
                                bb=bb, tt=tt, alpha=alpha)
            new_a.append(nb)
        else:
            bb, tt = _block_rows(bsz, seq, *((512, 2) if conv_b is None else (GDN_ROWS,)))
            x, nb, s = _gdn_layer(x, mod[l], None if conv_b is None else conv_b[i],
                                  None if ssm_b is None else ssm_b[i],
                                  wb["in"][i], wb["ab"][i], wb["conv"][i], wb["a_log"][i],
                                  wb["dt_bias"][i], wb["norm"][i], wb["out"][i], ln_g[l], ln_b[l],
                                  bb=bb, tt=tt, alpha=alpha)
            new_cb.append(nb)
            new_s.append(s)
    return x, jnp.stack(new_a), jnp.stack(new_cb), jnp.stack(new_s)


def kernel(x_prompt, x_sample, state_conv_a, state_conv_b, state_ssm_b, c_prompt, c_sample, w_mod, b_mod, ln_g, ln_b, wa_in, wa_conv, wa_out, wb_in, wb_conv, wb_a_log, wb_dt_bias, wb_norm, wb_out):
    depth, d, _ = w_mod.shape
    alpha = (2 * depth) ** 0.25
    bp = x_prompt.shape[0]
    n_heads = wb_a_log.shape[1]
    ch = wb_conv.shape[2]
    vw = wb_out.shape[1]

    mod = _modulation(jnp.concatenate([c_prompt, c_sample], axis=0), w_mod, b_mod)
    mod = mod.reshape(depth, -1, 3, d)
    mod_p, mod_s = mod[:, :bp], mod[:, bp:]

    n_gdn = wb_in.shape[0]
    w_ab = jnp.zeros((n_gdn, d, 2 * LANES), F32)
    w_ab = w_ab.at[:, :, :n_heads].set(wb_in[:, :, ch + vw + n_heads:])
    w_ab = w_ab.at[:, :, LANES:LANES + n_heads].set(wb_in[:, :, ch + vw:ch + vw + n_heads])
    wa = {"in": wa_in.astype(BF16), "conv": wa_conv, "out": wa_out.astype(BF16)}
    wb = {"in": wb_in[:, :, :ch + vw].astype(BF16), "ab": w_ab.astype(BF16), "conv": wb_conv,
          "a_log": wb_a_log, "dt_bias": wb_dt_bias, "norm": wb_norm, "out": wb_out.astype(BF16)}

    y_p, ca_p, cb_p, s_p = _trunk(x_prompt, mod_p, None, None, None, ln_g, ln_b, wa, wb, alpha)
    y_s, ca_s, cb_s, s_s = _trunk(x_sample, mod_s, state_conv_a, state_conv_b, state_ssm_b,
                                  ln_g, ln_b, wa, wb, alpha)
    return (y_p, y_s, ca_p, cb_p, s_p, ca_s, cb_s, s_s)
```

```python
import functools

import jax
import jax.numpy as jnp
from jax import lax
from jax.experimental import pallas as pl
from jax.experimental.pallas import tpu as pltpu

F32 = jnp.float32
BF16 = jnp.bfloat16

LN_EPS = 1e-5
NORM_EPS = 1e-6
SUBLANES = 8
LANES = 128
HALO = SUBLANES
GDN_ROWS = 64
VMEM_LIMIT = 56 * 1024 * 1024


def _dot(a, b):
    return jnp.dot(a.astype(BF16), b.astype(BF16), preferred_element_type=F32)


def _dot_nt(a, b):
    return lax.dot_general(a.astype(BF16), b.astype(BF16), (((1,), (1,)), ((), ())),
                           preferred_element_type=F32)


def _dot_tn(a, b):
    return lax.dot_general(a.astype(BF16), b.astype(BF16), (((0,), (0,)), ((), ())),
                           preferred_element_type=F32)


def _silu(x):
    return x * jax.nn.sigmoid(x)


def _split3(x):
    hi = x.astype(BF16)
    r = x - hi.astype(F32)
    mid = r.astype(BF16)
    lo = (r - mid.astype(F32)).astype(BF16)
    return hi, mid, lo


def _layer_norm(r, g, b):
    mu = jnp.mean(r, axis=-1, keepdims=True)
    d = r - mu
    var = jnp.mean(d * d, axis=-1, keepdims=True)
    return d * lax.rsqrt(var + LN_EPS) * g + b


def _mod_kernel(c_ref, w_ref, b_ref, o_ref):
    cs = _silu(c_ref[...])
    o_ref[0] = _dot(cs, w_ref[0]) + b_ref[0]


def _modulation(c_all, w_mod, b_mod):
    depth, d, n3 = w_mod.shape
    rows = c_all.shape[0]
    nb = d
    return pl.pallas_call(
        _mod_kernel,
        grid=(depth, n3 // nb),
        in_specs=[
            pl.BlockSpec((rows, d), lambda l, j: (0, 0)),
            pl.BlockSpec((1, d, nb), lambda l, j: (l, 0, j)),
            pl.BlockSpec((1, 1, nb), lambda l, j: (l, 0, j)),
        ],
        out_specs=pl.BlockSpec((1, rows, nb), lambda l, j: (l, 0, j)),
        out_shape=jax.ShapeDtypeStruct((depth, rows, n3), F32),
        compiler_params=pltpu.CompilerParams(
            dimension_semantics=("arbitrary", "arbitrary"), vmem_limit_bytes=VMEM_LIMIT),
        name="modulation",
    )(c_all, w_mod, b_mod.reshape(depth, 1, n3))


def _modulate(x_ref, mod_ref):
    x = x_ref[...]
    shift = mod_ref[:, 0:1, :]
    scale = mod_ref[:, 1:2, :]
    return x, x * (1.0 + scale) + shift


def _init_conv_halo(halo_scr, st_ref, width):
    halo_scr[...] = jnp.zeros(halo_scr.shape, F32)
    if st_ref is not None:
        halo_scr[:, HALO - (width - 1):HALO, :] = st_ref[...]


def _causal_conv(halo_scr, nst_ref, p, wconv_ref, width, sl):
    bb, tt, cw = p.shape
    ext = jnp.concatenate([halo_scr[:, :, sl], p], axis=1)
    halo_scr[:, :, sl] = ext[:, tt:, :]
    nst_ref[:, :, sl] = ext[:, HALO + tt - (width - 1):, :]
    e = ext.reshape(bb * (HALO + tt), cw)
    e1 = pltpu.roll(e, 1, axis=0)
    tap = lambda k: wconv_ref[width - 1 - k:width - k, sl]
    y = None
    for m in range((width + 1) // 2):
        pair = e * tap(2 * m)
        if 2 * m + 1 < width:
            pair = pair + e1 * tap(2 * m + 1)
        if m:
            pair = pltpu.roll(pair, 2 * m, axis=0)
        y = pair if y is None else y + pair
    return y.reshape(bb, HALO + tt, cw)[:, HALO:, :].reshape(bb * tt, cw)


def _residual_norm(x, out, mod_ref, lng_ref, lnb_ref, o_ref, alpha):
    bb, tt, d = o_ref.shape
    gate = mod_ref[:, 2:3, :]
    r = alpha * x.reshape(bb, tt, d) + gate * out.reshape(bb, tt, d)
    o_ref[...] = _layer_norm(r, lng_ref[...], lnb_ref[...])


def _const_spec(shape):
    zeros = (0,) * len(shape)
    return pl.BlockSpec(shape, lambda *_: zeros, pipeline_mode=pl.Buffered(1))


def _conv_layer_kernel(*refs, has_state, width, col_w, alpha):
    if has_state:
        (x_ref, mod_ref, st_ref, win_ref, wconv_ref, wout_ref, lng_ref, lnb_ref,
         o_ref, nst_ref, halo_scr, y_scr) = refs
    else:
        (x_ref, mod_ref, win_ref, wconv_ref, wout_ref, lng_ref, lnb_ref,
         o_ref, nst_ref, halo_scr, y_scr) = refs
        st_ref = None
    bb, tt, d = x_ref.shape
    rows = bb * tt
    dc = wconv_ref.shape[1]

    @pl.when(pl.program_id(1) == 0)
    def _():
        _init_conv_halo(halo_scr, st_ref, width)

    x, u = _modulate(x_ref, mod_ref)
    u = u.reshape(rows, d).astype(BF16)

    for j in range(dc // col_w):
        sl = slice(j * col_w, (j + 1) * col_w)
        pc = _dot(u, win_ref[:, dc + j * col_w:dc + (j + 1) * col_w])
        ph = _dot(u, win_ref[:, 2 * dc + j * col_w:2 * dc + (j + 1) * col_w])
        conv = _causal_conv(halo_scr, nst_ref, (pc * ph).reshape(bb, tt, col_w), wconv_ref, width, sl)
        pb = _dot(u, win_ref[:, sl])
        pz = _dot(u, win_ref[:, 3 * dc + j * col_w:3 * dc + (j + 1) * col_w])
        y_scr[:, sl] = (pb * conv * _silu(pz)).astype(BF16)

    out = _dot(y_scr[...], wout_ref[...])
    _residual_norm(x, out, mod_ref, lng_ref, lnb_ref, o_ref, alpha)


def _conv_layer(x, mod, state, w_in, w_conv, w_out, ln_g, ln_b, *, bb, tt, alpha):
    bsz, seq, d = x.shape
    width, dc = w_conv.shape
    has_state = state is not None
    rows = bb * tt
    seq_spec = lambda n: pl.BlockSpec((bb, n, d if n == 3 else dc), lambda b, t: (b, 0, 0))
    in_specs = [pl.BlockSpec((bb, tt, d), lambda b, t: (b, t, 0)), seq_spec(3)]
    args = [x, mod]
    if has_state:
        in_specs.append(seq_spec(width - 1))
        args.append(state)
    in_specs += [_const_spec(w_in.shape), _const_spec(w_conv.shape), _const_spec(w_out.shape),
                 _const_spec((1, d)), _const_spec((1, d))]
    args += [w_in, w_conv, w_out, ln_g.reshape(1, d), ln_b.reshape(1, d)]
    kern = functools.partial(_conv_layer_kernel, has_state=has_state, width=width,
                             col_w=min(dc, 256), alpha=alpha)
    return pl.pallas_call(
        kern,
        grid=(bsz // bb, seq // tt),
        in_specs=in_specs,
        out_specs=[pl.BlockSpec((bb, tt, d), lambda b, t: (b, t, 0)), seq_spec(width - 1)],
        out_shape=[jax.ShapeDtypeStruct((bsz, seq, d), F32),
                   jax.ShapeDtypeStruct((bsz, width - 1, dc), F32)],
        scratch_shapes=[pltpu.VMEM((bb, HALO, dc), F32), pltpu.VMEM((rows, dc), BF16)],
        compiler_params=pltpu.CompilerParams(
            dimension_semantics=("arbitrary", "arbitrary"), vmem_limit_bytes=VMEM_LIMIT),
        name="conv_layer",
    )(*args)


def _tri_masks(n, seq_len):
    i = lax.broadcasted_iota(jnp.int32, (n, n), 0)
    j = lax.broadcasted_iota(jnp.int32, (n, n), 1)
    same = (i // seq_len) == (j // seq_len)
    causal = same & (i >= j)
    strict = same & (i > j)
    eye = (i == j).astype(F32)
    levels = []
    s = 1
    while s < seq_len:
        levels.append(((i // (2 * s)) == (j // (2 * s))) & ((i % (2 * s)) >= s) & ((j % (2 * s)) < s))
        s *= 2
    return causal, strict, eye, levels


def _gdn_layer_kernel(*refs, has_state, width, n_heads, dk, dv, seq_len, col_w, group, alpha):
    if has_state:
        (x_ref, mod_ref, cst_ref, s0_ref, win_ref, wab_ref, wconv_ref, alog_ref, dtb_ref, nw_ref,
         wout_ref, lng_ref, lnb_ref, o_ref, ncst_ref, ns_ref,
         halo_scr, qkv_scr, z_scr, gt_scr, gx_scr, kb_scr, u_scr, wq_scr, attn_scr, oh_scr) = refs
        s_scr = None
    else:
        (x_ref, mod_ref, win_ref, wab_ref, wconv_ref, alog_ref, dtb_ref, nw_ref,
         wout_ref, lng_ref, lnb_ref, o_ref, ncst_ref, ns_ref,
         halo_scr, qkv_scr, z_scr, gt_scr, gx_scr, kb_scr, u_scr, wq_scr, attn_scr, oh_scr,
         s_scr) = refs
        cst_ref = s0_ref = None
    bb, tt, d = x_ref.shape
    rows = bb * tt
    ch = wconv_ref.shape[1]
    qk = n_heads * dk
    vw = n_heads * dv
    n = min(GDN_ROWS, rows)
    span = min(seq_len, n)
    n_seq = n // span
    n_chunks = rows // n
    heads = range(n_heads)
    q_sl = lambda h: slice(h * dk, (h + 1) * dk)
    k_sl = lambda h: slice(qk + h * dk, qk + (h + 1) * dk)
    v_sl = lambda h: slice(2 * qk + h * dv, 2 * qk + (h + 1) * dv)
    o_sl = lambda h: slice(h * dv, (h + 1) * dv)

    @pl.when(pl.program_id(1) == 0)
    def _():
        _init_conv_halo(halo_scr, cst_ref, width)
        if s_scr is not None:
            s_scr[...] = jnp.zeros(s_scr.shape, F32)

    x, u = _modulate(x_ref, mod_ref)
    u = u.reshape(rows, d).astype(BF16)

    ab = _dot(u, wab_ref[...])
    lane = lax.broadcasted_iota(jnp.int32, (rows, LANES), 1)
    a = ab[:, :LANES] + dtb_ref[...]
    softplus = jnp.maximum(a, 0.0) + jnp.log(1.0 + jnp.exp(-jnp.abs(a)))
    g = jnp.where(lane < n_heads, -jnp.exp(alog_ref[...]) * softplus, 0.0)
    beta = jax.nn.sigmoid(ab[:, LANES:])
    ri = lax.broadcasted_iota(jnp.int32, (rows, rows), 0)
    rj = lax.broadcasted_iota(jnp.int32, (rows, rows), 1)
    tri = (((ri // span) == (rj // span)) & (ri >= rj)).astype(BF16)
    g_hi, g_mid, g_lo = _split3(g)
    g_cum = (jnp.dot(tri, g_hi, preferred_element_type=F32)
             + jnp.dot(tri, g_mid, preferred_element_type=F32)
             + jnp.dot(tri, g_lo, preferred_element_type=F32))
    for c in range(n_chunks):
        gt_scr[c] = g_cum[c * n:(c + 1) * n, :].T
    for h in heads:
        gx_scr[:, o_sl(h)] = jnp.broadcast_to(g_cum[:, h:h + 1], (rows, dk))
        kb_scr[:, o_sl(h)] = jnp.broadcast_to(beta[:, h:h + 1], (rows, dk))

    for j in range(ch // col_w):
        sl = slice(j * col_w, (j + 1) * col_w)
        p = _dot(u, win_ref[:, sl]).reshape(bb, tt, col_w)
        qkv_scr[:, sl] = _silu(_causal_conv(halo_scr, ncst_ref, p, wconv_ref, width, sl))

    z_scr[...] = _silu(_dot(u, win_ref[:, ch:ch + vw]))

    for h in heads:
        t = qkv_scr[:, q_sl(h)]
        qkv_scr[:, q_sl(h)] = t * (lax.rsqrt(jnp.sum(t * t, axis=-1, keepdims=True) + NORM_EPS)
                                   * dk ** -0.5)
        t = qkv_scr[:, k_sl(h)]
        t = t * lax.rsqrt(jnp.sum(t * t, axis=-1, keepdims=True) + NORM_EPS)
        qkv_scr[:, k_sl(h)] = t
        beta_x = kb_scr[:, o_sl(h)]
        qkv_scr[:, v_sl(h)] = qkv_scr[:, v_sl(h)] * beta_x
        kb_scr[:, o_sl(h)] = t * beta_x

    causal, strict, eye, levels = _tri_masks(n, span)
    row_seq = lax.broadcasted_iota(jnp.int32, (n, 1), 0) // span

    def local_body(gi, carry):
        items = []
        for ci in range(group):
            c = gi * group + ci
            rs = pl.ds(pl.multiple_of(c * n, n), n)
            items += [(c, rs, h) for h in heads]
        decay, a_mat, t_inv = [], [], []
        for c, rs, h in items:
            diff = gx_scr[rs, h * dv:h * dv + n] - gt_scr[c, h:h + 1, :]
            decay.append(jnp.where(causal, jnp.exp(jnp.where(causal, diff, 0.0)), 0.0))
        for (c, rs, h), dec in zip(items, decay):
            k = qkv_scr[rs, k_sl(h)]
            a_mat.append(jnp.where(strict, _dot_nt(kb_scr[rs, o_sl(h)], k) * dec, 0.0))
        for (c, rs, h), dec in zip(items, decay):
            attn_scr[rs, h * dv:h * dv + n] = (
                _dot_nt(qkv_scr[rs, q_sl(h)], qkv_scr[rs, k_sl(h)]) * dec)
        t_inv = [eye - jnp.where(levels[0], a, 0.0) for a in a_mat]
        for li, m in enumerate(levels[1:]):
            s = 2 << li
            if s % SUBLANES:
                y = [_dot(jnp.where(m, a, 0.0), t) for a, t in zip(a_mat, t_inv)]
                xs = [_dot(t, yy) for t, yy in zip(t_inv, y)]
                t_inv = [t - xx for t, xx in zip(t_inv, xs)]
                continue
            lo = [(b * 2 * s + s, b * 2 * s + 2 * s) for b in range(n // (2 * s))]
            take = lambda x: jnp.concatenate([x[r0:r1] for r0, r1 in lo], axis=0)
            def put(xc, rest):
                parts = []
                for b, (r0, r1) in enumerate(lo):
                    parts += [rest(r0 - s, r0), xc[b * s:(b + 1) * s]]
                return jnp.concatenate(parts, axis=0)
            zeros = jnp.zeros((s, n), F32)
            y = [put(_dot(take(jnp.where(m, a, 0.0)), t), lambda r0, r1: zeros)
                 for a, t in zip(a_mat, t_inv)]
            xs = [_dot(take(t), yy) for t, yy in zip(t_inv, y)]
            t_inv = [put(take(t) - xx, lambda r0, r1, t=t: t[r0:r1]) for t, xx in zip(t_inv, xs)]
        for (c, rs, h), t in zip(items, t_inv):
            e_x = jnp.exp(gx_scr[rs, o_sl(h)])
            rhs = jnp.concatenate([qkv_scr[rs, v_sl(h)], kb_scr[rs, o_sl(h)] * e_x], axis=1)
            sol = _dot(t, rhs)
            u_scr[rs, o_sl(h)] = sol[:, :dv]
            wq_scr[c, 0:n, o_sl(h)] = sol[:, dv:]
            wq_scr[c, n:2 * n, o_sl(h)] = qkv_scr[rs, q_sl(h)] * e_x
        return carry

    lax.fori_loop(0, n_chunks // group, local_body, 0)

    chunks_per_seq = tt // n if not has_state else n_chunks

    def state_body(c, carry):
        if has_state:
            lanes = [(0, s) for s in range(n_seq)]
            rows_of = lambda q: pl.ds(pl.multiple_of(c * n, n), n)
            chunk_of = lambda q: c
        else:
            lanes = [(q, 0) for q in range(bb)]
            rows_of = lambda q: pl.ds(pl.multiple_of(q * tt + c * n, n), n)
            chunk_of = lambda q: q * chunks_per_seq + c
        groups = sorted({q for q, _ in lanes})
        masks = [None if n_seq == 1 else (row_seq == s) for s in range(n_seq)]
        if has_state:
            s_prev = {(h, q, s): s0_ref[c * n_seq + s, h] for h in heads for q, s in lanes}
        else:
            s_prev = {(h, q, s): s_scr[q, h] for h in heads for q, s in lanes}
        def own_rows(q, s, h):
            if n_seq == 1:
                return wq_scr[chunk_of(q), :, o_sl(h)]
            return jnp.concatenate(
                [wq_scr[chunk_of(q), s * span:(s + 1) * span, o_sl(h)],
                 wq_scr[chunk_of(q), n + s * span:n + (s + 1) * span, o_sl(h)]], axis=0)
        ws = {(h, q, s): _dot(own_rows(q, s, h), s_prev[(h, q, s)])
              for h in heads for q, s in lanes}
        v_new, v_dec, o_part, g_last = {}, {}, {}, {}
        for h in heads:
            for q in groups:
                w_s = jnp.concatenate([ws[(h, q, s)][:span] for s in range(n_seq)], axis=0)
                v_new[h, q] = u_scr[rows_of(q), o_sl(h)] - w_s
                o_part[h, q] = jnp.concatenate(
                    [ws[(h, q, s)][span:] for s in range(n_seq)], axis=0)
                gx = gx_scr[rows_of(q), o_sl(h)]
                for s in range(n_seq):
                    g_last[h, q, s] = gx[(s + 1) * span - 1:(s + 1) * span, :]
                g_end = jnp.concatenate(
                    [jnp.broadcast_to(g_last[h, q, s], (span, dv)) for s in range(n_seq)], axis=0)
                v_dec[h, q] = v_new[h, q] * jnp.exp(g_end - gx)
        for h in heads:
            for q in groups:
                oh_scr[rows_of(q), o_sl(h)] = o_part[h, q] + _dot(
                    attn_scr[rows_of(q), h * dv:h * dv + n], v_new[h, q])
        for h in heads:
            for q in groups:
                vs = v_dec[h, q] if n_seq == 1 else jnp.concatenate(
                    [jnp.where(masks[s], v_dec[h, q], 0.0) for s in range(n_seq)], axis=1)
                ktv = _dot_tn(qkv_scr[rows_of(q), k_sl(h)], vs)
                for s in range(n_seq):
                    s_new = (s_prev[(h, q, s)] * jnp.exp(g_last[h, q, s])
                             + ktv[:, s * dv:(s + 1) * dv])
                    if has_state:
                        ns_ref[c * n_seq + s, h] = s_new
                    else:
                        s_scr[q, h] = s_new
        return carry

    lax.fori_loop(0, chunks_per_seq, state_body, 0)
    if not has_state:
        ns_ref[...] = s_scr[...]

    for h in heads:
        o = oh_scr[:, o_sl(h)]
        o = o * lax.rsqrt(jnp.mean(o * o, axis=-1, keepdims=True) + NORM_EPS) * nw_ref[...]
        oh_scr[:, o_sl(h)] = o * z_scr[:, o_sl(h)]
    out = _dot(oh_scr[...], wout_ref[...])
    _residual_norm(x, out, mod_ref, lng_ref, lnb_ref, o_ref, alpha)


def _gdn_layer(x, mod, conv_state, ssm_state, w_in, w_ab, w_conv, a_log, dt_bias, norm_w, w_out,
               ln_g, ln_b, *, bb, tt, alpha):
    bsz, seq, d = x.shape
    width, ch = w_conv.shape
    n_heads = a_log.shape[0]
    vw = w_out.shape[0]
    dv = vw // n_heads
    dk = (ch - vw) // 2 // n_heads
    qk = n_heads * dk
    has_state = conv_state is not None
    rows = bb * tt
    seq_len = min(seq, GDN_ROWS)
    n = min(GDN_ROWS, rows)
    n_chunks = rows // n
    assert dk == dv == LANES, "one head per lane tile"
    assert rows % n == 0 and n % seq_len == 0
    assert has_state == (tt == seq), "a carried state needs whole sequences per block"
    group = max(g for g in (8, 4, 2, 1) if n_chunks % g == 0)

    pad = lambda vec: jnp.zeros((1, LANES), F32).at[0, :n_heads].set(vec)
    in_specs = [pl.BlockSpec((bb, tt, d), lambda b, t: (b, t, 0)),
                pl.BlockSpec((bb, 3, d), lambda b, t: (b, 0, 0))]
    args = [x, mod]
    if has_state:
        in_specs += [pl.BlockSpec((bb, width - 1, ch), lambda b, t: (b, 0, 0)),
                     pl.BlockSpec((bb, n_heads, dk, dv), lambda b, t: (b, 0, 0, 0))]
        args += [conv_state, ssm_state]
    in_specs += [_const_spec(w_in.shape), _const_spec(w_ab.shape), _const_spec(w_conv.shape),
                 _const_spec((1, LANES)), _const_spec((1, LANES)), _const_spec((1, dv)),
                 _const_spec(w_out.shape), _const_spec((1, d)), _const_spec((1, d))]
    args += [w_in, w_ab, w_conv, pad(a_log), pad(dt_bias), norm_w.reshape(1, dv), w_out,
             ln_g.reshape(1, d), ln_b.reshape(1, d)]
    scratch = [pltpu.VMEM((bb, HALO, ch), F32),
               pltpu.VMEM((rows, ch), F32),
               pltpu.VMEM((rows, vw), F32),
               pltpu.VMEM((n_chunks, LANES, n), F32),
               pltpu.VMEM((rows, vw), F32),
               pltpu.VMEM((rows, qk), F32),
               pltpu.VMEM((rows, vw), F32),
               pltpu.VMEM((n_chunks, 2 * n, qk), F32),
               pltpu.VMEM((rows, vw), F32),
               pltpu.VMEM((rows, vw), F32)]
    if not has_state:
        scratch.append(pltpu.VMEM((bb, n_heads, dk, dv), F32))
    kern = functools.partial(_gdn_layer_kernel, has_state=has_state, width=width, n_heads=n_heads,
                             dk=dk, dv=dv, seq_len=seq_len, col_w=min(ch, 512), group=group,
                             alpha=alpha)
    return pl.pallas_call(
        kern,
        grid=(bsz // bb, seq // tt),
        in_specs=in_specs,
        out_specs=[pl.BlockSpec((bb, tt, d), lambda b, t: (b, t, 0)),
                   pl.BlockSpec((bb, width - 1, ch), lambda b, t: (b, 0, 0)),
                   pl.BlockSpec((bb, n_heads, dk, dv), lambda b, t: (b, 0, 0, 0))],
        out_shape=[jax.ShapeDtypeStruct((bsz, seq, d), F32),
                   jax.ShapeDtypeStruct((bsz, width - 1, ch), F32),
                   jax.ShapeDtypeStruct((bsz, n_heads, dk, dv), F32)],
        scratch_shapes=scratch,
        compiler_params=pltpu.CompilerParams(
            dimension_semantics=("arbitrary", "arbitrary"), vmem_limit_bytes=VMEM_LIMIT),
        name="gdn_layer",
    )(*args)


def _block_rows(bsz, seq, target, long_seqs=1):
    if seq * long_seqs >= target:
        return long_seqs, target // long_seqs
    return max(1, min(bsz, target // seq)), seq


def _trunk(x, mod, conv_a, conv_b, ssm_b, ln_g, ln_b, wa, wb, alpha):
    bsz, seq, _ = x.shape
    new_a, new_cb, new_s = [], [], []
    depth = mod.shape[0]
    for l in range(depth):
        i = l // 2
        if l % 2 == 0:
            bb, tt = _block_rows(bsz, seq, 512)
            x, nb = _conv_layer(x, mod[l], None if conv_a is None else conv_a[i],
                                wa["in"][i], wa["conv"][i], wa["out"][i], ln_g[l], ln_b[l],
                                bb=bb, tt=tt, alpha=alpha)
            new_a.append(nb)
        else:
            bb, tt = _block_rows(bsz, seq, *((512, 2) if conv_b is None else (GDN_ROWS,)))
            x, nb, s = _gdn_layer(x, mod[l], None if conv_b is None else conv_b[i],
                                  None if ssm_b is None else ssm_b[i],
                                  wb["in"][i], wb["ab"][i], wb["conv"][i], wb["a_log"][i],
                                  wb["dt_bias"][i], wb["norm"][i], wb["out"][i], ln_g[l], ln_b[l],
                                  bb=bb, tt=tt, alpha=alpha)
            new_cb.append(nb)
            new_s.append(s)
    return x, jnp.stack(new_a), jnp.stack(new_cb), jnp.stack(new_s)


def kernel(x_prompt, x_sample, state_conv_a, state_conv_b, state_ssm_b, c_prompt, c_sample, w_mod, b_mod, ln_g, ln_b, wa_in, wa_conv, wa_out, wb_in, wb_conv, wb_a_log, wb_dt_bias, wb_norm, wb_out):
    depth, d, _ = w_mod.shape
    alpha = (2 * depth) ** 0.25
    bp = x_prompt.shape[0]
    n_heads = wb_a_log.shape[1]
    ch = wb_conv.shape[2]
    vw = wb_out.shape[1]

    mod = _modulation(jnp.concatenate([c_prompt, c_sample], axis=0), w_mod, b_mod)
    mod = mod.reshape(depth, -1, 3, d)
    mod_p, mod_s = mod[:, :bp], mod[:, bp:]

    n_gdn = wb_in.shape[0]
    w_ab = jnp.zeros((n_gdn, d, 2 * LANES), F32)
    w_ab = w_ab.at[:, :, :n_heads].set(wb_in[:, :, ch + vw + n_heads:])
    w_ab = w_ab.at[:, :, LANES:LANES + n_heads].set(wb_in[:, :, ch + vw:ch + vw + n_heads])
    wa = {"in": wa_in.astype(BF16), "conv": wa_conv, "out": wa_out.astype(BF16)}
    wb = {"in": wb_in[:, :, :ch + vw].astype(BF16), "ab": w_ab.astype(BF16), "conv": wb_conv,
          "a_log": wb_a_log, "dt_bias": wb_dt_bias, "norm": wb_norm, "out": wb_out.astype(BF16)}

    y_p, ca_p, cb_p, s_p = _trunk(x_prompt, mod_p, None, None, None, ln_g, ln_b, wa, wb, alpha)
    y_s, ca_s, cb_s, s_s = _trunk(x_sample, mod_s, state_conv_a, state_conv_b, state_ssm_b,
                                  ln_g, ln_b, wa, wb, alpha)
    return (y_p, y_s, ca_p, cb_p, s_p, ca_s, cb_s, s_s)
```

```python
import functools

import jax
import jax.numpy as jnp
from jax import lax
from jax.experimental import pallas as pl
from jax.experimental.pallas import tpu as pltpu

F32 = jnp.float32
BF16 = jnp.bfloat16

LN_EPS = 1e-5
NORM_EPS = 1e-6
SUBLANES = 8
LANES = 128
HALO = SUBLANES
GDN_ROWS = 64
VMEM_LIMIT = 56 * 1024 * 1024


def _dot(a, b):
    return jnp.dot(a.astype(BF16), b.astype(BF16), preferred_element_type=F32)


def _dot_nt(a, b):
    return lax.dot_general(a.astype(BF16), b.astype(BF16), (((1,), (1,)), ((), ())),
                           preferred_element_type=F32)


def _dot_tn(a, b):
    return lax.dot_general(a.astype(BF16), b.astype(BF16), (((0,), (0,)), ((), ())),
                           preferred_element_type=F32)


def _silu(x):
    return x * jax.nn.sigmoid(x)


def _split3(x):
    hi = x.astype(BF16)
    r = x - hi.astype(F32)
    mid = r.astype(BF16)
    lo = (r - mid.astype(F32)).astype(BF16)
    return hi, mid, lo


def _layer_norm(r, g, b):
    mu = jnp.mean(r, axis=-1, keepdims=True)
    d = r - mu
    var = jnp.mean(d * d, axis=-1, keepdims=True)
    return d * lax.rsqrt(var + LN_EPS) * g + b


def _mod_kernel(c_ref, w_ref, b_ref, o_ref):
    cs = _silu(c_ref[...])
    o_ref[0] = _dot(cs, w_ref[0]) + b_ref[0]


def _modulation(c_all, w_mod, b_mod):
    depth, d, n3 = w_mod.shape
    rows = c_all.shape[0]
    nb = d
    return pl.pallas_call(
        _mod_kernel,
        grid=(depth, n3 // nb),
        in_specs=[
            pl.BlockSpec((rows, d), lambda l, j: (0, 0)),
            pl.BlockSpec((1, d, nb), lambda l, j: (l, 0, j)),
            pl.BlockSpec((1, 1, nb), lambda l, j: (l, 0, j)),
        ],
        out_specs=pl.BlockSpec((1, rows, nb), lambda l, j: (l, 0, j)),
        out_shape=jax.ShapeDtypeStruct((depth, rows, n3), F32),
        compiler_params=pltpu.CompilerParams(
            dimension_semantics=("arbitrary", "arbitrary"), vmem_limit_bytes=VMEM_LIMIT),
        name="modulation",
    )(c_all, w_mod, b_mod.reshape(depth, 1, n3))


def _modulate(x_ref, mod_ref):
    x = x_ref[...]
    shift = mod_ref[:, 0:1, :]
    scale = mod_ref[:, 1:2, :]
    return x, x * (1.0 + scale) + shift


def _init_conv_halo(halo_scr, st_ref, width):
    halo_scr[...] = jnp.zeros(halo_scr.shape, F32)
    if st_ref is not None:
        halo_scr[:, HALO - (width - 1):HALO, :] = st_ref[...]


def _causal_conv(halo_scr, nst_ref, p, wconv_ref, width, sl):
    bb, tt, cw = p.shape
    ext = jnp.concatenate([halo_scr[:, :, sl], p], axis=1)
    halo_scr[:, :, sl] = ext[:, tt:, :]
    nst_ref[:, :, sl] = ext[:, HALO + tt - (width - 1):, :]
    e = ext.reshape(bb * (HALO + tt), cw)
    e1 = pltpu.roll(e, 1, axis=0)
    tap = lambda k: wconv_ref[width - 1 - k:width - k, sl]
    y = None
    for m in range((width + 1) // 2):
        pair = e * tap(2 * m)
        if 2 * m + 1 < width:
            pair = pair + e1 * tap(2 * m + 1)
        if m:
            pair = pltpu.roll(pair, 2 * m, axis=0)
        y = pair if y is None else y + pair
    return y.reshape(bb, HALO + tt, cw)[:, HALO:, :].reshape(bb * tt, cw)


def _residual_norm(x, out, mod_ref, lng_ref, lnb_ref, o_ref, alpha):
    bb, tt, d = o_ref.shape
    gate = mod_ref[:, 2:3, :]
    r = alpha * x.reshape(bb, tt, d) + gate * out.reshape(bb, tt, d)
    o_ref[...] = _layer_norm(r, lng_ref[...], lnb_ref[...])


def _const_spec(shape):
    zeros = (0,) * len(shape)
    return pl.BlockSpec(shape, lambda *_: zeros, pipeline_mode=pl.Buffered(1))


def _conv_layer_kernel(*refs, has_state, width, col_w, alpha):
    if has_state:
        (x_ref, mod_ref, st_ref, win_ref, wconv_ref, wout_ref, lng_ref, lnb_ref,
         o_ref, nst_ref, halo_scr, y_scr) = refs
    else:
        (x_ref, mod_ref, win_ref, wconv_ref, wout_ref, lng_ref, lnb_ref,
         o_ref, nst_ref, halo_scr, y_scr) = refs
        st_ref = None
    bb, tt, d = x_ref.shape
    rows = bb * tt
    dc = wconv_ref.shape[1]

    @pl.when(pl.program_id(1) == 0)
    def _():
        _init_conv_halo(halo_scr, st_ref, width)

    x, u = _modulate(x_ref, mod_ref)
    u = u.reshape(rows, d).astype(BF16)

    for j in range(dc // col_w):
        sl = slice(j * col_w, (j + 1) * col_w)
        pc = _dot(u, win_ref[:, dc + j * col_w:dc + (j + 1) * col_w])
        ph = _dot(u, win_ref[:, 2 * dc + j * col_w:2 * dc + (j + 1) * col_w])
        conv = _causal_conv(halo_scr, nst_ref, (pc * ph).reshape(bb, tt, col_w), wconv_ref, width, sl)
        pb = _dot(u, win_ref[:, sl])
        pz = _dot(u, win_ref[:, 3 * dc + j * col_w:3 * dc + (j + 1) * col_w])
        y_scr[:, sl] = (pb * conv * _silu(pz)).astype(BF16)

    out = _dot(y_scr[...], wout_ref[...])
    _residual_norm(x, out, mod_ref, lng_ref, lnb_ref, o_ref, alpha)


def _conv_layer(x, mod, state, w_in, w_conv, w_out, ln_g, ln_b, *, bb, tt, alpha):
    bsz, seq, d = x.shape
    width, dc = w_conv.shape
    has_state = state is not None
    rows = bb * tt
    seq_spec = lambda n: pl.BlockSpec((bb, n, d if n == 3 else dc), lambda b, t: (b, 0, 0))
    in_specs = [pl.BlockSpec((bb, tt, d), lambda b, t: (b, t, 0)), seq_spec(3)]
    args = [x, mod]
    if has_state:
        in_specs.append(seq_spec(width - 1))
        args.append(state)
    in_specs += [_const_spec(w_in.shape), _const_spec(w_conv.shape), _const_spec(w_out.shape),
                 _const_spec((1, d)), _const_spec((1, d))]
    args += [w_in, w_conv, w_out, ln_g.reshape(1, d), ln_b.reshape(1, d)]
    kern = functools.partial(_conv_layer_kernel, has_state=has_state, width=width,
                             col_w=min(dc, 256), alpha=alpha)
    return pl.pallas_call(
        kern,
        grid=(bsz // bb, seq // tt),
        in_specs=in_specs,
        out_specs=[pl.BlockSpec((bb, tt, d), lambda b, t: (b, t, 0)), seq_spec(width - 1)],
        out_shape=[jax.ShapeDtypeStruct((bsz, seq, d), F32),
                   jax.ShapeDtypeStruct((bsz, width - 1, dc), F32)],
        scratch_shapes=[pltpu.VMEM((bb, HALO, dc), F32), pltpu.VMEM((rows, dc), BF16)],
        compiler_params=pltpu.CompilerParams(
            dimension_semantics=("arbitrary", "arbitrary"), vmem_limit_bytes=VMEM_LIMIT),
        name="conv_layer",
    )(*args)


def _tri_masks(n, seq_len):
    i = lax.broadcasted_iota(jnp.int32, (n, n), 0)
    j = lax.broadcasted_iota(jnp.int32, (n, n), 1)
    same = (i // seq_len) == (j // seq_len)
    causal = same & (i >= j)
    strict = same & (i > j)
    eye = (i == j).astype(F32)
    levels = []
    s = 1
    while s < seq_len:
        levels.append(((i // (2 * s)) == (j // (2 * s))) & ((i % (2 * s)) >= s) & ((j % (2 * s)) < s))
        s *= 2
    return causal, strict, eye, levels


def _gdn_layer_kernel(*refs, has_state, width, n_heads, dk, dv, seq_len, col_w, group, alpha):
    if has_state:
        (x_ref, mod_ref, cst_ref, s0_ref, win_ref, wab_ref, wconv_ref, alog_ref, dtb_ref, nw_ref,
         wout_ref, lng_ref, lnb_ref, o_ref, ncst_ref, ns_ref,
         halo_scr, qkv_scr, z_scr, gt_scr, gx_scr, kb_scr, u_scr, wq_scr, attn_scr, oh_scr) = refs
        s_scr = None
    else:
        (x_ref, mod_ref, win_ref, wab_ref, wconv_ref, alog_ref, dtb_ref, nw_ref,
         wout_ref, lng_ref, lnb_ref, o_ref, ncst_ref, ns_ref,
         halo_scr, qkv_scr, z_scr, gt_scr, gx_scr, kb_scr, u_scr, wq_scr, attn_scr, oh_scr,
         s_scr) = refs
        cst_ref = s0_ref = None
    bb, tt, d = x_ref.shape
    rows = bb * tt
    ch = wconv_ref.shape[1]
    qk = n_heads * dk
    vw = n_heads * dv
    n = min(GDN_ROWS, rows)
    span = min(seq_len, n)
    n_seq = n // span
    n_chunks = rows // n
    heads = range(n_heads)
    q_sl = lambda h: slice(h * dk, (h + 1) * dk)
    k_sl = lambda h: slice(qk + h * dk, qk + (h + 1) * dk)
    v_sl = lambda h: slice(2 * qk + h * dv, 2 * qk + (h + 1) * dv)
    o_sl = lambda h: slice(h * dv, (h + 1) * dv)

    @pl.when(pl.program_id(1) == 0)
    def _():
        _init_conv_halo(halo_scr, cst_ref, width)
        if s_scr is not None:
            s_scr[...] = jnp.zeros(s_scr.shape, F32)

    x, u = _modulate(x_ref, mod_ref)
    u = u.reshape(rows, d).astype(BF16)

    ab = _dot(u, wab_ref[...])
    lane = lax.broadcasted_iota(jnp.int32, (rows, LANES), 1)
    a = ab[:, :LANES] + dtb_ref[...]
    softplus = jnp.maximum(a, 0.0) + jnp.log(1.0 + jnp.exp(-jnp.abs(a)))
    g = jnp.where(lane < n_heads, -jnp.exp(alog_ref[...]) * softplus, 0.0)
    beta = jax.nn.sigmoid(ab[:, LANES:])
    ri = lax.broadcasted_iota(jnp.int32, (rows, rows), 0)
    rj = lax.broadcasted_iota(jnp.int32, (rows, rows), 1)
    tri = (((ri // span) == (rj // span)) & (ri >= rj)).astype(BF16)
    g_hi, g_mid, g_lo = _split3(g)
    g_cum = (jnp.dot(tri, g_hi, preferred_element_type=F32)
             + jnp.dot(tri, g_mid, preferred_element_type=F32)
             + jnp.dot(tri, g_lo, preferred_element_type=F32))
    for c in range(n_chunks):
        gt_scr[c] = g_cum[c * n:(c + 1) * n, :].T
    for h in heads:
        gx_scr[:, o_sl(h)] = jnp.broadcast_to(g_cum[:, h:h + 1], (rows, dk))
        kb_scr[:, o_sl(h)] = jnp.broadcast_to(beta[:, h:h + 1], (rows, dk))

    for j in range(ch // col_w):
        sl = slice(j * col_w, (j + 1) * col_w)
        p = _dot(u, win_ref[:, sl]).reshape(bb, tt, col_w)
        qkv_scr[:, sl] = _silu(_causal_conv(halo_scr, ncst_ref, p, wconv_ref, width, sl))

    z_scr[...] = _silu(_dot(u, win_ref[:, ch:ch + vw]))

    for h in heads:
        t = qkv_scr[:, q_sl(h)]
        qkv_scr[:, q_sl(h)] = t * (lax.rsqrt(jnp.sum(t * t, axis=-1, keepdims=True) + NORM_EPS)
                                   * dk ** -0.5)
        t = qkv_scr[:, k_sl(h)]
        t = t * lax.rsqrt(jnp.sum(t * t, axis=-1, keepdims=True) + NORM_EPS)
        qkv_scr[:, k_sl(h)] = t
        beta_x = kb_scr[:, o_sl(h)]
        qkv_scr[:, v_sl(h)] = qkv_scr[:, v_sl(h)] * beta_x
        kb_scr[:, o_sl(h)] = t * beta_x

    causal, strict, eye, levels = _tri_masks(n, span)
    row_seq = lax.broadcasted_iota(jnp.int32, (n, 1), 0) // span

    def local_body(gi, carry):
        items = []
        for ci in range(group):
            c = gi * group + ci
            rs = pl.ds(pl.multiple_of(c * n, n), n)
            items += [(c, rs, h) for h in heads]
        decay, a_mat, t_inv = [], [], []
        for c, rs, h in items:
            diff = gx_scr[rs, h * dv:h * dv + n] - gt_scr[c, h:h + 1, :]
            decay.append(jnp.where(causal, jnp.exp(jnp.where(causal, diff, 0.0)), 0.0))
        for (c, rs, h), dec in zip(items, decay):
            k = qkv_scr[rs, k_sl(h)]
            a_mat.append(jnp.where(strict, _dot_nt(kb_scr[rs, o_sl(h)], k) * dec, 0.0))
        for (c, rs, h), dec in zip(items, decay):
            attn_scr[rs, h * dv:h * dv + n] = (
                _dot_nt(qkv_scr[rs, q_sl(h)], qkv_scr[rs, k_sl(h)]) * dec)
        t_inv = [eye - jnp.where(levels[0], a, 0.0) for a in a_mat]
        for li, m in enumerate(levels[1:]):
            s = 2 << li
            if s % SUBLANES:
                y = [_dot(jnp.where(m, a, 0.0), t) for a, t in zip(a_mat, t_inv)]
                xs = [_dot(t, yy) for t, yy in zip(t_inv, y)]
                t_inv = [t - xx for t, xx in zip(t_inv, xs)]
                continue
            lo = [(b * 2 * s + s, b * 2 * s + 2 * s) for b in range(n // (2 * s))]
            take = lambda x: jnp.concatenate([x[r0:r1] for r0, r1 in lo], axis=0)
            def put(xc, rest):
                parts = []
                for b, (r0, r1) in enumerate(lo):
                    parts += [rest(r0 - s, r0), xc[b * s:(b + 1) * s]]
                return jnp.concatenate(parts, axis=0)
            zeros = jnp.zeros((s, n), F32)
            y = [put(_dot(take(jnp.where(m, a, 0.0)), t), lambda r0, r1: zeros)
                 for a, t in zip(a_mat, t_inv)]
            xs = [_dot(take(t), yy) for t, yy in zip(t_inv, y)]
            t_inv = [put(take(t) - xx, lambda r0, r1, t=t: t[r0:r1]) for t, xx in zip(t_inv, xs)]
        for (c, rs, h), t in zip(items, t_inv):
            e_x = jnp.exp(gx_scr[rs, o_sl(h)])
            rhs = jnp.concatenate([qkv_scr[rs, v_sl(h)], kb_scr[rs, o_sl(h)] * e_x], axis=1)
            sol = _dot(t, rhs)
            u_scr[rs, o_sl(h)] = sol[:, :dv]
            wq_scr[c, 0:n, o_sl(h)] = sol[:, dv:]
            wq_scr[c, n:2 * n, o_sl(h)] = qkv_scr[rs, q_sl(h)] * e_x
        return carry

    lax.fori_loop(0, n_chunks // group, local_body, 0)

    chunks_per_seq = tt // n if not has_state else n_chunks

    def state_body(c, carry):
        if has_state:
            lanes = [(0, s) for s in range(n_seq)]
            rows_of = lambda q: slice(c * n, (c + 1) * n)
            chunk_of = lambda q: c
        else:
            lanes = [(q, 0) for q in range(bb)]
            rows_of = lambda q: slice(q * tt + c * n, q * tt + (c + 1) * n)
            chunk_of = lambda q: q * chunks_per_seq + c
        groups = sorted({q for q, _ in lanes})
        masks = [None if n_seq == 1 else (row_seq == s) for s in range(n_seq)]
        if has_state:
            s_prev = {(h, q, s): s0_ref[c * n_seq + s, h] for h in heads for q, s in lanes}
        else:
            s_prev = {(h, q, s): s_scr[q, h] for h in heads for q, s in lanes}
        def own_rows(q, s, h):
            if n_seq == 1:
                return wq_scr[chunk_of(q), :, o_sl(h)]
            return jnp.concatenate(
                [wq_scr[chunk_of(q), s * span:(s + 1) * span, o_sl(h)],
                 wq_scr[chunk_of(q), n + s * span:n + (s + 1) * span, o_sl(h)]], axis=0)
        ws = {(h, q, s): _dot(own_rows(q, s, h), s_prev[(h, q, s)])
              for h in heads for q, s in lanes}
        v_new, v_dec, o_part, g_last = {}, {}, {}, {}
        for h in heads:
            for q in groups:
                w_s = jnp.concatenate([ws[(h, q, s)][:span] for s in range(n_seq)], axis=0)
                v_new[h, q] = u_scr[rows_of(q), o_sl(h)] - w_s
                o_part[h, q] = jnp.concatenate(
                    [ws[(h, q, s)][span:] for s in range(n_seq)], axis=0)
                gx = gx_scr[rows_of(q), o_sl(h)]
                for s in range(n_seq):
                    g_last[h, q, s] = gx[(s + 1) * span - 1:(s + 1) * span, :]
                g_end = jnp.concatenate(
                    [jnp.broadcast_to(g_last[h, q, s], (span, dv)) for s in range(n_seq)], axis=0)
                v_dec[h, q] = v_new[h, q] * jnp.exp(g_end - gx)
        for h in heads:
            for q in groups:
                oh_scr[rows_of(q), o_sl(h)] = o_part[h, q] + _dot(
                    attn_scr[rows_of(q), h * dv:h * dv + n], v_new[h, q])
        for h in heads:
            for q in groups:
                vs = v_dec[h, q] if n_seq == 1 else jnp.concatenate(
                    [jnp.where(masks[s], v_dec[h, q], 0.0) for s in range(n_seq)], axis=1)
                ktv = _dot_tn(qkv_scr[rows_of(q), k_sl(h)], vs)
                for s in range(n_seq):
                    s_new = (s_prev[(h, q, s)] * jnp.exp(g_last[h, q, s])
                             + ktv[:, s * dv:(s + 1) * dv])
                    if has_state:
                        ns_ref[c * n_seq + s, h] = s_new
                    else:
                        s_scr[q, h] = s_new
        return carry

    for c in range(chunks_per_seq):
        state_body(c, 0)
    if not has_state:
        ns_ref[...] = s_scr[...]

    for h in heads:
        o = oh_scr[:, o_sl(h)]
        o = o * lax.rsqrt(jnp.mean(o * o, axis=-1, keepdims=True) + NORM_EPS) * nw_ref[...]
        oh_scr[:, o_sl(h)] = o * z_scr[:, o_sl(h)]
    out = _dot(oh_scr[...], wout_ref[...])
    _residual_norm(x, out, mod_ref, lng_ref, lnb_ref, o_ref, alpha)


def _gdn_layer(x, mod, conv_state, ssm_state, w_in, w_ab, w_conv, a_log, dt_bias, norm_w, w_out,
               ln_g, ln_b, *, bb, tt, alpha):
    bsz, seq, d = x.shape
    width, ch = w_conv.shape
    n_heads = a_log.shape[0]
    vw = w_out.shape[0]
    dv = vw // n_heads
    dk = (ch - vw) // 2 // n_heads
    qk = n_heads * dk
    has_state = conv_state is not None
    rows = bb * tt
    seq_len = min(seq, GDN_ROWS)
    n = min(GDN_ROWS, rows)
    n_chunks = rows // n
    assert dk == dv == LANES, "one head per lane tile"
    assert rows % n == 0 and n % seq_len == 0
    assert has_state == (tt == seq), "a carried state needs whole sequences per block"
    group = max(g for g in (8, 4, 2, 1) if n_chunks % g == 0)

    pad = lambda vec: jnp.zeros((1, LANES), F32).at[0, :n_heads].set(vec)
    in_specs = [pl.BlockSpec((bb, tt, d), lambda b, t: (b, t, 0)),
                pl.BlockSpec((bb, 3, d), lambda b, t: (b, 0, 0))]
    args = [x, mod]
    if has_state:
        in_specs += [pl.BlockSpec((bb, width - 1, ch), lambda b, t: (b, 0, 0)),
                     pl.BlockSpec((bb, n_heads, dk, dv), lambda b, t: (b, 0, 0, 0))]
        args += [conv_state, ssm_state]
    in_specs += [_const_spec(w_in.shape), _const_spec(w_ab.shape), _const_spec(w_conv.shape),
                 _const_spec((1, LANES)), _const_spec((1, LANES)), _const_spec((1, dv)),
                 _const_spec(w_out.shape), _const_spec((1, d)), _const_spec((1, d))]
    args += [w_in, w_ab, w_conv, pad(a_log), pad(dt_bias), norm_w.reshape(1, dv), w_out,
             ln_g.reshape(1, d), ln_b.reshape(1, d)]
    scratch = [pltpu.VMEM((bb, HALO, ch), F32),
               pltpu.VMEM((rows, ch), F32),
               pltpu.VMEM((rows, vw), F32),
               pltpu.VMEM((n_chunks, LANES, n), F32),
               pltpu.VMEM((rows, vw), F32),
               pltpu.VMEM((rows, qk), F32),
               pltpu.VMEM((rows, vw), F32),
               pltpu.VMEM((n_chunks, 2 * n, qk), F32),
               pltpu.VMEM((rows, vw), F32),
               pltpu.VMEM((rows, vw), F32)]
    if not has_state:
        scratch.append(pltpu.VMEM((bb, n_heads, dk, dv), F32))
    kern = functools.partial(_gdn_layer_kernel, has_state=has_state, width=width, n_heads=n_heads,
                             dk=dk, dv=dv, seq_len=seq_len, col_w=min(ch, 512), group=group,
                             alpha=alpha)
    return pl.pallas_call(
        kern,
        grid=(bsz // bb, seq // tt),
        in_specs=in_specs,
        out_specs=[pl.BlockSpec((bb, tt, d), lambda b, t: (b, t, 0)),
                   pl.BlockSpec((bb, width - 1, ch), lambda b, t: (b, 0, 0)),
                   pl.BlockSpec((bb, n_heads, dk, dv), lambda b, t: (b, 0, 0, 0))],
        out_shape=[jax.ShapeDtypeStruct((bsz, seq, d), F32),
                   jax.ShapeDtypeStruct((bsz, width - 1, ch), F32),
                   jax.ShapeDtypeStruct((bsz, n_heads, dk, dv), F32)],
        scratch_shapes=scratch,
        compiler_params=pltpu.CompilerParams(
            dimension_semantics=("arbitrary", "arbitrary"), vmem_limit_bytes=VMEM_LIMIT),
        name="gdn_layer",
    )(*args)


def _block_rows(bsz, seq, target, long_seqs=1):
    if seq * long_seqs >= target:
        return long_seqs, target // long_seqs
    return max(1, min(bsz, target // seq)), seq


def _trunk(x, mod, conv_a, conv_b, ssm_b, ln_g, ln_b, wa, wb, alpha):
    bsz, seq, _ = x.shape
    new_a, new_cb, new_s = [], [], []
    depth = mod.shape[0]
    for l in range(depth):
        i = l // 2
        if l % 2 == 0:
            bb, tt = _block_rows(bsz, seq, 512)
            x, nb = _conv_layer(x, mod[l], None if conv_a is None else conv_a[i],
                                wa["in"][i], wa["conv"][i], wa["out"][i], ln_g[l], ln_b[l],
                                bb=bb, tt=tt, alpha=alpha)
            new_a.append(nb)
        else:
            bb, tt = _block_rows(bsz, seq, *((512, 2) if conv_b is None else (GDN_ROWS,)))
            x, nb, s = _gdn_layer(x, mod[l], None if conv_b is None else conv_b[i],
                                  None if ssm_b is None else ssm_b[i],
                                  wb["in"][i], wb["ab"][i], wb["conv"][i], wb["a_log"][i],
                                  wb["dt_bias"][i], wb["norm"][i], wb["out"][i], ln_g[l], ln_b[l],
                                  bb=bb, tt=tt, alpha=alpha)
            new_cb.append(nb)
            new_s.append(s)
    return x, jnp.stack(new_a), jnp.stack(new_cb), jnp.stack(new_s)


def kernel(x_prompt, x_sample, state_conv_a, state_conv_b, state_ssm_b, c_prompt, c_sample, w_mod, b_mod, ln_g, ln_b, wa_in, wa_conv, wa_out, wb_in, wb_conv, wb_a_log, wb_dt_bias, wb_norm, wb_out):
    depth, d, _ = w_mod.shape
    alpha = (2 * depth) ** 0.25
    bp = x_prompt.shape[0]
    n_heads = wb_a_log.shape[1]
    ch = wb_conv.shape[2]
    vw = wb_out.shape[1]

    mod = _modulation(jnp.concatenate([c_prompt, c_sample], axis=0), w_mod, b_mod)
    mod = mod.reshape(depth, -1, 3, d)
    mod_p, mod_s = mod[:, :bp], mod[:, bp:]

    n_gdn = wb_in.shape[0]
    w_ab = jnp.zeros((n_gdn, d, 2 * LANES), F32)
    w_ab = w_ab.at[:, :, :n_heads].set(wb_in[:, :, ch + vw + n_heads:])
    w_ab = w_ab.at[:, :, LANES:LANES + n_heads].set(wb_in[:, :, ch + vw:ch + vw + n_heads])
    wa = {"in": wa_in.astype(BF16), "conv": wa_conv, "out": wa_out.astype(BF16)}
    wb = {"in": wb_in[:, :, :ch + vw].astype(BF16), "ab": w_ab.astype(BF16), "conv": wb_conv,
          "a_log": wb_a_log, "dt_bias": wb_dt_bias, "norm": wb_norm, "out": wb_out.astype(BF16)}

    y_p, ca_p, cb_p, s_p = _trunk(x_prompt, mod_p, None, None, None, ln_g, ln_b, wa, wb, alpha)
    y_s, ca_s, cb_s, s_s = _trunk(x_sample, mod_s, state_conv_a, state_conv_b, state_ssm_b,
                                  ln_g, ln_b, wa, wb, alpha)
    return (y_p, y_s, ca_p, cb_p, s_p, ca_s, cb_s, s_s)
```

```python
import functools

import jax
import jax.numpy as jnp
from jax import lax
from jax.experimental import pallas as pl
from jax.experimental.pallas import tpu as pltpu

F32 = jnp.float32
BF16 = jnp.bfloat16

LN_EPS = 1e-5
NORM_EPS = 1e-6
SUBLANES = 8
LANES = 128
HALO = SUBLANES
GDN_ROWS = 64
VMEM_LIMIT = 60 * 1024 * 1024


def _dot(a, b):
    return jnp.dot(a.astype(BF16), b.astype(BF16), preferred_element_type=F32)


def _dot_nt(a, b):
    return lax.dot_general(a.astype(BF16), b.astype(BF16), (((1,), (1,)), ((), ())),
                           preferred_element_type=F32)


def _dot_tn(a, b):
    return lax.dot_general(a.astype(BF16), b.astype(BF16), (((0,), (0,)), ((), ())),
                           preferred_element_type=F32)


def _silu(x):
    return x * jax.nn.sigmoid(x)


def _split3(x):
    hi = x.astype(BF16)
    r = x - hi.astype(F32)
    mid = r.astype(BF16)
    lo = (r - mid.astype(F32)).astype(BF16)
    return hi, mid, lo


def _layer_norm(r, g, b):
    mu = jnp.mean(r, axis=-1, keepdims=True)
    d = r - mu
    var = jnp.mean(d * d, axis=-1, keepdims=True)
    return d * lax.rsqrt(var + LN_EPS) * g + b


def _mod_kernel(c_ref, w_ref, b_ref, o_ref):
    cs = _silu(c_ref[...])
    o_ref[0] = _dot(cs, w_ref[0]) + b_ref[0]


def _modulation(c_all, w_mod, b_mod):
    depth, d, n3 = w_mod.shape
    rows = c_all.shape[0]
    nb = d
    return pl.pallas_call(
        _mod_kernel,
        grid=(depth, n3 // nb),
        in_specs=[
            pl.BlockSpec((rows, d), lambda l, j: (0, 0)),
            pl.BlockSpec((1, d, nb), lambda l, j: (l, 0, j)),
            pl.BlockSpec((1, 1, nb), lambda l, j: (l, 0, j)),
        ],
        out_specs=pl.BlockSpec((1, rows, nb), lambda l, j: (l, 0, j)),
        out_shape=jax.ShapeDtypeStruct((depth, rows, n3), F32),
        compiler_params=pltpu.CompilerParams(
            dimension_semantics=("arbitrary", "arbitrary"), vmem_limit_bytes=VMEM_LIMIT),
        name="modulation",
    )(c_all, w_mod, b_mod.reshape(depth, 1, n3))


def _modulate(x_ref, mod_ref):
    x = x_ref[...]
    shift = mod_ref[:, 0:1, :]
    scale = mod_ref[:, 1:2, :]
    return x, x * (1.0 + scale) + shift


def _init_conv_halo(halo_scr, st_ref, width):
    halo_scr[...] = jnp.zeros(halo_scr.shape, F32)
    if st_ref is not None:
        halo_scr[:, HALO - (width - 1):HALO, :] = st_ref[...]


def _causal_conv(halo_scr, nst_ref, p, wconv_ref, width, sl):
    bb, tt, cw = p.shape
    ext = jnp.concatenate([halo_scr[:, :, sl], p], axis=1)
    halo_scr[:, :, sl] = ext[:, tt:, :]
    nst_ref[:, :, sl] = ext[:, HALO + tt - (width - 1):, :]
    e = ext.reshape(bb * (HALO + tt), cw)
    e1 = pltpu.roll(e, 1, axis=0)
    tap = lambda k: wconv_ref[width - 1 - k:width - k, sl]
    y = None
    for m in range((width + 1) // 2):
        pair = e * tap(2 * m)
        if 2 * m + 1 < width:
            pair = pair + e1 * tap(2 * m + 1)
        if m:
            pair = pltpu.roll(pair, 2 * m, axis=0)
        y = pair if y is None else y + pair
    return y.reshape(bb, HALO + tt, cw)[:, HALO:, :].reshape(bb * tt, cw)


def _residual_norm(x, out, mod_ref, lng_ref, lnb_ref, o_ref, alpha):
    bb, tt, d = o_ref.shape
    gate = mod_ref[:, 2:3, :]
    r = alpha * x.reshape(bb, tt, d) + gate * out.reshape(bb, tt, d)
    o_ref[...] = _layer_norm(r, lng_ref[...], lnb_ref[...])


def _const_spec(shape):
    zeros = (0,) * len(shape)
    return pl.BlockSpec(shape, lambda *_: zeros, pipeline_mode=pl.Buffered(1))


def _conv_layer_kernel(*refs, has_state, width, col_w, alpha):
    if has_state:
        (x_ref, mod_ref, st_ref, win_ref, wconv_ref, wout_ref, lng_ref, lnb_ref,
         o_ref, nst_ref, halo_scr, y_scr) = refs
    else:
        (x_ref, mod_ref, win_ref, wconv_ref, wout_ref, lng_ref, lnb_ref,
         o_ref, nst_ref, halo_scr, y_scr) = refs
        st_ref = None
    bb, tt, d = x_ref.shape
    rows = bb * tt
    dc = wconv_ref.shape[1]

    @pl.when(pl.program_id(1) == 0)
    def _():
        _init_conv_halo(halo_scr, st_ref, width)

    x, u = _modulate(x_ref, mod_ref)
    u = u.reshape(rows, d).astype(BF16)

    for j in range(dc // col_w):
        sl = slice(j * col_w, (j + 1) * col_w)
        pc = _dot(u, win_ref[:, dc + j * col_w:dc + (j + 1) * col_w])
        ph = _dot(u, win_ref[:, 2 * dc + j * col_w:2 * dc + (j + 1) * col_w])
        conv = _causal_conv(halo_scr, nst_ref, (pc * ph).reshape(bb, tt, col_w), wconv_ref, width, sl)
        pb = _dot(u, win_ref[:, sl])
        pz = _dot(u, win_ref[:, 3 * dc + j * col_w:3 * dc + (j + 1) * col_w])
        y_scr[:, sl] = (pb * conv * _silu(pz)).astype(BF16)

    out = _dot(y_scr[...], wout_ref[...])
    _residual_norm(x, out, mod_ref, lng_ref, lnb_ref, o_ref, alpha)


def _conv_layer(x, mod, state, w_in, w_conv, w_out, ln_g, ln_b, *, bb, tt, alpha):
    bsz, seq, d = x.shape
    width, dc = w_conv.shape
    has_state = state is not None
    rows = bb * tt
    seq_spec = lambda n: pl.BlockSpec((bb, n, d if n == 3 else dc), lambda b, t: (b, 0, 0))
    in_specs = [pl.BlockSpec((bb, tt, d), lambda b, t: (b, t, 0)), seq_spec(3)]
    args = [x, mod]
    if has_state:
        in_specs.append(seq_spec(width - 1))
        args.append(state)
    in_specs += [_const_spec(w_in.shape), _const_spec(w_conv.shape), _const_spec(w_out.shape),
                 _const_spec((1, d)), _const_spec((1, d))]
    args += [w_in, w_conv, w_out, ln_g.reshape(1, d), ln_b.reshape(1, d)]
    kern = functools.partial(_conv_layer_kernel, has_state=has_state, width=width,
                             col_w=min(dc, 256), alpha=alpha)
    return pl.pallas_call(
        kern,
        grid=(bsz // bb, seq // tt),
        in_specs=in_specs,
        out_specs=[pl.BlockSpec((bb, tt, d), lambda b, t: (b, t, 0)), seq_spec(width - 1)],
        out_shape=[jax.ShapeDtypeStruct((bsz, seq, d), F32),
                   jax.ShapeDtypeStruct((bsz, width - 1, dc), F32)],
        scratch_shapes=[pltpu.VMEM((bb, HALO, dc), F32), pltpu.VMEM((rows, dc), BF16)],
        compiler_params=pltpu.CompilerParams(
            dimension_semantics=("arbitrary", "arbitrary"), vmem_limit_bytes=VMEM_LIMIT),
        name="conv_layer",
    )(*args)


def _tri_masks(n, seq_len):
    i = lax.broadcasted_iota(jnp.int32, (n, n), 0)
    j = lax.broadcasted_iota(jnp.int32, (n, n), 1)
    same = (i // seq_len) == (j // seq_len)
    causal = same & (i >= j)
    strict = same & (i > j)
    eye = (i == j).astype(F32)
    levels = []
    s = 1
    while s < seq_len:
        levels.append(((i // (2 * s)) == (j // (2 * s))) & ((i % (2 * s)) >= s) & ((j % (2 * s)) < s))
        s *= 2
    return causal, strict, eye, levels


def _gdn_layer_kernel(*refs, has_state, width, n_heads, dk, dv, seq_len, col_w, group, alpha):
    if has_state:
        (x_ref, mod_ref, cst_ref, s0_ref, win_ref, wab_ref, wconv_ref, alog_ref, dtb_ref, nw_ref,
         wout_ref, lng_ref, lnb_ref, o_ref, ncst_ref, ns_ref,
         halo_scr, qkv_scr, z_scr, gt_scr, gx_scr, kb_scr, u_scr, wq_scr, attn_scr, oh_scr) = refs
        s_scr = None
    else:
        (x_ref, mod_ref, win_ref, wab_ref, wconv_ref, alog_ref, dtb_ref, nw_ref,
         wout_ref, lng_ref, lnb_ref, o_ref, ncst_ref, ns_ref,
         halo_scr, qkv_scr, z_scr, gt_scr, gx_scr, kb_scr, u_scr, wq_scr, attn_scr, oh_scr,
         s_scr) = refs
        cst_ref = s0_ref = None
    bb, tt, d = x_ref.shape
    rows = bb * tt
    ch = wconv_ref.shape[1]
    qk = n_heads * dk
    vw = n_heads * dv
    n = min(GDN_ROWS, rows)
    span = min(seq_len, n)
    n_seq = n // span
    n_chunks = rows // n
    heads = range(n_heads)
    q_sl = lambda h: slice(h * dk, (h + 1) * dk)
    k_sl = lambda h: slice(qk + h * dk, qk + (h + 1) * dk)
    v_sl = lambda h: slice(2 * qk + h * dv, 2 * qk + (h + 1) * dv)
    o_sl = lambda h: slice(h * dv, (h + 1) * dv)

    @pl.when(pl.program_id(1) == 0)
    def _():
        _init_conv_halo(halo_scr, cst_ref, width)
        if s_scr is not None:
            s_scr[...] = jnp.zeros(s_scr.shape, F32)

    x, u = _modulate(x_ref, mod_ref)
    u = u.reshape(rows, d).astype(BF16)

    ab = _dot(u, wab_ref[...])
    lane = lax.broadcasted_iota(jnp.int32, (rows, LANES), 1)
    a = ab[:, :LANES] + dtb_ref[...]
    softplus = jnp.maximum(a, 0.0) + jnp.log(1.0 + jnp.exp(-jnp.abs(a)))
    g = jnp.where(lane < n_heads, -jnp.exp(alog_ref[...]) * softplus, 0.0)
    beta = jax.nn.sigmoid(ab[:, LANES:])
    ri = lax.broadcasted_iota(jnp.int32, (rows, rows), 0)
    rj = lax.broadcasted_iota(jnp.int32, (rows, rows), 1)
    tri = (((ri // span) == (rj // span)) & (ri >= rj)).astype(BF16)
    g_hi, g_mid, g_lo = _split3(g)
    g_cum = (jnp.dot(tri, g_hi, preferred_element_type=F32)
             + jnp.dot(tri, g_mid, preferred_element_type=F32)
             + jnp.dot(tri, g_lo, preferred_element_type=F32))
    for c in range(n_chunks):
        gt_scr[c] = g_cum[c * n:(c + 1) * n, :].T
    for h in heads:
        gx_scr[:, o_sl(h)] = jnp.broadcast_to(g_cum[:, h:h + 1], (rows, dk))
        kb_scr[:, o_sl(h)] = jnp.broadcast_to(beta[:, h:h + 1], (rows, dk))

    for j in range(ch // col_w):
        sl = slice(j * col_w, (j + 1) * col_w)
        p = _dot(u, win_ref[:, sl]).reshape(bb, tt, col_w)
        qkv_scr[:, sl] = _silu(_causal_conv(halo_scr, ncst_ref, p, wconv_ref, width, sl))

    z_scr[...] = _silu(_dot(u, win_ref[:, ch:ch + vw]))

    for h in heads:
        t = qkv_scr[:, q_sl(h)]
        qkv_scr[:, q_sl(h)] = t * (lax.rsqrt(jnp.sum(t * t, axis=-1, keepdims=True) + NORM_EPS)
                                   * dk ** -0.5)
        t = qkv_scr[:, k_sl(h)]
        t = t * lax.rsqrt(jnp.sum(t * t, axis=-1, keepdims=True) + NORM_EPS)
        qkv_scr[:, k_sl(h)] = t
        beta_x = kb_scr[:, o_sl(h)]
        qkv_scr[:, v_sl(h)] = qkv_scr[:, v_sl(h)] * beta_x
        kb_scr[:, o_sl(h)] = t * beta_x

    causal, strict, eye, levels = _tri_masks(n, span)
    row_seq = lax.broadcasted_iota(jnp.int32, (n, 1), 0) // span

    def local_body(gi, carry):
        items = []
        for ci in range(group):
            c = gi * group + ci
            rs = pl.ds(pl.multiple_of(c * n, n), n)
            items += [(c, rs, h) for h in heads]
        decay, a_mat, t_inv = [], [], []
        for c, rs, h in items:
            diff = gx_scr[rs, h * dv:h * dv + n] - gt_scr[c, h:h + 1, :]
            decay.append(jnp.where(causal, jnp.exp(jnp.where(causal, diff, 0.0)), 0.0))
        for (c, rs, h), dec in zip(items, decay):
            k = qkv_scr[rs, k_sl(h)]
            a_mat.append(jnp.where(strict, _dot_nt(kb_scr[rs, o_sl(h)], k) * dec, 0.0))
        for (c, rs, h), dec in zip(items, decay):
            attn_scr[rs, h * dv:h * dv + n] = (
                _dot_nt(qkv_scr[rs, q_sl(h)], qkv_scr[rs, k_sl(h)]) * dec)
        t_inv = [eye - jnp.where(levels[0], a, 0.0) for a in a_mat]
        for li, m in enumerate(levels[1:]):
            s = 2 << li
            if s % SUBLANES:
                y = [_dot(jnp.where(m, a, 0.0), t) for a, t in zip(a_mat, t_inv)]
                xs = [_dot(t, yy) for t, yy in zip(t_inv, y)]
                t_inv = [t - xx for t, xx in zip(t_inv, xs)]
                continue
            lo = [(b * 2 * s + s, b * 2 * s + 2 * s) for b in range(n // (2 * s))]
            take = lambda x: jnp.concatenate([x[r0:r1] for r0, r1 in lo], axis=0)
            def put(xc, rest):
                parts = []
                for b, (r0, r1) in enumerate(lo):
                    parts += [rest(r0 - s, r0), xc[b * s:(b + 1) * s]]
                return jnp.concatenate(parts, axis=0)
            zeros = jnp.zeros((s, n), F32)
            y = [put(_dot(take(jnp.where(m, a, 0.0)), t), lambda r0, r1: zeros)
                 for a, t in zip(a_mat, t_inv)]
            xs = [_dot(take(t), yy) for t, yy in zip(t_inv, y)]
            t_inv = [put(take(t) - xx, lambda r0, r1, t=t: t[r0:r1]) for t, xx in zip(t_inv, xs)]
        for (c, rs, h), t in zip(items, t_inv):
            e_x = jnp.exp(gx_scr[rs, o_sl(h)])
            rhs = jnp.concatenate([qkv_scr[rs, v_sl(h)], kb_scr[rs, o_sl(h)] * e_x], axis=1)
            sol = _dot(t, rhs)
            u_scr[rs, o_sl(h)] = sol[:, :dv]
            wq_scr[c, 0:n, o_sl(h)] = sol[:, dv:]
            wq_scr[c, n:2 * n, o_sl(h)] = qkv_scr[rs, q_sl(h)] * e_x
        return carry

    lax.fori_loop(0, n_chunks // group, local_body, 0)

    chunks_per_seq = tt // n if not has_state else n_chunks

    def state_body(c, carry):
        if has_state:
            lanes = [(0, s) for s in range(n_seq)]
            rows_of = lambda q: slice(c * n, (c + 1) * n)
            chunk_of = lambda q: c
        else:
            lanes = [(q, 0) for q in range(bb)]
            rows_of = lambda q: slice(q * tt + c * n, q * tt + (c + 1) * n)
            chunk_of = lambda q: q * chunks_per_seq + c
        groups = sorted({q for q, _ in lanes})
        masks = [None if n_seq == 1 else (row_seq == s) for s in range(n_seq)]
        if has_state:
            s_prev = {(h, q, s): s0_ref[c * n_seq + s, h] for h in heads for q, s in lanes}
        else:
            s_prev = {(h, q, s): s_scr[q, h] for h in heads for q, s in lanes}
        def own_rows(q, s, h):
            if n_seq == 1:
                return wq_scr[chunk_of(q), :, o_sl(h)]
            return jnp.concatenate(
                [wq_scr[chunk_of(q), s * span:(s + 1) * span, o_sl(h)],
                 wq_scr[chunk_of(q), n + s * span:n + (s + 1) * span, o_sl(h)]], axis=0)
        ws = {(h, q, s): _dot(own_rows(q, s, h), s_prev[(h, q, s)])
              for h in heads for q, s in lanes}
        v_new, v_dec, o_part, g_last = {}, {}, {}, {}
        for h in heads:
            for q in groups:
                w_s = jnp.concatenate([ws[(h, q, s)][:span] for s in range(n_seq)], axis=0)
                v_new[h, q] = u_scr[rows_of(q), o_sl(h)] - w_s
                o_part[h, q] = jnp.concatenate(
                    [ws[(h, q, s)][span:] for s in range(n_seq)], axis=0)
                gx = gx_scr[rows_of(q), o_sl(h)]
                for s in range(n_seq):
                    g_last[h, q, s] = gx[(s + 1) * span - 1:(s + 1) * span, :]
                g_end = jnp.concatenate(
                    [jnp.broadcast_to(g_last[h, q, s], (span, dv)) for s in range(n_seq)], axis=0)
                v_dec[h, q] = v_new[h, q] * jnp.exp(g_end - gx)
        for h in heads:
            for q in groups:
                oh_scr[rows_of(q), o_sl(h)] = o_part[h, q] + _dot(
                    attn_scr[rows_of(q), h * dv:h * dv + n], v_new[h, q])
        for h in heads:
            for q in groups:
                vs = v_dec[h, q] if n_seq == 1 else jnp.concatenate(
                    [jnp.where(masks[s], v_dec[h, q], 0.0) for s in range(n_seq)], axis=1)
                ktv = _dot_tn(qkv_scr[rows_of(q), k_sl(h)], vs)
                for s in range(n_seq):
                    s_new = (s_prev[(h, q, s)] * jnp.exp(g_last[h, q, s])
                             + ktv[:, s * dv:(s + 1) * dv])
                    if has_state:
                        ns_ref[c * n_seq + s, h] = s_new
                    else:
                        s_scr[q, h] = s_new
        return carry

    for c in range(chunks_per_seq):
        state_body(c, 0)
    if not has_state:
        ns_ref[...] = s_scr[...]

    for h in heads:
        o = oh_scr[:, o_sl(h)]
        o = o * lax.rsqrt(jnp.mean(o * o, axis=-1, keepdims=True) + NORM_EPS) * nw_ref[...]
        oh_scr[:, o_sl(h)] = o * z_scr[:, o_sl(h)]
    out = _dot(oh_scr[...], wout_ref[...])
    _residual_norm(x, out, mod_ref, lng_ref, lnb_ref, o_ref, alpha)


def _gdn_layer(x, mod, conv_state, ssm_state, w_in, w_ab, w_conv, a_log, dt_bias, norm_w, w_out,
               ln_g, ln_b, *, bb, tt, alpha):
    bsz, seq, d = x.shape
    width, ch = w_conv.shape
    n_heads = a_log.shape[0]
    vw = w_out.shape[0]
    dv = vw // n_heads
    dk = (ch - vw) // 2 // n_heads
    qk = n_heads * dk
    has_state = conv_state is not None
    rows = bb * tt
    seq_len = min(seq, GDN_ROWS)
    n = min(GDN_ROWS, rows)
    n_chunks = rows // n
    assert dk == dv == LANES, "one head per lane tile"
    assert rows % n == 0 and n % seq_len == 0
    assert has_state == (tt == seq), "a carried state needs whole sequences per block"
    group = max(g for g in (8, 4, 2, 1) if n_chunks % g == 0)

    pad = lambda vec: jnp.zeros((1, LANES), F32).at[0, :n_heads].set(vec)
    in_specs = [pl.BlockSpec((bb, tt, d), lambda b, t: (b, t, 0)),
                pl.BlockSpec((bb, 3, d), lambda b, t: (b, 0, 0))]
    args = [x, mod]
    if has_state:
        in_specs += [pl.BlockSpec((bb, width - 1, ch), lambda b, t: (b, 0, 0)),
                     pl.BlockSpec((bb, n_heads, dk, dv), lambda b, t: (b, 0, 0, 0))]
        args += [conv_state, ssm_state]
    in_specs += [_const_spec(w_in.shape), _const_spec(w_ab.shape), _const_spec(w_conv.shape),
                 _const_spec((1, LANES)), _const_spec((1, LANES)), _const_spec((1, dv)),
                 _const_spec(w_out.shape), _const_spec((1, d)), _const_spec((1, d))]
    args += [w_in, w_ab, w_conv, pad(a_log), pad(dt_bias), norm_w.reshape(1, dv), w_out,
             ln_g.reshape(1, d), ln_b.reshape(1, d)]
    scratch = [pltpu.VMEM((bb, HALO, ch), F32),
               pltpu.VMEM((rows, ch), F32),
               pltpu.VMEM((rows, vw), F32),
               pltpu.VMEM((n_chunks, LANES, n), F32),
               pltpu.VMEM((rows, vw), F32),
               pltpu.VMEM((rows, qk), F32),
               pltpu.VMEM((rows, vw), F32),
               pltpu.VMEM((n_chunks, 2 * n, qk), F32),
               pltpu.VMEM((rows, vw), F32),
               pltpu.VMEM((rows, vw), F32)]
    if not has_state:
        scratch.append(pltpu.VMEM((bb, n_heads, dk, dv), F32))
    kern = functools.partial(_gdn_layer_kernel, has_state=has_state, width=width, n_heads=n_heads,
                             dk=dk, dv=dv, seq_len=seq_len, col_w=min(ch, 512), group=group,
                             alpha=alpha)
    return pl.pallas_call(
        kern,
        grid=(bsz // bb, seq // tt),
        in_specs=in_specs,
        out_specs=[pl.BlockSpec((bb, tt, d), lambda b, t: (b, t, 0)),
                   pl.BlockSpec((bb, width - 1, ch), lambda b, t: (b, 0, 0)),
                   pl.BlockSpec((bb, n_heads, dk, dv), lambda b, t: (b, 0, 0, 0))],
        out_shape=[jax.ShapeDtypeStruct((bsz, seq, d), F32),
                   jax.ShapeDtypeStruct((bsz, width - 1, ch), F32),
                   jax.ShapeDtypeStruct((bsz, n_heads, dk, dv), F32)],
        scratch_shapes=scratch,
        compiler_params=pltpu.CompilerParams(
            dimension_semantics=("arbitrary", "arbitrary"), vmem_limit_bytes=VMEM_LIMIT),
        name="gdn_layer",
    )(*args)


def _block_rows(bsz, seq, target, long_seqs=1):
    if seq * long_seqs >= target:
        return long_seqs, target // long_seqs
    return max(1, min(bsz, target // seq)), seq


def _trunk(x, mod, conv_a, conv_b, ssm_b, ln_g, ln_b, wa, wb, alpha):
    bsz, seq, _ = x.shape
    new_a, new_cb, new_s = [], [], []
    depth = mod.shape[0]
    for l in range(depth):
        i = l // 2
        if l % 2 == 0:
            bb, tt = _block_rows(bsz, seq, 512)
            x, nb = _conv_layer(x, mod[l], None if conv_a is None else conv_a[i],
                                wa["in"][i], wa["conv"][i], wa["out"][i], ln_g[l], ln_b[l],
                                bb=bb, tt=tt, alpha=alpha)
            new_a.append(nb)
        else:
            bb, tt = _block_rows(bsz, seq, *((512, 2) if conv_b is None else (2 * GDN_ROWS,)))
            x, nb, s = _gdn_layer(x, mod[l], None if conv_b is None else conv_b[i],
                                  None if ssm_b is None else ssm_b[i],
                                  wb["in"][i], wb["ab"][i], wb["conv"][i], wb["a_log"][i],
                                  wb["dt_bias"][i], wb["norm"][i], wb["out"][i], ln_g[l], ln_b[l],
                                  bb=bb, tt=tt, alpha=alpha)
            new_cb.append(nb)
            new_s.append(s)
    return x, jnp.stack(new_a), jnp.stack(new_cb), jnp.stack(new_s)


def kernel(x_prompt, x_sample, state_conv_a, state_conv_b, state_ssm_b, c_prompt, c_sample, w_mod, b_mod, ln_g, ln_b, wa_in, wa_conv, wa_out, wb_in, wb_conv, wb_a_log, wb_dt_bias, wb_norm, wb_out):
    depth, d, _ = w_mod.shape
    alpha = (2 * depth) ** 0.25
    bp = x_prompt.shape[0]
    n_heads = wb_a_log.shape[1]
    ch = wb_conv.shape[2]
    vw = wb_out.shape[1]

    mod = _modulation(jnp.concatenate([c_prompt, c_sample], axis=0), w_mod, b_mod)
    mod = mod.reshape(depth, -1, 3, d)
    mod_p, mod_s = mod[:, :bp], mod[:, bp:]

    n_gdn = wb_in.shape[0]
    w_ab = jnp.zeros((n_gdn, d, 2 * LANES), F32)
    w_ab = w_ab.at[:, :, :n_heads].set(wb_in[:, :, ch + vw + n_heads:])
    w_ab = w_ab.at[:, :, LANES:LANES + n_heads].set(wb_in[:, :, ch + vw:ch + vw + n_heads])
    wa = {"in": wa_in.astype(BF16), "conv": wa_conv, "out": wa_out.astype(BF16)}
    wb = {"in": wb_in[:, :, :ch + vw].astype(BF16), "ab": w_ab.astype(BF16), "conv": wb_conv,
          "a_log": wb_a_log, "dt_bias": wb_dt_bias, "norm": wb_norm, "out": wb_out.astype(BF16)}

    y_p, ca_p, cb_p, s_p = _trunk(x_prompt, mod_p, None, None, None, ln_g, ln_b, wa, wb, alpha)
    y_s, ca_s, cb_s, s_s = _trunk(x_sample, mod_s, state_conv_a, state_conv_b, state_ssm_b,
                                  ln_g, ln_b, wa, wb, alpha)
    return (y_p, y_s, ca_p, cb_p, s_p, ca_s, cb_s, s_s)
```

```python
import functools

import jax
import jax.numpy as jnp
from jax import lax
from jax.experimental import pallas as pl
from jax.experimental.pallas import tpu as pltpu

F32 = jnp.float32
BF16 = jnp.bfloat16

LN_EPS = 1e-5
NORM_EPS = 1e-6
SUBLANES = 8
LANES = 128
HALO = SUBLANES
GDN_ROWS = 64
VMEM_LIMIT = 60 * 1024 * 1024


def _dot(a, b):
    return jnp.dot(a.astype(BF16), b.astype(BF16), preferred_element_type=F32)


def _dot_nt(a, b):
    return lax.dot_general(a.astype(BF16), b.astype(BF16), (((1,), (1,)), ((), ())),
                           preferred_element_type=F32)


def _dot_tn(a, b):
    return lax.dot_general(a.astype(BF16), b.astype(BF16), (((0,), (0,)), ((), ())),
                           preferred_element_type=F32)


def _silu(x):
    return x * jax.nn.sigmoid(x)


def _split3(x):
    hi = x.astype(BF16)
    r = x - hi.astype(F32)
    mid = r.astype(BF16)
    lo = (r - mid.astype(F32)).astype(BF16)
    return hi, mid, lo


def _layer_norm(r, g, b):
    mu = jnp.mean(r, axis=-1, keepdims=True)
    d = r - mu
    var = jnp.mean(d * d, axis=-1, keepdims=True)
    return d * lax.rsqrt(var + LN_EPS) * g + b


def _mod_kernel(c_ref, w_ref, b_ref, o_ref):
    cs = _silu(c_ref[...])
    o_ref[0] = _dot(cs, w_ref[0]) + b_ref[0]


def _modulation(c_all, w_mod, b_mod):
    depth, d, n3 = w_mod.shape
    rows = c_all.shape[0]
    nb = d
    return pl.pallas_call(
        _mod_kernel,
        grid=(depth, n3 // nb),
        in_specs=[
            pl.BlockSpec((rows, d), lambda l, j: (0, 0)),
            pl.BlockSpec((1, d, nb), lambda l, j: (l, 0, j)),
            pl.BlockSpec((1, 1, nb), lambda l, j: (l, 0, j)),
        ],
        out_specs=pl.BlockSpec((1, rows, nb), lambda l, j: (l, 0, j)),
        out_shape=jax.ShapeDtypeStruct((depth, rows, n3), F32),
        compiler_params=pltpu.CompilerParams(
            dimension_semantics=("arbitrary", "arbitrary"), vmem_limit_bytes=VMEM_LIMIT),
        name="modulation",
    )(c_all, w_mod, b_mod.reshape(depth, 1, n3))


def _modulate(x_ref, mod_ref):
    x = x_ref[...]
    shift = mod_ref[:, 0:1, :]
    scale = mod_ref[:, 1:2, :]
    return x, x * (1.0 + scale) + shift


def _init_conv_halo(halo_scr, st_ref, width):
    halo_scr[...] = jnp.zeros(halo_scr.shape, F32)
    if st_ref is not None:
        halo_scr[:, HALO - (width - 1):HALO, :] = st_ref[...]


def _causal_conv(halo_scr, nst_ref, p, wconv_ref, width, sl):
    bb, tt, cw = p.shape
    ext = jnp.concatenate([halo_scr[:, :, sl], p], axis=1)
    halo_scr[:, :, sl] = ext[:, tt:, :]
    nst_ref[:, :, sl] = ext[:, HALO + tt - (width - 1):, :]
    e = ext.reshape(bb * (HALO + tt), cw)
    e1 = pltpu.roll(e, 1, axis=0)
    tap = lambda k: wconv_ref[width - 1 - k:width - k, sl]
    y = None
    for m in range((width + 1) // 2):
        pair = e * tap(2 * m)
        if 2 * m + 1 < width:
            pair = pair + e1 * tap(2 * m + 1)
        if m:
            pair = pltpu.roll(pair, 2 * m, axis=0)
        y = pair if y is None else y + pair
    return y.reshape(bb, HALO + tt, cw)[:, HALO:, :].reshape(bb * tt, cw)


def _residual_norm(x, out, mod_ref, lng_ref, lnb_ref, o_ref, alpha):
    bb, tt, d = o_ref.shape
    gate = mod_ref[:, 2:3, :]
    r = alpha * x.reshape(bb, tt, d) + gate * out.reshape(bb, tt, d)
    o_ref[...] = _layer_norm(r, lng_ref[...], lnb_ref[...])


def _const_spec(shape):
    zeros = (0,) * len(shape)
    return pl.BlockSpec(shape, lambda *_: zeros, pipeline_mode=pl.Buffered(1))


def _conv_layer_kernel(*refs, has_state, width, col_w, alpha):
    if has_state:
        (x_ref, mod_ref, st_ref, win_ref, wconv_ref, wout_ref, lng_ref, lnb_ref,
         o_ref, nst_ref, halo_scr, y_scr) = refs
    else:
        (x_ref, mod_ref, win_ref, wconv_ref, wout_ref, lng_ref, lnb_ref,
         o_ref, nst_ref, halo_scr, y_scr) = refs
        st_ref = None
    bb, tt, d = x_ref.shape
    rows = bb * tt
    dc = wconv_ref.shape[1]

    @pl.when(pl.program_id(1) == 0)
    def _():
        _init_conv_halo(halo_scr, st_ref, width)

    x, u = _modulate(x_ref, mod_ref)
    u = u.reshape(rows, d).astype(BF16)

    for j in range(dc // col_w):
        sl = slice(j * col_w, (j + 1) * col_w)
        pc = _dot(u, win_ref[:, dc + j * col_w:dc + (j + 1) * col_w])
        ph = _dot(u, win_ref[:, 2 * dc + j * col_w:2 * dc + (j + 1) * col_w])
        conv = _causal_conv(halo_scr, nst_ref, (pc * ph).reshape(bb, tt, col_w), wconv_ref, width, sl)
        pb = _dot(u, win_ref[:, sl])
        pz = _dot(u, win_ref[:, 3 * dc + j * col_w:3 * dc + (j + 1) * col_w])
        y_scr[:, sl] = (pb * conv * _silu(pz)).astype(BF16)

    out = _dot(y_scr[...], wout_ref[...])
    _residual_norm(x, out, mod_ref, lng_ref, lnb_ref, o_ref, alpha)


def _conv_layer(x, mod, state, w_in, w_conv, w_out, ln_g, ln_b, *, bb, tt, alpha):
    bsz, seq, d = x.shape
    width, dc = w_conv.shape
    has_state = state is not None
    rows = bb * tt
    seq_spec = lambda n: pl.BlockSpec((bb, n, d if n == 3 else dc), lambda b, t: (b, 0, 0))
    in_specs = [pl.BlockSpec((bb, tt, d), lambda b, t: (b, t, 0)), seq_spec(3)]
    args = [x, mod]
    if has_state:
        in_specs.append(seq_spec(width - 1))
        args.append(state)
    in_specs += [_const_spec(w_in.shape), _const_spec(w_conv.shape), _const_spec(w_out.shape),
                 _const_spec((1, d)), _const_spec((1, d))]
    args += [w_in, w_conv, w_out, ln_g.reshape(1, d), ln_b.reshape(1, d)]
    kern = functools.partial(_conv_layer_kernel, has_state=has_state, width=width,
                             col_w=min(dc, 256), alpha=alpha)
    return pl.pallas_call(
        kern,
        grid=(bsz // bb, seq // tt),
        in_specs=in_specs,
        out_specs=[pl.BlockSpec((bb, tt, d), lambda b, t: (b, t, 0)), seq_spec(width - 1)],
        out_shape=[jax.ShapeDtypeStruct((bsz, seq, d), F32),
                   jax.ShapeDtypeStruct((bsz, width - 1, dc), F32)],
        scratch_shapes=[pltpu.VMEM((bb, HALO, dc), F32), pltpu.VMEM((rows, dc), BF16)],
        compiler_params=pltpu.CompilerParams(
            dimension_semantics=("arbitrary", "arbitrary"), vmem_limit_bytes=VMEM_LIMIT),
        name="conv_layer",
    )(*args)


def _tri_masks(n, seq_len):
    i = lax.broadcasted_iota(jnp.int32, (n, n), 0)
    j = lax.broadcasted_iota(jnp.int32, (n, n), 1)
    same = (i // seq_len) == (j // seq_len)
    causal = same & (i >= j)
    strict = same & (i > j)
    eye = (i == j).astype(F32)
    levels = []
    s = 1
    while s < seq_len:
        levels.append(((i // (2 * s)) == (j // (2 * s))) & ((i % (2 * s)) >= s) & ((j % (2 * s)) < s))
        s *= 2
    return causal, strict, eye, levels


def _gdn_layer_kernel(*refs, has_state, width, n_heads, dk, dv, seq_len, col_w, group, alpha):
    if has_state:
        (x_ref, mod_ref, cst_ref, s0_ref, win_ref, wconv_ref, alog_ref, dtb_ref, nw_ref,
         wout_ref, lng_ref, lnb_ref, o_ref, ncst_ref, ns_ref,
         halo_scr, qkv_scr, z_scr, gt_scr, gx_scr, kb_scr, u_scr, wq_scr, attn_scr, oh_scr) = refs
        s_scr = None
    else:
        (x_ref, mod_ref, win_ref, wconv_ref, alog_ref, dtb_ref, nw_ref,
         wout_ref, lng_ref, lnb_ref, o_ref, ncst_ref, ns_ref,
         halo_scr, qkv_scr, z_scr, gt_scr, gx_scr, kb_scr, u_scr, wq_scr, attn_scr, oh_scr,
         s_scr) = refs
        cst_ref = s0_ref = None
    bb, tt, d = x_ref.shape
    rows = bb * tt
    ch = wconv_ref.shape[1]
    qk = n_heads * dk
    vw = n_heads * dv
    n = min(GDN_ROWS, rows)
    span = min(seq_len, n)
    n_seq = n // span
    n_chunks = rows // n
    heads = range(n_heads)
    q_sl = lambda h: slice(h * dk, (h + 1) * dk)
    k_sl = lambda h: slice(qk + h * dk, qk + (h + 1) * dk)
    v_sl = lambda h: slice(2 * qk + h * dv, 2 * qk + (h + 1) * dv)
    o_sl = lambda h: slice(h * dv, (h + 1) * dv)

    @pl.when(pl.program_id(1) == 0)
    def _():
        _init_conv_halo(halo_scr, cst_ref, width)
        if s_scr is not None:
            s_scr[...] = jnp.zeros(s_scr.shape, F32)

    x, u = _modulate(x_ref, mod_ref)
    u = u.reshape(rows, d).astype(BF16)

    ab = _dot(u, win_ref[:, ch + vw:ch + vw + LANES])
    lane = lax.broadcasted_iota(jnp.int32, (rows, LANES), 1)
    a = ab + dtb_ref[...]
    softplus = jnp.maximum(a, 0.0) + jnp.log(1.0 + jnp.exp(-jnp.abs(a)))
    g = jnp.where((lane >= n_heads) & (lane < 2 * n_heads),
                  -jnp.exp(alog_ref[...]) * softplus, 0.0)
    beta = jax.nn.sigmoid(ab)
    ri = lax.broadcasted_iota(jnp.int32, (rows, rows), 0)
    rj = lax.broadcasted_iota(jnp.int32, (rows, rows), 1)
    tri = (((ri // span) == (rj // span)) & (ri >= rj)).astype(BF16)
    g_hi, g_mid, g_lo = _split3(g)
    g_cum = (jnp.dot(tri, g_hi, preferred_element_type=F32)
             + jnp.dot(tri, g_mid, preferred_element_type=F32)
             + jnp.dot(tri, g_lo, preferred_element_type=F32))
    for c in range(n_chunks):
        gt_scr[c] = g_cum[c * n:(c + 1) * n, :].T
    for h in heads:
        gx_scr[:, o_sl(h)] = jnp.broadcast_to(g_cum[:, n_heads + h:n_heads + h + 1], (rows, dk))
        kb_scr[:, o_sl(h)] = jnp.broadcast_to(beta[:, h:h + 1], (rows, dk))

    for j in range(ch // col_w):
        sl = slice(j * col_w, (j + 1) * col_w)
        p = _dot(u, win_ref[:, sl]).reshape(bb, tt, col_w)
        qkv_scr[:, sl] = _silu(_causal_conv(halo_scr, ncst_ref, p, wconv_ref, width, sl))

    z_scr[...] = _silu(_dot(u, win_ref[:, ch:ch + vw]))

    for h in heads:
        t = qkv_scr[:, q_sl(h)]
        qkv_scr[:, q_sl(h)] = t * (lax.rsqrt(jnp.sum(t * t, axis=-1, keepdims=True) + NORM_EPS)
                                   * dk ** -0.5)
        t = qkv_scr[:, k_sl(h)]
        t = t * lax.rsqrt(jnp.sum(t * t, axis=-1, keepdims=True) + NORM_EPS)
        qkv_scr[:, k_sl(h)] = t
        beta_x = kb_scr[:, o_sl(h)]
        qkv_scr[:, v_sl(h)] = qkv_scr[:, v_sl(h)] * beta_x
        kb_scr[:, o_sl(h)] = t * beta_x

    causal, strict, eye, levels = _tri_masks(n, span)
    row_seq = lax.broadcasted_iota(jnp.int32, (n, 1), 0) // span

    def local_body(gi, carry):
        items = []
        for ci in range(group):
            c = gi * group + ci
            rs = pl.ds(pl.multiple_of(c * n, n), n)
            items += [(c, rs, h) for h in heads]
        decay, a_mat, t_inv = [], [], []
        for c, rs, h in items:
            g_row = gt_scr[c, n_heads + h:n_heads + h + 1, :]
            diff = gx_scr[rs, h * dv:h * dv + n] - g_row
            decay.append(jnp.where(causal, jnp.exp(jnp.where(causal, diff, 0.0)), 0.0))
        for (c, rs, h), dec in zip(items, decay):
            k = qkv_scr[rs, k_sl(h)]
            a_mat.append(jnp.where(strict, _dot_nt(kb_scr[rs, o_sl(h)], k) * dec, 0.0))
        for (c, rs, h), dec in zip(items, decay):
            attn_scr[rs, h * dv:h * dv + n] = (
                _dot_nt(qkv_scr[rs, q_sl(h)], qkv_scr[rs, k_sl(h)]) * dec)
        t_inv = [eye - jnp.where(levels[0], a, 0.0) for a in a_mat]
        for li, m in enumerate(levels[1:]):
            s = 2 << li
            if s % SUBLANES:
                y = [_dot(jnp.where(m, a, 0.0), t) for a, t in zip(a_mat, t_inv)]
                xs = [_dot(t, yy) for t, yy in zip(t_inv, y)]
                t_inv = [t - xx for t, xx in zip(t_inv, xs)]
                continue
            lo = [(b * 2 * s + s, b * 2 * s + 2 * s) for b in range(n // (2 * s))]
            take = lambda x: jnp.concatenate([x[r0:r1] for r0, r1 in lo], axis=0)
            def put(xc, rest):
                parts = []
                for b, (r0, r1) in enumerate(lo):
                    parts += [rest(r0 - s, r0), xc[b * s:(b + 1) * s]]
                return jnp.concatenate(parts, axis=0)
            zeros = jnp.zeros((s, n), F32)
            y = [put(_dot(take(jnp.where(m, a, 0.0)), t), lambda r0, r1: zeros)
                 for a, t in zip(a_mat, t_inv)]
            xs = [_dot(take(t), yy) for t, yy in zip(t_inv, y)]
            t_inv = [put(take(t) - xx, lambda r0, r1, t=t: t[r0:r1]) for t, xx in zip(t_inv, xs)]
        for (c, rs, h), t in zip(items, t_inv):
            e_x = jnp.exp(gx_scr[rs, o_sl(h)])
            rhs = jnp.concatenate([qkv_scr[rs, v_sl(h)], kb_scr[rs, o_sl(h)] * e_x], axis=1)
            sol = _dot(t, rhs)
            u_scr[rs, o_sl(h)] = sol[:, :dv]
            wq_scr[c, 0:n, o_sl(h)] = sol[:, dv:]
            wq_scr[c, n:2 * n, o_sl(h)] = qkv_scr[rs, q_sl(h)] * e_x
        return carry

    lax.fori_loop(0, n_chunks // group, local_body, 0)

    chunks_per_seq = tt // n if not has_state else n_chunks

    def state_body(c, carry):
        if has_state:
            lanes = [(0, s) for s in range(n_seq)]
            rows_of = lambda q: slice(c * n, (c + 1) * n)
            chunk_of = lambda q: c
        else:
            lanes = [(q, 0) for q in range(bb)]
            rows_of = lambda q: slice(q * tt + c * n, q * tt + (c + 1) * n)
            chunk_of = lambda q: q * chunks_per_seq + c
        groups = sorted({q for q, _ in lanes})
        masks = [None if n_seq == 1 else (row_seq == s) for s in range(n_seq)]
        if has_state:
            s_prev = {(h, q, s): s0_ref[c * n_seq + s, h] for h in heads for q, s in lanes}
        else:
            s_prev = {(h, q, s): s_scr[q, h] for h in heads for q, s in lanes}
        def own_rows(q, s, h):
            if n_seq == 1:
                return wq_scr[chunk_of(q), :, o_sl(h)]
            return jnp.concatenate(
                [wq_scr[chunk_of(q), s * span:(s + 1) * span, o_sl(h)],
                 wq_scr[chunk_of(q), n + s * span:n + (s + 1) * span, o_sl(h)]], axis=0)
        ws = {(h, q, s): _dot(own_rows(q, s, h), s_prev[(h, q, s)])
              for h in heads for q, s in lanes}
        v_new, v_dec, o_part, g_last = {}, {}, {}, {}
        for h in heads:
            for q in groups:
                w_s = jnp.concatenate([ws[(h, q, s)][:span] for s in range(n_seq)], axis=0)
                v_new[h, q] = u_scr[rows_of(q), o_sl(h)] - w_s
                o_part[h, q] = jnp.concatenate(
                    [ws[(h, q, s)][span:] for s in range(n_seq)], axis=0)
                gx = gx_scr[rows_of(q), o_sl(h)]
                for s in range(n_seq):
                    g_last[h, q, s] = gx[(s + 1) * span - 1:(s + 1) * span, :]
                g_end = jnp.concatenate(
                    [jnp.broadcast_to(g_last[h, q, s], (span, dv)) for s in range(n_seq)], axis=0)
                v_dec[h, q] = v_new[h, q] * jnp.exp(g_end - gx)
        for h in heads:
            for q in groups:
                oh_scr[rows_of(q), o_sl(h)] = o_part[h, q] + _dot(
                    attn_scr[rows_of(q), h * dv:h * dv + n], v_new[h, q])
        for h in heads:
            for q in groups:
                vs = v_dec[h, q] if n_seq == 1 else jnp.concatenate(
                    [jnp.where(masks[s], v_dec[h, q], 0.0) for s in range(n_seq)], axis=1)
                ktv = _dot_tn(qkv_scr[rows_of(q), k_sl(h)], vs)
                for s in range(n_seq):
                    s_new = (s_prev[(h, q, s)] * jnp.exp(g_last[h, q, s])
                             + ktv[:, s * dv:(s + 1) * dv])
                    if has_state:
                        ns_ref[c * n_seq + s, h] = s_new
                    else:
                        s_scr[q, h] = s_new
        return carry

    for c in range(chunks_per_seq):
        state_body(c, 0)
    if not has_state:
        ns_ref[...] = s_scr[...]

    for h in heads:
        o = oh_scr[:, o_sl(h)]
        o = o * lax.rsqrt(jnp.mean(o * o, axis=-1, keepdims=True) + NORM_EPS) * nw_ref[...]
        oh_scr[:, o_sl(h)] = o * z_scr[:, o_sl(h)]
    out = _dot(oh_scr[...], wout_ref[...])
    _residual_norm(x, out, mod_ref, lng_ref, lnb_ref, o_ref, alpha)


def _gdn_layer(x, mod, conv_state, ssm_state, w_in, w_conv, a_log, dt_bias, norm_w, w_out,
               ln_g, ln_b, *, bb, tt, alpha):
    bsz, seq, d = x.shape
    width, ch = w_conv.shape
    n_heads = a_log.shape[0]
    vw = w_out.shape[0]
    dv = vw // n_heads
    dk = (ch - vw) // 2 // n_heads
    qk = n_heads * dk
    has_state = conv_state is not None
    rows = bb * tt
    seq_len = min(seq, GDN_ROWS)
    n = min(GDN_ROWS, rows)
    n_chunks = rows // n
    assert dk == dv == LANES, "one head per lane tile"
    assert rows % n == 0 and n % seq_len == 0
    assert has_state == (tt == seq), "a carried state needs whole sequences per block"
    group = max(g for g in (8, 4, 2, 1) if n_chunks % g == 0)

    assert w_in.shape[1] == ch + vw + LANES and 2 * n_heads <= LANES
    pad = lambda vec: jnp.zeros((1, LANES), F32).at[0, n_heads:2 * n_heads].set(vec)
    in_specs = [pl.BlockSpec((bb, tt, d), lambda b, t: (b, t, 0)),
                pl.BlockSpec((bb, 3, d), lambda b, t: (b, 0, 0))]
    args = [x, mod]
    if has_state:
        in_specs += [pl.BlockSpec((bb, width - 1, ch), lambda b, t: (b, 0, 0)),
                     pl.BlockSpec((bb, n_heads, dk, dv), lambda b, t: (b, 0, 0, 0))]
        args += [conv_state, ssm_state]
    in_specs += [_const_spec(w_in.shape), _const_spec(w_conv.shape),
                 _const_spec((1, LANES)), _const_spec((1, LANES)), _const_spec((1, dv)),
                 _const_spec(w_out.shape), _const_spec((1, d)), _const_spec((1, d))]
    args += [w_in, w_conv, pad(a_log), pad(dt_bias), norm_w.reshape(1, dv), w_out,
             ln_g.reshape(1, d), ln_b.reshape(1, d)]
    scratch = [pltpu.VMEM((bb, HALO, ch), F32),
               pltpu.VMEM((rows, ch), F32),
               pltpu.VMEM((rows, vw), F32),
               pltpu.VMEM((n_chunks, LANES, n), F32),
               pltpu.VMEM((rows, vw), F32),
               pltpu.VMEM((rows, qk), F32),
               pltpu.VMEM((rows, vw), F32),
               pltpu.VMEM((n_chunks, 2 * n, qk), F32),
               pltpu.VMEM((rows, vw), F32),
               pltpu.VMEM((rows, vw), F32)]
    if not has_state:
        scratch.append(pltpu.VMEM((bb, n_heads, dk, dv), F32))
    kern = functools.partial(_gdn_layer_kernel, has_state=has_state, width=width, n_heads=n_heads,
                             dk=dk, dv=dv, seq_len=seq_len, col_w=min(ch, 512), group=group,
                             alpha=alpha)
    return pl.pallas_call(
        kern,
        grid=(bsz // bb, seq // tt),
        in_specs=in_specs,
        out_specs=[pl.BlockSpec((bb, tt, d), lambda b, t: (b, t, 0)),
                   pl.BlockSpec((bb, width - 1, ch), lambda b, t: (b, 0, 0)),
                   pl.BlockSpec((bb, n_heads, dk, dv), lambda b, t: (b, 0, 0, 0))],
        out_shape=[jax.ShapeDtypeStruct((bsz, seq, d), F32),
                   jax.ShapeDtypeStruct((bsz, width - 1, ch), F32),
                   jax.ShapeDtypeStruct((bsz, n_heads, dk, dv), F32)],
        scratch_shapes=scratch,
        compiler_params=pltpu.CompilerParams(
            dimension_semantics=("arbitrary", "arbitrary"), vmem_limit_bytes=VMEM_LIMIT),
        name="gdn_layer",
    )(*args)


def _block_rows(bsz, seq, target, long_seqs=1):
    if seq * long_seqs >= target:
        return long_seqs, target // long_seqs
    return max(1, min(bsz, target // seq)), seq


def _trunk(x, mod, conv_a, conv_b, ssm_b, ln_g, ln_b, wa, wb, alpha):
    bsz, seq, _ = x.shape
    new_a, new_cb, new_s = [], [], []
    depth = mod.shape[0]
    for l in range(depth):
        i = l // 2
        if l % 2 == 0:
            bb, tt = _block_rows(bsz, seq, 512)
            x, nb = _conv_layer(x, mod[l], None if conv_a is None else conv_a[i],
                                wa["in"][i], wa["conv"][i], wa["out"][i], ln_g[l], ln_b[l],
                                bb=bb, tt=tt, alpha=alpha)
            new_a.append(nb)
        else:
            bb, tt = _block_rows(bsz, seq, *((512, 2) if conv_b is None else (2 * GDN_ROWS,)))
            x, nb, s = _gdn_layer(x, mod[l], None if conv_b is None else conv_b[i],
                                  None if ssm_b is None else ssm_b[i],
                                  wb["in"][i], wb["conv"][i], wb["a_log"][i],
                                  wb["dt_bias"][i], wb["norm"][i], wb["out"][i], ln_g[l], ln_b[l],
                                  bb=bb, tt=tt, alpha=alpha)
            new_cb.append(nb)
            new_s.append(s)
    return x, jnp.stack(new_a), jnp.stack(new_cb), jnp.stack(new_s)


def kernel(x_prompt, x_sample, state_conv_a, state_conv_b, state_ssm_b, c_prompt, c_sample, w_mod, b_mod, ln_g, ln_b, wa_in, wa_conv, wa_out, wb_in, wb_conv, wb_a_log, wb_dt_bias, wb_norm, wb_out):
    depth, d, _ = w_mod.shape
    alpha = (2 * depth) ** 0.25
    bp = x_prompt.shape[0]
    n_heads = wb_a_log.shape[1]
    ch = wb_conv.shape[2]
    vw = wb_out.shape[1]

    mod = _modulation(jnp.concatenate([c_prompt, c_sample], axis=0), w_mod, b_mod)
    mod = mod.reshape(depth, -1, 3, d)
    mod_p, mod_s = mod[:, :bp], mod[:, bp:]

    tail = ch + vw + LANES - wb_in.shape[2]
    wa = {"in": wa_in.astype(BF16), "conv": wa_conv, "out": wa_out.astype(BF16)}
    wb = {"in": jnp.pad(wb_in, ((0, 0), (0, 0), (0, tail))).astype(BF16), "conv": wb_conv,
          "a_log": wb_a_log, "dt_bias": wb_dt_bias, "norm": wb_norm, "out": wb_out.astype(BF16)}

    y_p, ca_p, cb_p, s_p = _trunk(x_prompt, mod_p, None, None, None, ln_g, ln_b, wa, wb, alpha)
    y_s, ca_s, cb_s, s_s = _trunk(x_sample, mod_s, state_conv_a, state_conv_b, state_ssm_b,
                                  ln_g, ln_b, wa, wb, alpha)
    return (y_p, y_s, ca_p, cb_p, s_p, ca_s, cb_s, s_s)
```

```python
import functools

import jax
import jax.numpy as jnp
from jax import lax
from jax.experimental import pallas as pl
from jax.experimental.pallas import tpu as pltpu

F32 = jnp.float32
BF16 = jnp.bfloat16

LN_EPS = 1e-5
NORM_EPS = 1e-6
SUBLANES = 8
LANES = 128
HALO = SUBLANES
GDN_ROWS = 64
VMEM_LIMIT = 60 * 1024 * 1024


def _dot(a, b):
    return jnp.dot(a.astype(BF16), b.astype(BF16), preferred_element_type=F32)


def _dot_nt(a, b):
    return lax.dot_general(a.astype(BF16), b.astype(BF16), (((1,), (1,)), ((), ())),
                           preferred_element_type=F32)


def _dot_tn(a, b):
    return lax.dot_general(a.astype(BF16), b.astype(BF16), (((0,), (0,)), ((), ())),
                           preferred_element_type=F32)


def _silu(x):
    return x * jax.nn.sigmoid(x)


def _split3(x):
    hi = x.astype(BF16)
    r = x - hi.astype(F32)
    mid = r.astype(BF16)
    lo = (r - mid.astype(F32)).astype(BF16)
    return hi, mid, lo


def _layer_norm(r, g, b):
    mu = jnp.mean(r, axis=-1, keepdims=True)
    d = r - mu
    var = jnp.mean(d * d, axis=-1, keepdims=True)
    return d * lax.rsqrt(var + LN_EPS) * g + b


def _mod_kernel(c_ref, w_ref, b_ref, o_ref):
    cs = _silu(c_ref[...])
    o_ref[0] = _dot(cs, w_ref[0]) + b_ref[0]


def _modulation(c_all, w_mod, b_mod):
    depth, d, n3 = w_mod.shape
    rows = c_all.shape[0]
    nb = d
    return pl.pallas_call(
        _mod_kernel,
        grid=(depth, n3 // nb),
        in_specs=[
            pl.BlockSpec((rows, d), lambda l, j: (0, 0)),
            pl.BlockSpec((1, d, nb), lambda l, j: (l, 0, j)),
            pl.BlockSpec((1, 1, nb), lambda l, j: (l, 0, j)),
        ],
        out_specs=pl.BlockSpec((1, rows, nb), lambda l, j: (l, 0, j)),
        out_shape=jax.ShapeDtypeStruct((depth, rows, n3), F32),
        compiler_params=pltpu.CompilerParams(
            dimension_semantics=("arbitrary", "arbitrary"), vmem_limit_bytes=VMEM_LIMIT),
        name="modulation",
    )(c_all, w_mod, b_mod.reshape(depth, 1, n3))


def _modulate(x_ref, mod_ref):
    x = x_ref[...]
    shift = mod_ref[:, 0:1, :]
    scale = mod_ref[:, 1:2, :]
    return x, x * (1.0 + scale) + shift


def _init_conv_halo(halo_scr, st_ref, width):
    halo_scr[...] = jnp.zeros(halo_scr.shape, F32)
    if st_ref is not None:
        halo_scr[:, HALO - (width - 1):HALO, :] = st_ref[...]


def _causal_conv(halo_scr, nst_ref, p, wconv_ref, width, sl):
    bb, tt, cw = p.shape
    ext = jnp.concatenate([halo_scr[:, :, sl], p], axis=1)
    halo_scr[:, :, sl] = ext[:, tt:, :]
    nst_ref[:, :, sl] = ext[:, HALO + tt - (width - 1):, :]
    e = ext.reshape(bb * (HALO + tt), cw)
    e1 = pltpu.roll(e, 1, axis=0)
    tap = lambda k: wconv_ref[width - 1 - k:width - k, sl]
    y = None
    for m in range((width + 1) // 2):
        pair = e * tap(2 * m)
        if 2 * m + 1 < width:
            pair = pair + e1 * tap(2 * m + 1)
        if m:
            pair = pltpu.roll(pair, 2 * m, axis=0)
        y = pair if y is None else y + pair
    return y.reshape(bb, HALO + tt, cw)[:, HALO:, :].reshape(bb * tt, cw)


def _residual_norm(x, out, mod_ref, lng_ref, lnb_ref, o_ref, alpha):
    bb, tt, d = o_ref.shape
    gate = mod_ref[:, 2:3, :]
    r = alpha * x.reshape(bb, tt, d) + gate * out.reshape(bb, tt, d)
    o_ref[...] = _layer_norm(r, lng_ref[...], lnb_ref[...])


def _const_spec(shape):
    zeros = (0,) * len(shape)
    return pl.BlockSpec(shape, lambda *_: zeros, pipeline_mode=pl.Buffered(1))


def _conv_layer_kernel(*refs, has_state, width, col_w, alpha):
    if has_state:
        (x_ref, mod_ref, st_ref, win_ref, wconv_ref, wout_ref, lng_ref, lnb_ref,
         o_ref, nst_ref, halo_scr, y_scr) = refs
    else:
        (x_ref, mod_ref, win_ref, wconv_ref, wout_ref, lng_ref, lnb_ref,
         o_ref, nst_ref, halo_scr, y_scr) = refs
        st_ref = None
    bb, tt, d = x_ref.shape
    rows = bb * tt
    dc = wconv_ref.shape[1]

    @pl.when(pl.program_id(1) == 0)
    def _():
        _init_conv_halo(halo_scr, st_ref, width)

    x, u = _modulate(x_ref, mod_ref)
    u = u.reshape(rows, d).astype(BF16)

    for j in range(dc // col_w):
        sl = slice(j * col_w, (j + 1) * col_w)
        pc = _dot(u, win_ref[:, dc + j * col_w:dc + (j + 1) * col_w])
        ph = _dot(u, win_ref[:, 2 * dc + j * col_w:2 * dc + (j + 1) * col_w])
        conv = _causal_conv(halo_scr, nst_ref, (pc * ph).reshape(bb, tt, col_w), wconv_ref, width, sl)
        pb = _dot(u, win_ref[:, sl])
        pz = _dot(u, win_ref[:, 3 * dc + j * col_w:3 * dc + (j + 1) * col_w])
        y_scr[:, sl] = (pb * conv * _silu(pz)).astype(BF16)

    out = _dot(y_scr[...], wout_ref[...])
    _residual_norm(x, out, mod_ref, lng_ref, lnb_ref, o_ref, alpha)


def _conv_layer(x, mod, state, w_in, w_conv, w_out, ln_g, ln_b, *, bb, tt, alpha):
    bsz, seq, d = x.shape
    width, dc = w_conv.shape
    has_state = state is not None
    rows = bb * tt
    seq_spec = lambda n: pl.BlockSpec((bb, n, d if n == 3 else dc), lambda b, t: (b, 0, 0))
    in_specs = [pl.BlockSpec((bb, tt, d), lambda b, t: (b, t, 0)), seq_spec(3)]
    args = [x, mod]
    if has_state:
        in_specs.append(seq_spec(width - 1))
        args.append(state)
    in_specs += [_const_spec(w_in.shape), _const_spec(w_conv.shape), _const_spec(w_out.shape),
                 _const_spec((1, d)), _const_spec((1, d))]
    args += [w_in, w_conv, w_out, ln_g.reshape(1, d), ln_b.reshape(1, d)]
    kern = functools.partial(_conv_layer_kernel, has_state=has_state, width=width,
                             col_w=min(dc, 256), alpha=alpha)
    return pl.pallas_call(
        kern,
        grid=(bsz // bb, seq // tt),
        in_specs=in_specs,
        out_specs=[pl.BlockSpec((bb, tt, d), lambda b, t: (b, t, 0)), seq_spec(width - 1)],
        out_shape=[jax.ShapeDtypeStruct((bsz, seq, d), F32),
                   jax.ShapeDtypeStruct((bsz, width - 1, dc), F32)],
        scratch_shapes=[pltpu.VMEM((bb, HALO, dc), F32), pltpu.VMEM((rows, dc), BF16)],
        compiler_params=pltpu.CompilerParams(
            dimension_semantics=("arbitrary", "arbitrary"), vmem_limit_bytes=VMEM_LIMIT),
        name="conv_layer",
    )(*args)


def _tri_masks(n, seq_len):
    i = lax.broadcasted_iota(jnp.int32, (n, n), 0)
    j = lax.broadcasted_iota(jnp.int32, (n, n), 1)
    same = (i // seq_len) == (j // seq_len)
    causal = same & (i >= j)
    strict = same & (i > j)
    eye = (i == j).astype(F32)
    levels = []
    s = 1
    while s < seq_len:
        levels.append(((i // (2 * s)) == (j // (2 * s))) & ((i % (2 * s)) >= s) & ((j % (2 * s)) < s))
        s *= 2
    return causal, strict, eye, levels


def _gdn_layer_kernel(*refs, has_state, width, n_heads, dk, dv, seq_len, col_w, group, alpha):
    if has_state:
        (x_ref, mod_ref, cst_ref, s0_ref, win_ref, wconv_ref, alog_ref, dtb_ref, nw_ref,
         wout_ref, lng_ref, lnb_ref, o_ref, ncst_ref, ns_ref,
         halo_scr, wtail_scr, qkv_scr, z_scr, gt_scr, gx_scr, kb_scr, u_scr, wq_scr, attn_scr,
         oh_scr) = refs
        s_scr = None
    else:
        (x_ref, mod_ref, win_ref, wconv_ref, alog_ref, dtb_ref, nw_ref,
         wout_ref, lng_ref, lnb_ref, o_ref, ncst_ref, ns_ref,
         halo_scr, wtail_scr, qkv_scr, z_scr, gt_scr, gx_scr, kb_scr, u_scr, wq_scr, attn_scr,
         oh_scr, s_scr) = refs
        cst_ref = s0_ref = None
    bb, tt, d = x_ref.shape
    rows = bb * tt
    ch = wconv_ref.shape[1]
    qk = n_heads * dk
    vw = n_heads * dv
    n = min(GDN_ROWS, rows)
    span = min(seq_len, n)
    n_seq = n // span
    n_chunks = rows // n
    heads = range(n_heads)
    q_sl = lambda h: slice(h * dk, (h + 1) * dk)
    k_sl = lambda h: slice(qk + h * dk, qk + (h + 1) * dk)
    v_sl = lambda h: slice(2 * qk + h * dv, 2 * qk + (h + 1) * dv)
    o_sl = lambda h: slice(h * dv, (h + 1) * dv)

    @pl.when(pl.program_id(1) == 0)
    def _():
        _init_conv_halo(halo_scr, cst_ref, width)
        if s_scr is not None:
            s_scr[...] = jnp.zeros(s_scr.shape, F32)

    @pl.when((pl.program_id(0) == 0) & (pl.program_id(1) == 0))
    def _():
        tail = win_ref.shape[1] - (ch + vw)
        wtail_scr[...] = jnp.zeros(wtail_scr.shape, BF16)
        wtail_scr[:, 0:tail] = win_ref[:, ch + vw:]

    x, u = _modulate(x_ref, mod_ref)
    u = u.reshape(rows, d).astype(BF16)

    ab = _dot(u, wtail_scr[...])
    lane = lax.broadcasted_iota(jnp.int32, (rows, LANES), 1)
    a = ab + dtb_ref[...]
    softplus = jnp.maximum(a, 0.0) + jnp.log(1.0 + jnp.exp(-jnp.abs(a)))
    g = jnp.where((lane >= n_heads) & (lane < 2 * n_heads),
                  -jnp.exp(alog_ref[...]) * softplus, 0.0)
    beta = jax.nn.sigmoid(ab)
    ri = lax.broadcasted_iota(jnp.int32, (rows, rows), 0)
    rj = lax.broadcasted_iota(jnp.int32, (rows, rows), 1)
    tri = (((ri // span) == (rj // span)) & (ri >= rj)).astype(BF16)
    g_hi, g_mid, g_lo = _split3(g)
    g_cum = (jnp.dot(tri, g_hi, preferred_element_type=F32)
             + jnp.dot(tri, g_mid, preferred_element_type=F32)
             + jnp.dot(tri, g_lo, preferred_element_type=F32))
    for c in range(n_chunks):
        gt_scr[c] = g_cum[c * n:(c + 1) * n, :].T
    for h in heads:
        gx_scr[:, o_sl(h)] = jnp.broadcast_to(g_cum[:, n_heads + h:n_heads + h + 1], (rows, dk))
        kb_scr[:, o_sl(h)] = jnp.broadcast_to(beta[:, h:h + 1], (rows, dk))

    for j in range(ch // col_w):
        sl = slice(j * col_w, (j + 1) * col_w)
        p = _dot(u, win_ref[:, sl]).reshape(bb, tt, col_w)
        qkv_scr[:, sl] = _silu(_causal_conv(halo_scr, ncst_ref, p, wconv_ref, width, sl))

    z_scr[...] = _silu(_dot(u, win_ref[:, ch:ch + vw]))

    for h in heads:
        t = qkv_scr[:, q_sl(h)]
        qkv_scr[:, q_sl(h)] = t * (lax.rsqrt(jnp.sum(t * t, axis=-1, keepdims=True) + NORM_EPS)
                                   * dk ** -0.5)
        t = qkv_scr[:, k_sl(h)]
        t = t * lax.rsqrt(jnp.sum(t * t, axis=-1, keepdims=True) + NORM_EPS)
        qkv_scr[:, k_sl(h)] = t
        beta_x = kb_scr[:, o_sl(h)]
        qkv_scr[:, v_sl(h)] = qkv_scr[:, v_sl(h)] * beta_x
        kb_scr[:, o_sl(h)] = t * beta_x

    causal, strict, eye, levels = _tri_masks(n, span)
    row_seq = lax.broadcasted_iota(jnp.int32, (n, 1), 0) // span

    def local_body(gi, carry):
        items = []
        for ci in range(group):
            c = gi * group + ci
            rs = pl.ds(pl.multiple_of(c * n, n), n)
            items += [(c, rs, h) for h in heads]
        decay, a_mat, t_inv = [], [], []
        for c, rs, h in items:
            g_row = gt_scr[c, n_heads + h:n_heads + h + 1, :]
            diff = gx_scr[rs, h * dv:h * dv + n] - g_row
            decay.append(jnp.where(causal, jnp.exp(jnp.where(causal, diff, 0.0)), 0.0))
        for (c, rs, h), dec in zip(items, decay):
            k = qkv_scr[rs, k_sl(h)]
            a_mat.append(jnp.where(strict, _dot_nt(kb_scr[rs, o_sl(h)], k) * dec, 0.0))
        for (c, rs, h), dec in zip(items, decay):
            attn_scr[rs, h * dv:h * dv + n] = (
                _dot_nt(qkv_scr[rs, q_sl(h)], qkv_scr[rs, k_sl(h)]) * dec)
        t_inv = [eye - jnp.where(levels[0], a, 0.0) for a in a_mat]
        for li, m in enumerate(levels[1:]):
            s = 2 << li
            if s % SUBLANES:
                y = [_dot(jnp.where(m, a, 0.0), t) for a, t in zip(a_mat, t_inv)]
                xs = [_dot(t, yy) for t, yy in zip(t_inv, y)]
                t_inv = [t - xx for t, xx in zip(t_inv, xs)]
                continue
            lo = [(b * 2 * s + s, b * 2 * s + 2 * s) for b in range(n // (2 * s))]
            take = lambda x: jnp.concatenate([x[r0:r1] for r0, r1 in lo], axis=0)
            def put(xc, rest):
                parts = []
                for b, (r0, r1) in enumerate(lo):
                    parts += [rest(r0 - s, r0), xc[b * s:(b + 1) * s]]
                return jnp.concatenate(parts, axis=0)
            zeros = jnp.zeros((s, n), F32)
            y = [put(_dot(take(jnp.where(m, a, 0.0)), t), lambda r0, r1: zeros)
                 for a, t in zip(a_mat, t_inv)]
            xs = [_dot(take(t), yy) for t, yy in zip(t_inv, y)]
            t_inv = [put(take(t) - xx, lambda r0, r1, t=t: t[r0:r1]) for t, xx in zip(t_inv, xs)]
        for (c, rs, h), t in zip(items, t_inv):
            e_x = jnp.exp(gx_scr[rs, o_sl(h)])
            rhs = jnp.concatenate([qkv_scr[rs, v_sl(h)], kb_scr[rs, o_sl(h)] * e_x], axis=1)
            sol = _dot(t, rhs)
            u_scr[rs, o_sl(h)] = sol[:, :dv]
            wq_scr[c, 0:n, o_sl(h)] = sol[:, dv:]
            wq_scr[c, n:2 * n, o_sl(h)] = qkv_scr[rs, q_sl(h)] * e_x
        return carry

    lax.fori_loop(0, n_chunks // group, local_body, 0)

    chunks_per_seq = tt // n if not has_state else n_chunks

    def state_body(c, carry):
        if has_state:
            lanes = [(0, s) for s in range(n_seq)]
            rows_of = lambda q: slice(c * n, (c + 1) * n)
            chunk_of = lambda q: c
        else:
            lanes = [(q, 0) for q in range(bb)]
            rows_of = lambda q: slice(q * tt + c * n, q * tt + (c + 1) * n)
            chunk_of = lambda q: q * chunks_per_seq + c
        groups = sorted({q for q, _ in lanes})
        masks = [None if n_seq == 1 else (row_seq == s) for s in range(n_seq)]
        if has_state:
            s_prev = {(h, q, s): s0_ref[c * n_seq + s, h] for h in heads for q, s in lanes}
        else:
            s_prev = {(h, q, s): s_scr[q, h] for h in heads for q, s in lanes}
        def own_rows(q, s, h):
            if n_seq == 1:
                return wq_scr[chunk_of(q), :, o_sl(h)]
            return jnp.concatenate(
                [wq_scr[chunk_of(q), s * span:(s + 1) * span, o_sl(h)],
                 wq_scr[chunk_of(q), n + s * span:n + (s + 1) * span, o_sl(h)]], axis=0)
        ws = {(h, q, s): _dot(own_rows(q, s, h), s_prev[(h, q, s)])
              for h in heads for q, s in lanes}
        v_new, v_dec, o_part, g_last = {}, {}, {}, {}
        for h in heads:
            for q in groups:
                w_s = jnp.concatenate([ws[(h, q, s)][:span] for s in range(n_seq)], axis=0)
                v_new[h, q] = u_scr[rows_of(q), o_sl(h)] - w_s
                o_part[h, q] = jnp.concatenate(
                    [ws[(h, q, s)][span:] for s in range(n_seq)], axis=0)
                gx = gx_scr[rows_of(q), o_sl(h)]
                for s in range(n_seq):
                    g_last[h, q, s] = gx[(s + 1) * span - 1:(s + 1) * span, :]
                g_end = jnp.concatenate(
                    [jnp.broadcast_to(g_last[h, q, s], (span, dv)) for s in range(n_seq)], axis=0)
                v_dec[h, q] = v_new[h, q] * jnp.exp(g_end - gx)
        for h in heads:
            for q in groups:
                oh_scr[rows_of(q), o_sl(h)] = o_part[h, q] + _dot(
                    attn_scr[rows_of(q), h * dv:h * dv + n], v_new[h, q])
        for h in heads:
            for q in groups:
                vs = v_dec[h, q] if n_seq == 1 else jnp.concatenate(
                    [jnp.where(masks[s], v_dec[h, q], 0.0) for s in range(n_seq)], axis=1)
                ktv = _dot_tn(qkv_scr[rows_of(q), k_sl(h)], vs)
                for s in range(n_seq):
                    s_new = (s_prev[(h, q, s)] * jnp.exp(g_last[h, q, s])
                             + ktv[:, s * dv:(s + 1) * dv])
                    if has_state:
                        ns_ref[c * n_seq + s, h] = s_new
                    else:
                        s_scr[q, h] = s_new
        return carry

    for c in range(chunks_per_seq):
        state_body(c, 0)
    if not has_state:
        ns_ref[...] = s_scr[...]

    for h in heads:
        o = oh_scr[:, o_sl(h)]
        o = o * lax.rsqrt(jnp.mean(o * o, axis=-1, keepdims=True) + NORM_EPS) * nw_ref[...]
        oh_scr[:, o_sl(h)] = o * z_scr[:, o_sl(h)]
    out = _dot(oh_scr[...], wout_ref[...])
    _residual_norm(x, out, mod_ref, lng_ref, lnb_ref, o_ref, alpha)


def _gdn_layer(x, mod, conv_state, ssm_state, w_in, w_conv, a_log, dt_bias, norm_w, w_out,
               ln_g, ln_b, *, bb, tt, alpha):
    bsz, seq, d = x.shape
    width, ch = w_conv.shape
    n_heads = a_log.shape[0]
    vw = w_out.shape[0]
    dv = vw // n_heads
    dk = (ch - vw) // 2 // n_heads
    qk = n_heads * dk
    has_state = conv_state is not None
    rows = bb * tt
    seq_len = min(seq, GDN_ROWS)
    n = min(GDN_ROWS, rows)
    n_chunks = rows // n
    assert dk == dv == LANES, "one head per lane tile"
    assert rows % n == 0 and n % seq_len == 0
    assert has_state == (tt == seq), "a carried state needs whole sequences per block"
    group = max(g for g in (8, 4, 2, 1) if n_chunks % g == 0)

    assert w_in.shape[1] == ch + vw + 2 * n_heads and 2 * n_heads <= LANES
    pad = lambda vec: jnp.zeros((1, LANES), F32).at[0, n_heads:2 * n_heads].set(vec)
    in_specs = [pl.BlockSpec((bb, tt, d), lambda b, t: (b, t, 0)),
                pl.BlockSpec((bb, 3, d), lambda b, t: (b, 0, 0))]
    args = [x, mod]
    if has_state:
        in_specs += [pl.BlockSpec((bb, width - 1, ch), lambda b, t: (b, 0, 0)),
                     pl.BlockSpec((bb, n_heads, dk, dv), lambda b, t: (b, 0, 0, 0))]
        args += [conv_state, ssm_state]
    in_specs += [_const_spec(w_in.shape), _const_spec(w_conv.shape),
                 _const_spec((1, LANES)), _const_spec((1, LANES)), _const_spec((1, dv)),
                 _const_spec(w_out.shape), _const_spec((1, d)), _const_spec((1, d))]
    args += [w_in, w_conv, pad(a_log), pad(dt_bias), norm_w.reshape(1, dv), w_out,
             ln_g.reshape(1, d), ln_b.reshape(1, d)]
    scratch = [pltpu.VMEM((bb, HALO, ch), F32),
               pltpu.VMEM((d, LANES), BF16),
               pltpu.VMEM((rows, ch), F32),
               pltpu.VMEM((rows, vw), F32),
               pltpu.VMEM((n_chunks, LANES, n), F32),
               pltpu.VMEM((rows, vw), F32),
               pltpu.VMEM((rows, qk), F32),
               pltpu.VMEM((rows, vw), F32),
               pltpu.VMEM((n_chunks, 2 * n, qk), F32),
               pltpu.VMEM((rows, vw), F32),
               pltpu.VMEM((rows, vw), F32)]
    if not has_state:
        scratch.append(pltpu.VMEM((bb, n_heads, dk, dv), F32))
    kern = functools.partial(_gdn_layer_kernel, has_state=has_state, width=width, n_heads=n_heads,
                             dk=dk, dv=dv, seq_len=seq_len, col_w=min(ch, 512), group=group,
                             alpha=alpha)
    return pl.pallas_call(
        kern,
        grid=(bsz // bb, seq // tt),
        in_specs=in_specs,
        out_specs=[pl.BlockSpec((bb, tt, d), lambda b, t: (b, t, 0)),
                   pl.BlockSpec((bb, width - 1, ch), lambda b, t: (b, 0, 0)),
                   pl.BlockSpec((bb, n_heads, dk, dv), lambda b, t: (b, 0, 0, 0))],
        out_shape=[jax.ShapeDtypeStruct((bsz, seq, d), F32),
                   jax.ShapeDtypeStruct((bsz, width - 1, ch), F32),
                   jax.ShapeDtypeStruct((bsz, n_heads, dk, dv), F32)],
        scratch_shapes=scratch,
        compiler_params=pltpu.CompilerParams(
            dimension_semantics=("arbitrary", "arbitrary"), vmem_limit_bytes=VMEM_LIMIT),
        name="gdn_layer",
    )(*args)


def _block_rows(bsz, seq, target, long_seqs=1):
    if seq * long_seqs >= target:
        return long_seqs, target // long_seqs
    return max(1, min(bsz, target // seq)), seq


def _trunk(x, mod, conv_a, conv_b, ssm_b, ln_g, ln_b, wa, wb, alpha):
    bsz, seq, _ = x.shape
    new_a, new_cb, new_s = [], [], []
    depth = mod.shape[0]
    for l in range(depth):
        i = l // 2
        if l % 2 == 0:
            bb, tt = _block_rows(bsz, seq, 512)
            x, nb = _conv_layer(x, mod[l], None if conv_a is None else conv_a[i],
                                wa["in"][i], wa["conv"][i], wa["out"][i], ln_g[l], ln_b[l],
                                bb=bb, tt=tt, alpha=alpha)
            new_a.append(nb)
        else:
            bb, tt = _block_rows(bsz, seq, *((512, 2) if conv_b is None else (2 * GDN_ROWS,)))
            x, nb, s = _gdn_layer(x, mod[l], None if conv_b is None else conv_b[i],
                                  None if ssm_b is None else ssm_b[i],
                                  wb["in"][i], wb["conv"][i], wb["a_log"][i],
                                  wb["dt_bias"][i], wb["norm"][i], wb["out"][i], ln_g[l], ln_b[l],
                                  bb=bb, tt=tt, alpha=alpha)
            new_cb.append(nb)
            new_s.append(s)
    return x, jnp.stack(new_a), jnp.stack(new_cb), jnp.stack(new_s)


def kernel(x_prompt, x_sample, state_conv_a, state_conv_b, state_ssm_b, c_prompt, c_sample, w_mod, b_mod, ln_g, ln_b, wa_in, wa_conv, wa_out, wb_in, wb_conv, wb_a_log, wb_dt_bias, wb_norm, wb_out):
    depth, d, _ = w_mod.shape
    alpha = (2 * depth) ** 0.25
    bp = x_prompt.shape[0]

    mod = _modulation(jnp.concatenate([c_prompt, c_sample], axis=0), w_mod, b_mod)
    mod = mod.reshape(depth, -1, 3, d)
    mod_p, mod_s = mod[:, :bp], mod[:, bp:]

    wa = {"in": wa_in.astype(BF16), "conv": wa_conv, "out": wa_out.astype(BF16)}
    wb = {"in": wb_in.astype(BF16), "conv": wb_conv,
          "a_log": wb_a_log, "dt_bias": wb_dt_bias, "norm": wb_norm, "out": wb_out.astype(BF16)}

    y_p, ca_p, cb_p, s_p = _trunk(x_prompt, mod_p, None, None, None, ln_g, ln_b, wa, wb, alpha)
    y_s, ca_s, cb_s, s_s = _trunk(x_sample, mod_s, state_conv_a, state_conv_b, state_ssm_b,
                                  ln_g, ln_b, wa, wb, alpha)
    return (y_p, y_s, ca_p, cb_p, s_p, ca_s, cb_s, s_s)
```

```python
import functools

import jax
import jax.numpy as jnp
from jax import lax
from jax.experimental import pallas as pl
from jax.experimental.pallas import tpu as pltpu

F32 = jnp.float32
BF16 = jnp.bfloat16

LN_EPS = 1e-5
NORM_EPS = 1e-6
SUBLANES = 8
LANES = 128
HALO = SUBLANES
GDN_ROWS = 64
VMEM_LIMIT = 60 * 1024 * 1024


def _dot(a, b):
    return jnp.dot(a.astype(BF16), b.astype(BF16), preferred_element_type=F32)


def _dot_nt(a, b):
    return lax.dot_general(a.astype(BF16), b.astype(BF16), (((1,), (1,)), ((), ())),
                           preferred_element_type=F32)


def _dot_tn(a, b):
    return lax.dot_general(a.astype(BF16), b.astype(BF16), (((0,), (0,)), ((), ())),
                           preferred_element_type=F32)


def _silu(x):
    return x * jax.nn.sigmoid(x)


def _split3(x):
    hi = x.astype(BF16)
    r = x - hi.astype(F32)
    mid = r.astype(BF16)
    lo = (r - mid.astype(F32)).astype(BF16)
    return hi, mid, lo


def _layer_norm(r, g, b):
    mu = jnp.mean(r, axis=-1, keepdims=True)
    d = r - mu
    var = jnp.mean(d * d, axis=-1, keepdims=True)
    return d * lax.rsqrt(var + LN_EPS) * g + b


def _mod_kernel(c_ref, w_ref, b_ref, o_ref):
    cs = _silu(c_ref[...])
    o_ref[0, 0] = _dot(cs, w_ref[0]) + b_ref[0]


def _modulation(c_all, w_mod, b_mod):
    depth, d, n3 = w_mod.shape
    rows = c_all.shape[0]
    return pl.pallas_call(
        _mod_kernel,
        grid=(depth, n3 // d),
        in_specs=[
            pl.BlockSpec((rows, d), lambda l, j: (0, 0)),
            pl.BlockSpec((1, d, d), lambda l, j: (l, 0, j)),
            pl.BlockSpec((1, 1, d), lambda l, j: (l, 0, j)),
        ],
        out_specs=pl.BlockSpec((1, 1, rows, d), lambda l, j: (l, j, 0, 0)),
        out_shape=jax.ShapeDtypeStruct((depth, n3 // d, rows, d), F32),
        compiler_params=pltpu.CompilerParams(
            dimension_semantics=("arbitrary", "arbitrary"), vmem_limit_bytes=VMEM_LIMIT),
        name="modulation",
    )(c_all, w_mod, b_mod.reshape(depth, 1, n3))


def _mod_rows(mod_ref, part, bb, pick_rows):
    if not pick_rows:
        return [mod_ref[0, part, q:q + 1, :] for q in range(bb)]
    rows = mod_ref[0, part]
    row_id = lax.broadcasted_iota(jnp.int32, (rows.shape[0], 1), 0)
    first = pl.program_id(0) * bb
    return [jnp.sum(jnp.where(row_id == first + q, rows, 0.0), axis=0, keepdims=True)
            for q in range(bb)]


def _modulate(x_ref, mod_ref, pick_rows):
    bb = x_ref.shape[0]
    shift = _mod_rows(mod_ref, 0, bb, pick_rows)
    scale = _mod_rows(mod_ref, 1, bb, pick_rows)
    xs = [x_ref[q] for q in range(bb)]
    u = [xs[q] * (1.0 + scale[q]) + shift[q] for q in range(bb)]
    return jnp.concatenate(xs, axis=0), jnp.concatenate(u, axis=0)


def _init_conv_halo(halo_scr, st_ref, width):
    halo_scr[...] = jnp.zeros(halo_scr.shape, F32)
    if st_ref is not None:
        halo_scr[:, HALO - (width - 1):HALO, :] = st_ref[...]


def _causal_conv(halo_scr, nst_ref, p, wconv_ref, width, sl):
    bb, tt, cw = p.shape
    ext = jnp.concatenate([halo_scr[:, :, sl], p], axis=1)
    halo_scr[:, :, sl] = ext[:, tt:, :]
    nst_ref[:, :, sl] = ext[:, HALO + tt - (width - 1):, :]
    e = ext.reshape(bb * (HALO + tt), cw)
    e1 = pltpu.roll(e, 1, axis=0)
    tap = lambda k: wconv_ref[width - 1 - k:width - k, sl]
    y = None
    for m in range((width + 1) // 2):
        pair = e * tap(2 * m)
        if 2 * m + 1 < width:
            pair = pair + e1 * tap(2 * m + 1)
        if m:
            pair = pltpu.roll(pair, 2 * m, axis=0)
        y = pair if y is None else y + pair
    return y.reshape(bb, HALO + tt, cw)[:, HALO:, :].reshape(bb * tt, cw)


def _residual_norm(x, out, mod_ref, pick_rows, lng_ref, lnb_ref, o_ref, alpha):
    bb, tt, d = o_ref.shape
    gate = _mod_rows(mod_ref, 2, bb, pick_rows)
    r = jnp.concatenate([alpha * x[q * tt:(q + 1) * tt] + gate[q] * out[q * tt:(q + 1) * tt]
                         for q in range(bb)], axis=0)
    o_ref[...] = _layer_norm(r, lng_ref[...], lnb_ref[...]).reshape(bb, tt, d)


def _const_spec(shape):
    zeros = (0,) * len(shape)
    return pl.BlockSpec(shape, lambda *_: zeros, pipeline_mode=pl.Buffered(1))


def _conv_layer_kernel(*refs, has_state, width, col_w, alpha):
    if has_state:
        (x_ref, mod_ref, st_ref, win_ref, wconv_ref, wout_ref, lng_ref, lnb_ref,
         o_ref, nst_ref, halo_scr, y_scr) = refs
    else:
        (x_ref, mod_ref, win_ref, wconv_ref, wout_ref, lng_ref, lnb_ref,
         o_ref, nst_ref, halo_scr, y_scr) = refs
        st_ref = None
    bb, tt, d = x_ref.shape
    rows = bb * tt
    dc = wconv_ref.shape[1]

    @pl.when(pl.program_id(1) == 0)
    def _():
        _init_conv_halo(halo_scr, st_ref, width)

    x, u = _modulate(x_ref, mod_ref, not has_state)
    u = u.astype(BF16)

    for j in range(dc // col_w):
        sl = slice(j * col_w, (j + 1) * col_w)
        pc = _dot(u, win_ref[:, dc + j * col_w:dc + (j + 1) * col_w])
        ph = _dot(u, win_ref[:, 2 * dc + j * col_w:2 * dc + (j + 1) * col_w])
        conv = _causal_conv(halo_scr, nst_ref, (pc * ph).reshape(bb, tt, col_w), wconv_ref, width, sl)
        pb = _dot(u, win_ref[:, sl])
        pz = _dot(u, win_ref[:, 3 * dc + j * col_w:3 * dc + (j + 1) * col_w])
        y_scr[:, sl] = (pb * conv * _silu(pz)).astype(BF16)

    out = _dot(y_scr[...], wout_ref[...])
    _residual_norm(x, out, mod_ref, not has_state, lng_ref, lnb_ref, o_ref, alpha)


def _mod_spec(mod, layer, row0, bsz, bb, whole_batch):
    m = bsz if whole_batch else bb
    assert row0 % m == 0 and m % SUBLANES == 0
    if whole_batch:
        return pl.BlockSpec((1, 3, m, mod.shape[3]), lambda b, t: (layer, 0, row0 // m, 0))
    return pl.BlockSpec((1, 3, m, mod.shape[3]), lambda b, t: (layer, 0, row0 // m + b, 0))


def _conv_layer(x, mod, layer, row0, state, w_in, w_conv, w_out, ln_g, ln_b, *, bb, tt, alpha):
    bsz, seq, d = x.shape
    width, dc = w_conv.shape
    has_state = state is not None
    rows = bb * tt
    seq_spec = lambda n: pl.BlockSpec((bb, n, dc), lambda b, t: (b, 0, 0))
    in_specs = [pl.BlockSpec((bb, tt, d), lambda b, t: (b, t, 0)),
                _mod_spec(mod, layer, row0, bsz, bb, not has_state)]
    args = [x, mod]
    if has_state:
        in_specs.append(seq_spec(width - 1))
        args.append(state)
    in_specs += [_const_spec(w_in.shape), _const_spec(w_conv.shape), _const_spec(w_out.shape),
                 _const_spec((1, d)), _const_spec((1, d))]
    args += [w_in, w_conv, w_out, ln_g.reshape(1, d), ln_b.reshape(1, d)]
    kern = functools.partial(_conv_layer_kernel, has_state=has_state, width=width,
                             col_w=min(dc, 256), alpha=alpha)
    return pl.pallas_call(
        kern,
        grid=(bsz // bb, seq // tt),
        in_specs=in_specs,
        out_specs=[pl.BlockSpec((bb, tt, d), lambda b, t: (b, t, 0)), seq_spec(width - 1)],
        out_shape=[jax.ShapeDtypeStruct((bsz, seq, d), F32),
                   jax.ShapeDtypeStruct((bsz, width - 1, dc), F32)],
        scratch_shapes=[pltpu.VMEM((bb, HALO, dc), F32), pltpu.VMEM((rows, dc), BF16)],
        compiler_params=pltpu.CompilerParams(
            dimension_semantics=("arbitrary", "arbitrary"), vmem_limit_bytes=VMEM_LIMIT),
        name="conv_layer",
    )(*args)


def _tri_masks(n, seq_len):
    i = lax.broadcasted_iota(jnp.int32, (n, n), 0)
    j = lax.broadcasted_iota(jnp.int32, (n, n), 1)
    same = (i // seq_len) == (j // seq_len)
    causal = same & (i >= j)
    strict = same & (i > j)
    eye = (i == j).astype(F32)
    levels = []
    s = 1
    while s < seq_len:
        levels.append(((i // (2 * s)) == (j // (2 * s))) & ((i % (2 * s)) >= s) & ((j % (2 * s)) < s))
        s *= 2
    return causal, strict, eye, levels


def _gdn_layer_kernel(*refs, has_state, width, n_heads, dk, dv, seq_len, col_w, group, alpha):
    if has_state:
        (x_ref, mod_ref, cst_ref, s0_ref, win_ref, wtail_ref, wconv_ref, alog_ref, dtb_ref, nw_ref,
         wout_ref, lng_ref, lnb_ref, o_ref, ncst_ref, ns_ref,
         halo_scr, wtail_scr, qkv_scr, z_scr, gt_scr, gx_scr, kb_scr, u_scr, wq_scr, attn_scr,
         oh_scr) = refs
        s_scr = None
    else:
        (x_ref, mod_ref, win_ref, wtail_ref, wconv_ref, alog_ref, dtb_ref, nw_ref,
         wout_ref, lng_ref, lnb_ref, o_ref, ncst_ref, ns_ref,
         halo_scr, wtail_scr, qkv_scr, z_scr, gt_scr, gx_scr, kb_scr, u_scr, wq_scr, attn_scr,
         oh_scr, s_scr) = refs
        cst_ref = s0_ref = None
    bb, tt, d = x_ref.shape
    rows = bb * tt
    ch = wconv_ref.shape[1]
    qk = n_heads * dk
    vw = n_heads * dv
    n = min(GDN_ROWS, rows)
    span = min(seq_len, n)
    n_seq = n // span
    n_chunks = rows // n
    heads = range(n_heads)
    q_sl = lambda h: slice(h * dk, (h + 1) * dk)
    k_sl = lambda h: slice(qk + h * dk, qk + (h + 1) * dk)
    v_sl = lambda h: slice(2 * qk + h * dv, 2 * qk + (h + 1) * dv)
    o_sl = lambda h: slice(h * dv, (h + 1) * dv)

    @pl.when(pl.program_id(1) == 0)
    def _():
        _init_conv_halo(halo_scr, cst_ref, width)
        if s_scr is not None:
            s_scr[...] = jnp.zeros(s_scr.shape, F32)

    @pl.when((pl.program_id(0) == 0) & (pl.program_id(1) == 0))
    def _():
        wtail_scr[...] = jnp.zeros(wtail_scr.shape, BF16)
        wtail_scr[:, 0:wtail_ref.shape[1]] = wtail_ref[...]

    x, u = _modulate(x_ref, mod_ref, not has_state)
    u = u.astype(BF16)

    ab = _dot(u, wtail_scr[...])
    lane = lax.broadcasted_iota(jnp.int32, (rows, LANES), 1)
    a = ab + dtb_ref[...]
    softplus = jnp.maximum(a, 0.0) + jnp.log(1.0 + jnp.exp(-jnp.abs(a)))
    g = jnp.where((lane >= n_heads) & (lane < 2 * n_heads),
                  -jnp.exp(alog_ref[...]) * softplus, 0.0)
    beta = jax.nn.sigmoid(ab)
    ri = lax.broadcasted_iota(jnp.int32, (rows, rows), 0)
    rj = lax.broadcasted_iota(jnp.int32, (rows, rows), 1)
    tri = (((ri // span) == (rj // span)) & (ri >= rj)).astype(BF16)
    g_hi, g_mid, g_lo = _split3(g)
    g_cum = (jnp.dot(tri, g_hi, preferred_element_type=F32)
             + jnp.dot(tri, g_mid, preferred_element_type=F32)
             + jnp.dot(tri, g_lo, preferred_element_type=F32))
    for c in range(n_chunks):
        gt_scr[c] = g_cum[c * n:(c + 1) * n, :].T
    for h in heads:
        gx_scr[:, o_sl(h)] = jnp.broadcast_to(g_cum[:, n_heads + h:n_heads + h + 1], (rows, dk))
        kb_scr[:, o_sl(h)] = jnp.broadcast_to(beta[:, h:h + 1], (rows, dk))

    for j in range(ch // col_w):
        sl = slice(j * col_w, (j + 1) * col_w)
        p = _dot(u, win_ref[:, sl]).reshape(bb, tt, col_w)
        qkv_scr[:, sl] = _silu(_causal_conv(halo_scr, ncst_ref, p, wconv_ref, width, sl))

    z_scr[...] = _silu(_dot(u, win_ref[:, ch:ch + vw]))

    for h in heads:
        t = qkv_scr[:, q_sl(h)]
        qkv_scr[:, q_sl(h)] = t * (lax.rsqrt(jnp.sum(t * t, axis=-1, keepdims=True) + NORM_EPS)
                                   * dk ** -0.5)
        t = qkv_scr[:, k_sl(h)]
        t = t * lax.rsqrt(jnp.sum(t * t, axis=-1, keepdims=True) + NORM_EPS)
        qkv_scr[:, k_sl(h)] = t
        beta_x = kb_scr[:, o_sl(h)]
        qkv_scr[:, v_sl(h)] = qkv_scr[:, v_sl(h)] * beta_x
        kb_scr[:, o_sl(h)] = t * beta_x

    causal, strict, eye, levels = _tri_masks(n, span)
    row_seq = lax.broadcasted_iota(jnp.int32, (n, 1), 0) // span

    def local_body(gi, carry):
        items = []
        for ci in range(group):
            c = gi * group + ci
            rs = pl.ds(pl.multiple_of(c * n, n), n)
            items += [(c, rs, h) for h in heads]
        decay, a_mat, t_inv = [], [], []
        for c, rs, h in items:
            g_row = gt_scr[c, n_heads + h:n_heads + h + 1, :]
            diff = gx_scr[rs, h * dv:h * dv + n] - g_row
            decay.append(jnp.where(causal, jnp.exp(jnp.where(causal, diff, 0.0)), 0.0))
        for (c, rs, h), dec in zip(items, decay):
            k = qkv_scr[rs, k_sl(h)]
            a_mat.append(jnp.where(strict, _dot_nt(kb_scr[rs, o_sl(h)], k) * dec, 0.0))
        for (c, rs, h), dec in zip(items, decay):
            attn_scr[rs, h * dv:h * dv + n] = (
                _dot_nt(qkv_scr[rs, q_sl(h)], qkv_scr[rs, k_sl(h)]) * dec)
        t_inv = [eye - jnp.where(levels[0], a, 0.0) for a in a_mat]
        for li, m in enumerate(levels[1:]):
            s = 2 << li
            if s % SUBLANES:
                y = [_dot(jnp.where(m, a, 0.0), t) for a, t in zip(a_mat, t_inv)]
                xs = [_dot(t, yy) for t, yy in zip(t_inv, y)]
                t_inv = [t - xx for t, xx in zip(t_inv, xs)]
                continue
            lo = [(b * 2 * s + s, b * 2 * s + 2 * s) for b in range(n // (2 * s))]
            take = lambda x: jnp.concatenate([x[r0:r1] for r0, r1 in lo], axis=0)
            def put(xc, rest):
                parts = []
                for b, (r0, r1) in enumerate(lo):
                    parts += [rest(r0 - s, r0), xc[b * s:(b + 1) * s]]
                return jnp.concatenate(parts, axis=0)
            zeros = jnp.zeros((s, n), F32)
            y = [put(_dot(take(jnp.where(m, a, 0.0)), t), lambda r0, r1: zeros)
                 for a, t in zip(a_mat, t_inv)]
            xs = [_dot(take(t), yy) for t, yy in zip(t_inv, y)]
            t_inv = [put(take(t) - xx, lambda r0, r1, t=t: t[r0:r1]) for t, xx in zip(t_inv, xs)]
        for (c, rs, h), t in zip(items, t_inv):
            e_x = jnp.exp(gx_scr[rs, o_sl(h)])
            rhs = jnp.concatenate([qkv_scr[rs, v_sl(h)], kb_scr[rs, o_sl(h)] * e_x], axis=1)
            sol = _dot(t, rhs)
            u_scr[rs, o_sl(h)] = sol[:, :dv]
            wq_scr[c, 0:n, o_sl(h)] = sol[:, dv:]
            wq_scr[c, n:2 * n, o_sl(h)] = qkv_scr[rs, q_sl(h)] * e_x
        return carry

    lax.fori_loop(0, n_chunks // group, local_body, 0)

    chunks_per_seq = tt // n if not has_state else n_chunks

    def state_body(c, carry):
        if has_state:
            lanes = [(0, s) for s in range(n_seq)]
            rows_of = lambda q: slice(c * n, (c + 1) * n)
            chunk_of = lambda q: c
        else:
            lanes = [(q, 0) for q in range(bb)]
            rows_of = lambda q: slice(q * tt + c * n, q * tt + (c + 1) * n)
            chunk_of = lambda q: q * chunks_per_seq + c
        groups = sorted({q for q, _ in lanes})
        masks = [None if n_seq == 1 else (row_seq == s) for s in range(n_seq)]
        if has_state:
            s_prev = {(h, q, s): s0_ref[c * n_seq + s, h] for h in heads for q, s in lanes}
        else:
            s_prev = {(h, q, s): s_scr[q, h] for h in heads for q, s in lanes}
        def own_rows(q, s, h):
            if n_seq == 1:
                return wq_scr[chunk_of(q), :, o_sl(h)]
            return jnp.concatenate(
                [wq_scr[chunk_of(q), s * span:(s + 1) * span, o_sl(h)],
                 wq_scr[chunk_of(q), n + s * span:n + (s + 1) * span, o_sl(h)]], axis=0)
        ws = {(h, q, s): _dot(own_rows(q, s, h), s_prev[(h, q, s)])
              for h in heads for q, s in lanes}
        v_new, v_dec, o_part, g_last = {}, {}, {}, {}
        for h in heads:
            for q in groups:
                w_s = jnp.concatenate([ws[(h, q, s)][:span] for s in range(n_seq)], axis=0)
                v_new[h, q] = u_scr[rows_of(q), o_sl(h)] - w_s
                o_part[h, q] = jnp.concatenate(
                    [ws[(h, q, s)][span:] for s in range(n_seq)], axis=0)
                gx = gx_scr[rows_of(q), o_sl(h)]
                for s in range(n_seq):
                    g_last[h, q, s] = gx[(s + 1) * span - 1:(s + 1) * span, :]
                g_end = jnp.concatenate(
                    [jnp.broadcast_to(g_last[h, q, s], (span, dv)) for s in range(n_seq)], axis=0)
                v_dec[h, q] = v_new[h, q] * jnp.exp(g_end - gx)
        for h in heads:
            for q in groups:
                oh_scr[rows_of(q), o_sl(h)] = o_part[h, q] + _dot(
                    attn_scr[rows_of(q), h * dv:h * dv + n], v_new[h, q])
        for h in heads:
            for q in groups:
                vs = v_dec[h, q] if n_seq == 1 else jnp.concatenate(
                    [jnp.where(masks[s], v_dec[h, q], 0.0) for s in range(n_seq)], axis=1)
                ktv = _dot_tn(qkv_scr[rows_of(q), k_sl(h)], vs)
                for s in range(n_seq):
                    s_new = (s_prev[(h, q, s)] * jnp.exp(g_last[h, q, s])
                             + ktv[:, s * dv:(s + 1) * dv])
                    if has_state:
                        ns_ref[c * n_seq + s, h] = s_new
                    else:
                        s_scr[q, h] = s_new
        return carry

    for c in range(chunks_per_seq):
        state_body(c, 0)
    if not has_state:
        ns_ref[...] = s_scr[...]

    for h in heads:
        o = oh_scr[:, o_sl(h)]
        o = o * lax.rsqrt(jnp.mean(o * o, axis=-1, keepdims=True) + NORM_EPS) * nw_ref[...]
        oh_scr[:, o_sl(h)] = o * z_scr[:, o_sl(h)]
    out = _dot(oh_scr[...], wout_ref[...])
    _residual_norm(x, out, mod_ref, not has_state, lng_ref, lnb_ref, o_ref, alpha)


def _gdn_layer(x, mod, layer, row0, conv_state, ssm_state, w_in, w_tail, w_conv, a_log, dt_bias,
               norm_w, w_out,
               ln_g, ln_b, *, bb, tt, alpha):
    bsz, seq, d = x.shape
    width, ch = w_conv.shape
    n_heads = a_log.shape[0]
    vw = w_out.shape[0]
    dv = vw // n_heads
    dk = (ch - vw) // 2 // n_heads
    qk = n_heads * dk
    has_state = conv_state is not None
    rows = bb * tt
    seq_len = min(seq, GDN_ROWS)
    n = min(GDN_ROWS, rows)
    n_chunks = rows // n
    assert dk == dv == LANES, "one head per lane tile"
    assert rows % n == 0 and n % seq_len == 0
    assert has_state == (tt == seq), "a carried state needs whole sequences per block"
    group = max(g for g in (8, 4, 2, 1) if n_chunks % g == 0)

    assert w_in.shape[1] == ch + vw and w_tail.shape[1] == 2 * n_heads <= LANES
    pad = lambda vec: jnp.zeros((1, LANES), F32).at[0, n_heads:2 * n_heads].set(vec)
    in_specs = [pl.BlockSpec((bb, tt, d), lambda b, t: (b, t, 0)),
                _mod_spec(mod, layer, row0, bsz, bb, not has_state)]
    args = [x, mod]
    if has_state:
        in_specs += [pl.BlockSpec((bb, width - 1, ch), lambda b, t: (b, 0, 0)),
                     pl.BlockSpec((bb, n_heads, dk, dv), lambda b, t: (b, 0, 0, 0))]
        args += [conv_state, ssm_state]
    in_specs += [_const_spec(w_in.shape), _const_spec(w_tail.shape), _const_spec(w_conv.shape),
                 _const_spec((1, LANES)), _const_spec((1, LANES)), _const_spec((1, dv)),
                 _const_spec(w_out.shape), _const_spec((1, d)), _const_spec((1, d))]
    args += [w_in, w_tail, w_conv, pad(a_log), pad(dt_bias), norm_w.reshape(1, dv), w_out,
             ln_g.reshape(1, d), ln_b.reshape(1, d)]
    scratch = [pltpu.VMEM((bb, HALO, ch), F32),
               pltpu.VMEM((d, LANES), BF16),
               pltpu.VMEM((rows, ch), F32),
               pltpu.VMEM((rows, vw), F32),
               pltpu.VMEM((n_chunks, LANES, n), F32),
               pltpu.VMEM((rows, vw), F32),
               pltpu.VMEM((rows, qk), F32),
               pltpu.VMEM((rows, vw), F32),
               pltpu.VMEM((n_chunks, 2 * n, qk), F32),
               pltpu.VMEM((rows, vw), F32),
               pltpu.VMEM((rows, vw), F32)]
    if not has_state:
        scratch.append(pltpu.VMEM((bb, n_heads, dk, dv), F32))
    kern = functools.partial(_gdn_layer_kernel, has_state=has_state, width=width, n_heads=n_heads,
                             dk=dk, dv=dv, seq_len=seq_len, col_w=min(ch, 512), group=group,
                             alpha=alpha)
    return pl.pallas_call(
        kern,
        grid=(bsz // bb, seq // tt),
        in_specs=in_specs,
        out_specs=[pl.BlockSpec((bb, tt, d), lambda b, t: (b, t, 0)),
                   pl.BlockSpec((bb, width - 1, ch), lambda b, t: (b, 0, 0)),
                   pl.BlockSpec((bb, n_heads, dk, dv), lambda b, t: (b, 0, 0, 0))],
        out_shape=[jax.ShapeDtypeStruct((bsz, seq, d), F32),
                   jax.ShapeDtypeStruct((bsz, width - 1, ch), F32),
                   jax.ShapeDtypeStruct((bsz, n_heads, dk, dv), F32)],
        scratch_shapes=scratch,
        compiler_params=pltpu.CompilerParams(
            dimension_semantics=("arbitrary", "arbitrary"), vmem_limit_bytes=VMEM_LIMIT),
        name="gdn_layer",
    )(*args)


def _block_rows(bsz, seq, target, long_seqs=1):
    if seq * long_seqs >= target:
        return long_seqs, target // long_seqs
    return max(1, min(bsz, target // seq)), seq


def _trunk(x, mod, row0, conv_a, conv_b, ssm_b, ln_g, ln_b, wa, wb, alpha):
    bsz, seq, _ = x.shape
    new_a, new_cb, new_s = [], [], []
    depth = mod.shape[0]
    for l in range(depth):
        i = l // 2
        if l % 2 == 0:
            bb, tt = _block_rows(bsz, seq, 512)
            x, nb = _conv_layer(x, mod, l, row0, None if conv_a is None else conv_a[i],
                                wa["in"][i], wa["conv"][i], wa["out"][i], ln_g[l], ln_b[l],
                                bb=bb, tt=tt, alpha=alpha)
            new_a.append(nb)
        else:
            bb, tt = _block_rows(bsz, seq, *((512, 2) if conv_b is None else (2 * GDN_ROWS,)))
            x, nb, s = _gdn_layer(x, mod, l, row0, None if conv_b is None else conv_b[i],
                                  None if ssm_b is None else ssm_b[i],
                                  wb["in"][i], wb["tail"][i], wb["conv"][i], wb["a_log"][i],
                                  wb["dt_bias"][i], wb["norm"][i], wb["out"][i], ln_g[l], ln_b[l],
                                  bb=bb, tt=tt, alpha=alpha)
            new_cb.append(nb)
            new_s.append(s)
    return x, jnp.stack(new_a), jnp.stack(new_cb), jnp.stack(new_s)


def kernel(x_prompt, x_sample, state_conv_a, state_conv_b, state_ssm_b, c_prompt, c_sample, w_mod, b_mod, ln_g, ln_b, wa_in, wa_conv, wa_out, wb_in, wb_conv, wb_a_log, wb_dt_bias, wb_norm, wb_out):
    depth, d, _ = w_mod.shape
    alpha = (2 * depth) ** 0.25
    n_sample = x_sample.shape[0]

    mod = _modulation(jnp.concatenate([c_sample, c_prompt], axis=0), w_mod, b_mod)

    wa = {"in": wa_in.astype(BF16), "conv": wa_conv, "out": wa_out.astype(BF16)}
    n_main = wb_conv.shape[2] + wb_out.shape[1]
    wb = {"in": wb_in[:, :, :n_main].astype(BF16), "tail": wb_in[:, :, n_main:].astype(BF16),
          "conv": wb_conv,
          "a_log": wb_a_log, "dt_bias": wb_dt_bias, "norm": wb_norm, "out": wb_out.astype(BF16)}

    y_p, ca_p, cb_p, s_p = _trunk(x_prompt, mod, n_sample, None, None, None, ln_g, ln_b, wa, wb,
                                  alpha)
    y_s, ca_s, cb_s, s_s = _trunk(x_sample, mod, 0, state_conv_a, state_conv_b, state_ssm_b,
                                  ln_g, ln_b, wa, wb, alpha)
    return (y_p, y_s, ca_p, cb_p, s_p, ca_s, cb_s, s_s)
```

```python
import functools

import jax
import jax.numpy as jnp
from jax import lax
from jax.experimental import pallas as pl
from jax.experimental.pallas import tpu as pltpu

F32 = jnp.float32
BF16 = jnp.bfloat16

LN_EPS = 1e-5
NORM_EPS = 1e-6
SUBLANES = 8
LANES = 128
HALO = SUBLANES
GDN_ROWS = 64
VMEM_LIMIT = 60 * 1024 * 1024


def _dot(a, b):
    return jnp.dot(a.astype(BF16), b.astype(BF16), preferred_element_type=F32)


def _dot_nt(a, b):
    return lax.dot_general(a.astype(BF16), b.astype(BF16), (((1,), (1,)), ((), ())),
                           preferred_element_type=F32)


def _dot_tn(a, b):
    return lax.dot_general(a.astype(BF16), b.astype(BF16), (((0,), (0,)), ((), ())),
                           preferred_element_type=F32)


def _silu(x):
    return x * jax.nn.sigmoid(x)


def _split3(x):
    hi = x.astype(BF16)
    r = x - hi.astype(F32)
    mid = r.astype(BF16)
    lo = (r - mid.astype(F32)).astype(BF16)
    return hi, mid, lo


def _layer_norm(r, g, b):
    mu = jnp.mean(r, axis=-1, keepdims=True)
    d = r - mu
    var = jnp.mean(d * d, axis=-1, keepdims=True)
    return d * lax.rsqrt(var + LN_EPS) * g + b


def _mod_kernel(c_ref, w_ref, b_ref, o_ref):
    cs = _silu(c_ref[...])
    o_ref[0] = _dot(cs, w_ref[0]) + b_ref[0]


def _modulation(c_all, w_mod, b_mod):
    depth, d, n3 = w_mod.shape
    rows = c_all.shape[0]
    nb = d
    return pl.pallas_call(
        _mod_kernel,
        grid=(depth, n3 // nb),
        in_specs=[
            pl.BlockSpec((rows, d), lambda l, j: (0, 0)),
            pl.BlockSpec((1, d, nb), lambda l, j: (l, 0, j)),
            pl.BlockSpec((1, 1, nb), lambda l, j: (l, 0, j)),
        ],
        out_specs=pl.BlockSpec((1, rows, nb), lambda l, j: (l, 0, j)),
        out_shape=jax.ShapeDtypeStruct((depth, rows, n3), F32),
        compiler_params=pltpu.CompilerParams(
            dimension_semantics=("arbitrary", "arbitrary"), vmem_limit_bytes=VMEM_LIMIT),
        name="modulation",
    )(c_all, w_mod, b_mod.reshape(depth, 1, n3))


def _modulate(x_ref, mod_ref):
    x = x_ref[...]
    shift = mod_ref[:, 0:1, :]
    scale = mod_ref[:, 1:2, :]
    return x, x * (1.0 + scale) + shift


def _init_conv_halo(halo_scr, st_ref, width):
    halo_scr[...] = jnp.zeros(halo_scr.shape, F32)
    if st_ref is not None:
        halo_scr[:, HALO - (width - 1):HALO, :] = st_ref[...]


def _causal_conv(halo_scr, nst_ref, p, wconv_ref, width, sl):
    bb, tt, cw = p.shape
    ext = jnp.concatenate([halo_scr[:, :, sl], p], axis=1)
    halo_scr[:, :, sl] = ext[:, tt:, :]
    nst_ref[:, :, sl] = ext[:, HALO + tt - (width - 1):, :]
    e = ext.reshape(bb * (HALO + tt), cw)
    e1 = pltpu.roll(e, 1, axis=0)
    tap = lambda k: wconv_ref[width - 1 - k:width - k, sl]
    y = None
    for m in range((width + 1) // 2):
        pair = e * tap(2 * m)
        if 2 * m + 1 < width:
            pair = pair + e1 * tap(2 * m + 1)
        if m:
            pair = pltpu.roll(pair, 2 * m, axis=0)
        y = pair if y is None else y + pair
    return y.reshape(bb, HALO + tt, cw)[:, HALO:, :].reshape(bb * tt, cw)


def _residual_norm(x, out, mod_ref, lng_ref, lnb_ref, o_ref, alpha):
    bb, tt, d = o_ref.shape
    gate = mod_ref[:, 2:3, :]
    r = alpha * x.reshape(bb, tt, d) + gate * out.reshape(bb, tt, d)
    o_ref[...] = _layer_norm(r, lng_ref[...], lnb_ref[...])


def _const_spec(shape):
    zeros = (0,) * len(shape)
    return pl.BlockSpec(shape, lambda *_: zeros, pipeline_mode=pl.Buffered(1))


def _conv_layer_kernel(*refs, has_state, width, col_w, alpha):
    if has_state:
        (x_ref, mod_ref, st_ref, win_ref, wconv_ref, wout_ref, lng_ref, lnb_ref,
         o_ref, nst_ref, halo_scr, y_scr) = refs
    else:
        (x_ref, mod_ref, win_ref, wconv_ref, wout_ref, lng_ref, lnb_ref,
         o_ref, nst_ref, halo_scr, y_scr) = refs
        st_ref = None
    bb, tt, d = x_ref.shape
    rows = bb * tt
    dc = wconv_ref.shape[1]

    @pl.when(pl.program_id(1) == 0)
    def _():
        _init_conv_halo(halo_scr, st_ref, width)

    x, u = _modulate(x_ref, mod_ref)
    u = u.reshape(rows, d).astype(BF16)

    for j in range(dc // col_w):
        sl = slice(j * col_w, (j + 1) * col_w)
        pc = _dot(u, win_ref[:, dc + j * col_w:dc + (j + 1) * col_w])
        ph = _dot(u, win_ref[:, 2 * dc + j * col_w:2 * dc + (j + 1) * col_w])
        conv = _causal_conv(halo_scr, nst_ref, (pc * ph).reshape(bb, tt, col_w), wconv_ref, width, sl)
        pb = _dot(u, win_ref[:, sl])
        pz = _dot(u, win_ref[:, 3 * dc + j * col_w:3 * dc + (j + 1) * col_w])
        y_scr[:, sl] = (pb * conv * _silu(pz)).astype(BF16)

    out = _dot(y_scr[...], wout_ref[...])
    _residual_norm(x, out, mod_ref, lng_ref, lnb_ref, o_ref, alpha)


def _conv_layer(x, mod, state, w_in, w_conv, w_out, ln_g, ln_b, *, bb, tt, alpha):
    bsz, seq, d = x.shape
    width, dc = w_conv.shape
    has_state = state is not None
    rows = bb * tt
    seq_spec = lambda n: pl.BlockSpec((bb, n, d if n == 3 else dc), lambda b, t: (b, 0, 0))
    in_specs = [pl.BlockSpec((bb, tt, d), lambda b, t: (b, t, 0)), seq_spec(3)]
    args = [x, mod]
    if has_state:
        in_specs.append(seq_spec(width - 1))
        args.append(state)
    in_specs += [_const_spec(w_in.shape), _const_spec(w_conv.shape), _const_spec(w_out.shape),
                 _const_spec((1, d)), _const_spec((1, d))]
    args += [w_in, w_conv, w_out, ln_g.reshape(1, d), ln_b.reshape(1, d)]
    kern = functools.partial(_conv_layer_kernel, has_state=has_state, width=width,
                             col_w=min(dc, 256), alpha=alpha)
    return pl.pallas_call(
        kern,
        grid=(bsz // bb, seq // tt),
        in_specs=in_specs,
        out_specs=[pl.BlockSpec((bb, tt, d), lambda b, t: (b, t, 0)), seq_spec(width - 1)],
        out_shape=[jax.ShapeDtypeStruct((bsz, seq, d), F32),
                   jax.ShapeDtypeStruct((bsz, width - 1, dc), F32)],
        scratch_shapes=[pltpu.VMEM((bb, HALO, dc), F32), pltpu.VMEM((rows, dc), BF16)],
        compiler_params=pltpu.CompilerParams(
            dimension_semantics=("arbitrary", "arbitrary"), vmem_limit_bytes=VMEM_LIMIT),
        name="conv_layer",
    )(*args)


def _tri_masks(n, seq_len):
    i = lax.broadcasted_iota(jnp.int32, (n, n), 0)
    j = lax.broadcasted_iota(jnp.int32, (n, n), 1)
    same = (i // seq_len) == (j // seq_len)
    causal = same & (i >= j)
    strict = same & (i > j)
    eye = (i == j).astype(F32)
    levels = []
    s = 1
    while s < seq_len:
        levels.append(((i // (2 * s)) == (j // (2 * s))) & ((i % (2 * s)) >= s) & ((j % (2 * s)) < s))
        s *= 2
    return causal, strict, eye, levels


def _gdn_layer_kernel(*refs, has_state, width, n_heads, dk, dv, seq_len, col_w, group, alpha):
    if has_state:
        (x_ref, mod_ref, cst_ref, s0_ref, win_ref, wconv_ref, alog_ref, dtb_ref, nw_ref,
         wout_ref, lng_ref, lnb_ref, o_ref, ncst_ref, ns_ref,
         halo_scr, wtail_scr, qkv_scr, z_scr, gt_scr, gx_scr, kb_scr, u_scr, wq_scr, attn_scr,
         oh_scr) = refs
        s_scr = None
    else:
        (x_ref, mod_ref, win_ref, wconv_ref, alog_ref, dtb_ref, nw_ref,
         wout_ref, lng_ref, lnb_ref, o_ref, ncst_ref, ns_ref,
         halo_scr, wtail_scr, qkv_scr, z_scr, gt_scr, gx_scr, kb_scr, u_scr, wq_scr, attn_scr,
         oh_scr, s_scr) = refs
        cst_ref = s0_ref = None
    bb, tt, d = x_ref.shape
    rows = bb * tt
    ch = wconv_ref.shape[1]
    qk = n_heads * dk
    vw = n_heads * dv
    n = min(GDN_ROWS, rows)
    span = min(seq_len, n)
    n_seq = n // span
    n_chunks = rows // n
    heads = range(n_heads)
    q_sl = lambda h: slice(h * dk, (h + 1) * dk)
    k_sl = lambda h: slice(qk + h * dk, qk + (h + 1) * dk)
    v_sl = lambda h: slice(2 * qk + h * dv, 2 * qk + (h + 1) * dv)
    o_sl = lambda h: slice(h * dv, (h + 1) * dv)

    @pl.when(pl.program_id(1) == 0)
    def _():
        _init_conv_halo(halo_scr, cst_ref, width)
        if s_scr is not None:
            s_scr[...] = jnp.zeros(s_scr.shape, F32)

    @pl.when((pl.program_id(0) == 0) & (pl.program_id(1) == 0))
    def _():
        tail = win_ref.shape[1] - (ch + vw)
        wtail_scr[...] = jnp.zeros(wtail_scr.shape, BF16)
        wtail_scr[:, 0:tail] = win_ref[:, ch + vw:]

    x, u = _modulate(x_ref, mod_ref)
    u = u.reshape(rows, d).astype(BF16)

    ab = _dot(u, wtail_scr[...])
    lane = lax.broadcasted_iota(jnp.int32, (rows, LANES), 1)
    a = ab + dtb_ref[...]
    softplus = jnp.maximum(a, 0.0) + jnp.log(1.0 + jnp.exp(-jnp.abs(a)))
    g = jnp.where((lane >= n_heads) & (lane < 2 * n_heads),
                  -jnp.exp(alog_ref[...]) * softplus, 0.0)
    beta = jax.nn.sigmoid(ab)
    ri = lax.broadcasted_iota(jnp.int32, (rows, rows), 0)
    rj = lax.broadcasted_iota(jnp.int32, (rows, rows), 1)
    tri = (((ri // span) == (rj // span)) & (ri >= rj)).astype(BF16)
    g_hi, g_mid, g_lo = _split3(g)
    g_cum = (jnp.dot(tri, g_hi, preferred_element_type=F32)
             + jnp.dot(tri, g_mid, preferred_element_type=F32)
             + jnp.dot(tri, g_lo, preferred_element_type=F32))
    for c in range(n_chunks):
        gt_scr[c] = g_cum[c * n:(c + 1) * n, :].T
    for h in heads:
        gx_scr[:, o_sl(h)] = jnp.broadcast_to(g_cum[:, n_heads + h:n_heads + h + 1], (rows, dk))
        kb_scr[:, o_sl(h)] = jnp.broadcast_to(beta[:, h:h + 1], (rows, dk))

    for j in range(ch // col_w):
        sl = slice(j * col_w, (j + 1) * col_w)
        p = _dot(u, win_ref[:, sl]).reshape(bb, tt, col_w)
        qkv_scr[:, sl] = _silu(_causal_conv(halo_scr, ncst_ref, p, wconv_ref, width, sl))

    z_scr[...] = _silu(_dot(u, win_ref[:, ch:ch + vw]))

    for h in heads:
        t = qkv_scr[:, q_sl(h)]
        qkv_scr[:, q_sl(h)] = t * (lax.rsqrt(jnp.sum(t * t, axis=-1, keepdims=True) + NORM_EPS)
                                   * dk ** -0.5)
        t = qkv_scr[:, k_sl(h)]
        t = t * lax.rsqrt(jnp.sum(t * t, axis=-1, keepdims=True) + NORM_EPS)
        qkv_scr[:, k_sl(h)] = t
        beta_x = kb_scr[:, o_sl(h)]
        qkv_scr[:, v_sl(h)] = qkv_scr[:, v_sl(h)] * beta_x
        kb_scr[:, o_sl(h)] = t * beta_x

    causal, strict, eye, levels = _tri_masks(n, span)
    row_seq = lax.broadcasted_iota(jnp.int32, (n, 1), 0) // span

    def local_body(gi, carry):
        items = []
        for ci in range(group):
            c = gi * group + ci
            rs = pl.ds(pl.multiple_of(c * n, n), n)
            items += [(c, rs, h) for h in heads]
        decay, a_mat, t_inv = [], [], []
        for c, rs, h in items:
            g_row = gt_scr[c, n_heads + h:n_heads + h + 1, :]
            diff = gx_scr[rs, h * dv:h * dv + n] - g_row
            decay.append(jnp.where(causal, jnp.exp(jnp.where(causal, diff, 0.0)), 0.0))
        for (c, rs, h), dec in zip(items, decay):
            k = qkv_scr[rs, k_sl(h)]
            a_mat.append(jnp.where(strict, _dot_nt(kb_scr[rs, o_sl(h)], k) * dec, 0.0))
        for (c, rs, h), dec in zip(items, decay):
            attn_scr[rs, h * dv:h * dv + n] = (
                _dot_nt(qkv_scr[rs, q_sl(h)], qkv_scr[rs, k_sl(h)]) * dec)
        t_inv = [eye - jnp.where(levels[0], a, 0.0) for a in a_mat]
        for li, m in enumerate(levels[1:]):
            s = 2 << li
            if s % SUBLANES:
                y = [_dot(jnp.where(m, a, 0.0), t) for a, t in zip(a_mat, t_inv)]
                xs = [_dot(t, yy) for t, yy in zip(t_inv, y)]
                t_inv = [t - xx for t, xx in zip(t_inv, xs)]
                continue
            lo = [(b * 2 * s + s, b * 2 * s + 2 * s) for b in range(n // (2 * s))]
            take = lambda x: jnp.concatenate([x[r0:r1] for r0, r1 in lo], axis=0)
            def put(xc, rest):
                parts = []
                for b, (r0, r1) in enumerate(lo):
                    parts += [rest(r0 - s, r0), xc[b * s:(b + 1) * s]]
                return jnp.concatenate(parts, axis=0)
            zeros = jnp.zeros((s, n), F32)
            y = [put(_dot(take(jnp.where(m, a, 0.0)), t), lambda r0, r1: zeros)
                 for a, t in zip(a_mat, t_inv)]
            xs = [_dot(take(t), yy) for t, yy in zip(t_inv, y)]
            t_inv = [put(take(t) - xx, lambda r0, r1, t=t: t[r0:r1]) for t, xx in zip(t_inv, xs)]
        for (c, rs, h), t in zip(items, t_inv):
            e_x = jnp.exp(gx_scr[rs, o_sl(h)])
            rhs = jnp.concatenate([qkv_scr[rs, v_sl(h)], kb_scr[rs, o_sl(h)] * e_x], axis=1)
            sol = _dot(t, rhs)
            u_scr[rs, o_sl(h)] = sol[:, :dv]
            wq_scr[c, 0:n, o_sl(h)] = sol[:, dv:]
            wq_scr[c, n:2 * n, o_sl(h)] = qkv_scr[rs, q_sl(h)] * e_x
        return carry

    lax.fori_loop(0, n_chunks // group, local_body, 0)

    chunks_per_seq = tt // n if not has_state else n_chunks

    def state_body(c, carry):
        if has_state:
            lanes = [(0, s) for s in range(n_seq)]
            rows_of = lambda q: slice(c * n, (c + 1) * n)
            chunk_of = lambda q: c
        else:
            lanes = [(q, 0) for q in range(bb)]
            rows_of = lambda q: slice(q * tt + c * n, q * tt + (c + 1) * n)
            chunk_of = lambda q: q * chunks_per_seq + c
        groups = sorted({q for q, _ in lanes})
        masks = [None if n_seq == 1 else (row_seq == s) for s in range(n_seq)]
        if has_state:
            s_prev = {(h, q, s): s0_ref[c * n_seq + s, h] for h in heads for q, s in lanes}
        else:
            s_prev = {(h, q, s): s_scr[q, h] for h in heads for q, s in lanes}
        def own_rows(q, s, h):
            if n_seq == 1:
                return wq_scr[chunk_of(q), :, o_sl(h)]
            return jnp.concatenate(
                [wq_scr[chunk_of(q), s * span:(s + 1) * span, o_sl(h)],
                 wq_scr[chunk_of(q), n + s * span:n + (s + 1) * span, o_sl(h)]], axis=0)
        ws = {(h, q, s): _dot(own_rows(q, s, h), s_prev[(h, q, s)])
              for h in heads for q, s in lanes}
        v_new, v_dec, o_part, g_last = {}, {}, {}, {}
        for h in heads:
            for q in groups:
                w_s = jnp.concatenate([ws[(h, q, s)][:span] for s in range(n_seq)], axis=0)
                v_new[h, q] = u_scr[rows_of(q), o_sl(h)] - w_s
                o_part[h, q] = jnp.concatenate(
                    [ws[(h, q, s)][span:] for s in range(n_seq)], axis=0)
                gx = gx_scr[rows_of(q), o_sl(h)]
                for s in range(n_seq):
                    g_last[h, q, s] = gx[(s + 1) * span - 1:(s + 1) * span, :]
                g_end = jnp.concatenate(
                    [jnp.broadcast_to(g_last[h, q, s], (span, dv)) for s in range(n_seq)], axis=0)
                v_dec[h, q] = v_new[h, q] * jnp.exp(g_end - gx)
        for h in heads:
            for q in groups:
                oh_scr[rows_of(q), o_sl(h)] = o_part[h, q] + _dot(
                    attn_scr[rows_of(q), h * dv:h * dv + n], v_new[h, q])
        for h in heads:
            for q in groups:
                vs = v_dec[h, q] if n_seq == 1 else jnp.concatenate(
                    [jnp.where(masks[s], v_dec[h, q], 0.0) for s in range(n_seq)], axis=1)
                ktv = _dot_tn(qkv_scr[rows_of(q), k_sl(h)], vs)
                for s in range(n_seq):
                    s_new = (s_prev[(h, q, s)] * jnp.exp(g_last[h, q, s])
                             + ktv[:, s * dv:(s + 1) * dv])
                    if has_state:
                        ns_ref[c * n_seq + s, h] = s_new
                    else:
                        s_scr[q, h] = s_new
        return carry

    def gated_norm(r):
        for h in heads:
            o = oh_scr[r, o_sl(h)]
            o = o * lax.rsqrt(jnp.mean(o * o, axis=-1, keepdims=True) + NORM_EPS) * nw_ref[...]
            oh_scr[r, o_sl(h)] = o * z_scr[r, o_sl(h)]

    for c in range(chunks_per_seq):
        state_body(c, 0)
        for q in range(1 if has_state else bb):
            gated_norm(slice(q * tt + c * n, q * tt + (c + 1) * n))
    if not has_state:
        ns_ref[...] = s_scr[...]

    out = _dot(oh_scr[...], wout_ref[...])
    _residual_norm(x, out, mod_ref, lng_ref, lnb_ref, o_ref, alpha)


def _gdn_layer(x, mod, conv_state, ssm_state, w_in, w_conv, a_log, dt_bias, norm_w, w_out,
               ln_g, ln_b, *, bb, tt, alpha):
    bsz, seq, d = x.shape
    width, ch = w_conv.shape
    n_heads = a_log.shape[0]
    vw = w_out.shape[0]
    dv = vw // n_heads
    dk = (ch - vw) // 2 // n_heads
    qk = n_heads * dk
    has_state = conv_state is not None
    rows = bb * tt
    seq_len = min(seq, GDN_ROWS)
    n = min(GDN_ROWS, rows)
    n_chunks = rows // n
    assert dk == dv == LANES, "one head per lane tile"
    assert rows % n == 0 and n % seq_len == 0
    assert has_state == (tt == seq), "a carried state needs whole sequences per block"
    group = max(g for g in (8, 4, 2, 1) if n_chunks % g == 0)

    assert w_in.shape[1] == ch + vw + 2 * n_heads and 2 * n_heads <= LANES
    pad = lambda vec: jnp.zeros((1, LANES), F32).at[0, n_heads:2 * n_heads].set(vec)
    in_specs = [pl.BlockSpec((bb, tt, d), lambda b, t: (b, t, 0)),
                pl.BlockSpec((bb, 3, d), lambda b, t: (b, 0, 0))]
    args = [x, mod]
    if has_state:
        in_specs += [pl.BlockSpec((bb, width - 1, ch), lambda b, t: (b, 0, 0)),
                     pl.BlockSpec((bb, n_heads, dk, dv), lambda b, t: (b, 0, 0, 0))]
        args += [conv_state, ssm_state]
    in_specs += [_const_spec(w_in.shape), _const_spec(w_conv.shape),
                 _const_spec((1, LANES)), _const_spec((1, LANES)), _const_spec((1, dv)),
                 _const_spec(w_out.shape), _const_spec((1, d)), _const_spec((1, d))]
    args += [w_in, w_conv, pad(a_log), pad(dt_bias), norm_w.reshape(1, dv), w_out,
             ln_g.reshape(1, d), ln_b.reshape(1, d)]
    scratch = [pltpu.VMEM((bb, HALO, ch), F32),
               pltpu.VMEM((d, LANES), BF16),
               pltpu.VMEM((rows, ch), F32),
               pltpu.VMEM((rows, vw), F32),
               pltpu.VMEM((n_chunks, LANES, n), F32),
               pltpu.VMEM((rows, vw), F32),
               pltpu.VMEM((rows, qk), F32),
               pltpu.VMEM((rows, vw), F32),
               pltpu.VMEM((n_chunks, 2 * n, qk), F32),
               pltpu.VMEM((rows, vw), F32),
               pltpu.VMEM((rows, vw), F32)]
    if not has_state:
        scratch.append(pltpu.VMEM((bb, n_heads, dk, dv), F32))
    kern = functools.partial(_gdn_layer_kernel, has_state=has_state, width=width, n_heads=n_heads,
                             dk=dk, dv=dv, seq_len=seq_len, col_w=min(ch, 512), group=group,
                             alpha=alpha)
    return pl.pallas_call(
        kern,
        grid=(bsz // bb, seq // tt),
        in_specs=in_specs,
        out_specs=[pl.BlockSpec((bb, tt, d), lambda b, t: (b, t, 0)),
                   pl.BlockSpec((bb, width - 1, ch), lambda b, t: (b, 0, 0)),
                   pl.BlockSpec((bb, n_heads, dk, dv), lambda b, t: (b, 0, 0, 0))],
        out_shape=[jax.ShapeDtypeStruct((bsz, seq, d), F32),
                   jax.ShapeDtypeStruct((bsz, width - 1, ch), F32),
                   jax.ShapeDtypeStruct((bsz, n_heads, dk, dv), F32)],
        scratch_shapes=scratch,
        compiler_params=pltpu.CompilerParams(
            dimension_semantics=("arbitrary", "arbitrary"), vmem_limit_bytes=VMEM_LIMIT),
        name="gdn_layer",
    )(*args)


def _block_rows(bsz, seq, target, long_seqs=1):
    if seq * long_seqs >= target:
        return long_seqs, target // long_seqs
    return max(1, min(bsz, target // seq)), seq


def _trunk(x, mod, conv_a, conv_b, ssm_b, ln_g, ln_b, wa, wb, alpha):
    bsz, seq, _ = x.shape
    new_a, new_cb, new_s = [], [], []
    depth = mod.shape[0]
    for l in range(depth):
        i = l // 2
        if l % 2 == 0:
            bb, tt = _block_rows(bsz, seq, 512)
            x, nb = _conv_layer(x, mod[l], None if conv_a is None else conv_a[i],
                                wa["in"][i], wa["conv"][i], wa["out"][i], ln_g[l], ln_b[l],
                                bb=bb, tt=tt, alpha=alpha)
            new_a.append(nb)
        else:
            bb, tt = _block_rows(bsz, seq, *((512, 2) if conv_b is None else (2 * GDN_ROWS,)))
            x, nb, s = _gdn_layer(x, mod[l], None if conv_b is None else conv_b[i],
                                  None if ssm_b is None else ssm_b[i],
                                  wb["in"][i], wb["conv"][i], wb["a_log"][i],
                                  wb["dt_bias"][i], wb["norm"][i], wb["out"][i], ln_g[l], ln_b[l],
                                  bb=bb, tt=tt, alpha=alpha)
            new_cb.append(nb)
            new_s.append(s)
    return x, jnp.stack(new_a), jnp.stack(new_cb), jnp.stack(new_s)


def kernel(x_prompt, x_sample, state_conv_a, state_conv_b, state_ssm_b, c_prompt, c_sample, w_mod, b_mod, ln_g, ln_b, wa_in, wa_conv, wa_out, wb_in, wb_conv, wb_a_log, wb_dt_bias, wb_norm, wb_out):
    depth, d, _ = w_mod.shape
    alpha = (2 * depth) ** 0.25
    bp = x_prompt.shape[0]

    mod = _modulation(jnp.concatenate([c_prompt, c_sample], axis=0), w_mod, b_mod)
    mod = mod.reshape(depth, -1, 3, d)
    mod_p, mod_s = mod[:, :bp], mod[:, bp:]

    wa = {"in": wa_in.astype(BF16), "conv": wa_conv, "out": wa_out.astype(BF16)}
    wb = {"in": wb_in.astype(BF16), "conv": wb_conv,
          "a_log": wb_a_log, "dt_bias": wb_dt_bias, "norm": wb_norm, "out": wb_out.astype(BF16)}

    y_p, ca_p, cb_p, s_p = _trunk(x_prompt, mod_p, None, None, None, ln_g, ln_b, wa, wb, alpha)
    y_s, ca_s, cb_s, s_s = _trunk(x_sample, mod_s, state_conv_a, state_conv_b, state_ssm_b,
                                  ln_g, ln_b, wa, wb, alpha)
    return (y_p, y_s, ca_p, cb_p, s_p, ca_s, cb_s, s_s)
```

```python
import functools

import jax
import jax.numpy as jnp
from jax import lax
from jax.experimental import pallas as pl
from jax.experimental.pallas import tpu as pltpu

F32 = jnp.float32
BF16 = jnp.bfloat16

LN_EPS = 1e-5
NORM_EPS = 1e-6
SUBLANES = 8
LANES = 128
HALO = SUBLANES
GDN_ROWS = 64
VMEM_LIMIT = 60 * 1024 * 1024


def _dot(a, b):
    return jnp.dot(a.astype(BF16), b.astype(BF16), preferred_element_type=F32)


def _dot_nt(a, b):
    return lax.dot_general(a.astype(BF16), b.astype(BF16), (((1,), (1,)), ((), ())),
                           preferred_element_type=F32)


def _dot_tn(a, b):
    return lax.dot_general(a.astype(BF16), b.astype(BF16), (((0,), (0,)), ((), ())),
                           preferred_element_type=F32)


def _silu(x):
    return x * jax.nn.sigmoid(x)


def _split3(x):
    hi = x.astype(BF16)
    r = x - hi.astype(F32)
    mid = r.astype(BF16)
    lo = (r - mid.astype(F32)).astype(BF16)
    return hi, mid, lo


def _layer_norm(r, g, b):
    mu = jnp.mean(r, axis=-1, keepdims=True)
    d = r - mu
    var = jnp.mean(d * d, axis=-1, keepdims=True)
    return d * lax.rsqrt(var + LN_EPS) * g + b


def _mod_kernel(c_ref, w_ref, b_ref, o_ref):
    cs = _silu(c_ref[...])
    o_ref[0] = _dot(cs, w_ref[0]) + b_ref[0]


def _modulation(c_all, w_mod, b_mod):
    depth, d, n3 = w_mod.shape
    rows = c_all.shape[0]
    nb = d
    return pl.pallas_call(
        _mod_kernel,
        grid=(depth, n3 // nb),
        in_specs=[
            pl.BlockSpec((rows, d), lambda l, j: (0, 0)),
            pl.BlockSpec((1, d, nb), lambda l, j: (l, 0, j)),
            pl.BlockSpec((1, 1, nb), lambda l, j: (l, 0, j)),
        ],
        out_specs=pl.BlockSpec((1, rows, nb), lambda l, j: (l, 0, j)),
        out_shape=jax.ShapeDtypeStruct((depth, rows, n3), F32),
        compiler_params=pltpu.CompilerParams(
            dimension_semantics=("arbitrary", "arbitrary"), vmem_limit_bytes=VMEM_LIMIT),
        name="modulation",
    )(c_all, w_mod, b_mod.reshape(depth, 1, n3))


def _modulate(x_ref, mod_ref):
    x = x_ref[...]
    shift = mod_ref[:, 0:1, :]
    scale = mod_ref[:, 1:2, :]
    return x, x * (1.0 + scale) + shift


def _init_conv_halo(halo_scr, st_ref, width):
    halo_scr[...] = jnp.zeros(halo_scr.shape, F32)
    if st_ref is not None:
        halo_scr[:, HALO - (width - 1):HALO, :] = st_ref[...]


def _causal_conv(halo_scr, nst_ref, p, wconv_ref, width, sl):
    bb, tt, cw = p.shape
    ext = jnp.concatenate([halo_scr[:, :, sl], p], axis=1)
    halo_scr[:, :, sl] = ext[:, tt:, :]
    nst_ref[:, :, sl] = ext[:, HALO + tt - (width - 1):, :]
    e = ext.reshape(bb * (HALO + tt), cw)
    e1 = pltpu.roll(e, 1, axis=0)
    tap = lambda k: wconv_ref[width - 1 - k:width - k, sl]
    y = None
    for m in range((width + 1) // 2):
        pair = e * tap(2 * m)
        if 2 * m + 1 < width:
            pair = pair + e1 * tap(2 * m + 1)
        if m:
            pair = pltpu.roll(pair, 2 * m, axis=0)
        y = pair if y is None else y + pair
    return y.reshape(bb, HALO + tt, cw)[:, HALO:, :].reshape(bb * tt, cw)


def _residual_norm(x, out, mod_ref, lng_ref, lnb_ref, o_ref, alpha):
    bb, tt, d = o_ref.shape
    gate = mod_ref[:, 2:3, :]
    r = alpha * x.reshape(bb, tt, d) + gate * out.reshape(bb, tt, d)
    o_ref[...] = _layer_norm(r, lng_ref[...], lnb_ref[...])


def _const_spec(shape):
    zeros = (0,) * len(shape)
    return pl.BlockSpec(shape, lambda *_: zeros, pipeline_mode=pl.Buffered(1))


def _conv_layer_kernel(*refs, has_state, width, col_w, alpha):
    if has_state:
        (x_ref, mod_ref, st_ref, win_ref, wconv_ref, wout_ref, lng_ref, lnb_ref,
         o_ref, nst_ref, halo_scr, y_scr) = refs
    else:
        (x_ref, mod_ref, win_ref, wconv_ref, wout_ref, lng_ref, lnb_ref,
         o_ref, nst_ref, halo_scr, y_scr) = refs
        st_ref = None
    bb, tt, d = x_ref.shape
    rows = bb * tt
    dc = wconv_ref.shape[1]

    @pl.when(pl.program_id(1) == 0)
    def _():
        _init_conv_halo(halo_scr, st_ref, width)

    x, u = _modulate(x_ref, mod_ref)
    u = u.reshape(rows, d).astype(BF16)

    for j in range(dc // col_w):
        sl = slice(j * col_w, (j + 1) * col_w)
        pc = _dot(u, win_ref[:, dc + j * col_w:dc + (j + 1) * col_w])
        ph = _dot(u, win_ref[:, 2 * dc + j * col_w:2 * dc + (j + 1) * col_w])
        conv = _causal_conv(halo_scr, nst_ref, (pc * ph).reshape(bb, tt, col_w), wconv_ref, width, sl)
        pb = _dot(u, win_ref[:, sl])
        pz = _dot(u, win_ref[:, 3 * dc + j * col_w:3 * dc + (j + 1) * col_w])
        y_scr[:, sl] = (pb * conv * _silu(pz)).astype(BF16)

    out = _dot(y_scr[...], wout_ref[...])
    _residual_norm(x, out, mod_ref, lng_ref, lnb_ref, o_ref, alpha)


def _conv_layer(x, mod, state, w_in, w_conv, w_out, ln_g, ln_b, *, bb, tt, alpha):
    bsz, seq, d = x.shape
    width, dc = w_conv.shape
    has_state = state is not None
    rows = bb * tt
    seq_spec = lambda n: pl.BlockSpec((bb, n, d if n == 3 else dc), lambda b, t: (b, 0, 0))
    in_specs = [pl.BlockSpec((bb, tt, d), lambda b, t: (b, t, 0)), seq_spec(3)]
    args = [x, mod]
    if has_state:
        in_specs.append(seq_spec(width - 1))
        args.append(state)
    in_specs += [_const_spec(w_in.shape), _const_spec(w_conv.shape), _const_spec(w_out.shape),
                 _const_spec((1, d)), _const_spec((1, d))]
    args += [w_in, w_conv, w_out, ln_g.reshape(1, d), ln_b.reshape(1, d)]
    kern = functools.partial(_conv_layer_kernel, has_state=has_state, width=width,
                             col_w=min(dc, 256), alpha=alpha)
    return pl.pallas_call(
        kern,
        grid=(bsz // bb, seq // tt),
        in_specs=in_specs,
        out_specs=[pl.BlockSpec((bb, tt, d), lambda b, t: (b, t, 0)), seq_spec(width - 1)],
        out_shape=[jax.ShapeDtypeStruct((bsz, seq, d), F32),
                   jax.ShapeDtypeStruct((bsz, width - 1, dc), F32)],
        scratch_shapes=[pltpu.VMEM((bb, HALO, dc), F32), pltpu.VMEM((rows, dc), BF16)],
        compiler_params=pltpu.CompilerParams(
            dimension_semantics=("arbitrary", "arbitrary"), vmem_limit_bytes=VMEM_LIMIT),
        name="conv_layer",
    )(*args)


def _tri_masks(n, seq_len):
    i = lax.broadcasted_iota(jnp.int32, (n, n), 0)
    j = lax.broadcasted_iota(jnp.int32, (n, n), 1)
    same = (i // seq_len) == (j // seq_len)
    causal = same & (i >= j)
    strict = same & (i > j)
    eye = (i == j).astype(F32)
    levels = []
    s = 1
    while s < seq_len:
        levels.append(((i // (2 * s)) == (j // (2 * s))) & ((i % (2 * s)) >= s) & ((j % (2 * s)) < s))
        s *= 2
    return causal, strict, eye, levels


def _gdn_layer_kernel(*refs, has_state, width, n_heads, dk, dv, seq_len, col_w, group, alpha):
    if has_state:
        (x_ref, mod_ref, cst_ref, s0_ref, win_ref, wconv_ref, alog_ref, dtb_ref, nw_ref,
         wout_ref, lng_ref, lnb_ref, o_ref, ncst_ref, ns_ref,
         halo_scr, wtail_scr, qkv_scr, z_scr, gt_scr, gx_scr, kb_scr, u_scr, wq_scr, attn_scr,
         oh_scr) = refs
        s_scr = None
    else:
        (x_ref, mod_ref, win_ref, wconv_ref, alog_ref, dtb_ref, nw_ref,
         wout_ref, lng_ref, lnb_ref, o_ref, ncst_ref, ns_ref,
         halo_scr, wtail_scr, qkv_scr, z_scr, gt_scr, gx_scr, kb_scr, u_scr, wq_scr, attn_scr,
         oh_scr, s_scr) = refs
        cst_ref = s0_ref = None
    bb, tt, d = x_ref.shape
    rows = bb * tt
    ch = wconv_ref.shape[1]
    qk = n_heads * dk
    vw = n_heads * dv
    n = min(GDN_ROWS, rows)
    span = min(seq_len, n)
    n_seq = n // span
    n_chunks = rows // n
    heads = range(n_heads)
    q_sl = lambda h: slice(h * dk, (h + 1) * dk)
    k_sl = lambda h: slice(qk + h * dk, qk + (h + 1) * dk)
    v_sl = lambda h: slice(2 * qk + h * dv, 2 * qk + (h + 1) * dv)
    o_sl = lambda h: slice(h * dv, (h + 1) * dv)

    @pl.when(pl.program_id(1) == 0)
    def _():
        _init_conv_halo(halo_scr, cst_ref, width)
        if s_scr is not None:
            s_scr[...] = jnp.zeros(s_scr.shape, F32)

    @pl.when((pl.program_id(0) == 0) & (pl.program_id(1) == 0))
    def _():
        tail = win_ref.shape[1] - (ch + vw)
        wtail_scr[...] = jnp.zeros(wtail_scr.shape, BF16)
        wtail_scr[:, 0:tail] = win_ref[:, ch + vw:]

    x, u = _modulate(x_ref, mod_ref)
    u = u.reshape(rows, d).astype(BF16)

    ab = _dot(u, wtail_scr[...])
    lane = lax.broadcasted_iota(jnp.int32, (rows, LANES), 1)
    a = ab + dtb_ref[...]
    softplus = jnp.maximum(a, 0.0) + jnp.log(1.0 + jnp.exp(-jnp.abs(a)))
    g = jnp.where((lane >= n_heads) & (lane < 2 * n_heads),
                  -jnp.exp(alog_ref[...]) * softplus, 0.0)
    beta = jax.nn.sigmoid(ab)
    ri = lax.broadcasted_iota(jnp.int32, (rows, rows), 0)
    rj = lax.broadcasted_iota(jnp.int32, (rows, rows), 1)
    tri = (((ri // span) == (rj // span)) & (ri >= rj)).astype(BF16)
    g_hi, g_mid, g_lo = _split3(g)
    g_cum = (jnp.dot(tri, g_hi, preferred_element_type=F32)
             + jnp.dot(tri, g_mid, preferred_element_type=F32)
             + jnp.dot(tri, g_lo, preferred_element_type=F32))
    for c in range(n_chunks):
        gt_scr[c] = g_cum[c * n:(c + 1) * n, :].T
    for h in heads:
        gx_scr[:, o_sl(h)] = jnp.broadcast_to(g_cum[:, n_heads + h:n_heads + h + 1], (rows, dk))
        kb_scr[:, o_sl(h)] = jnp.broadcast_to(beta[:, h:h + 1], (rows, dk))

    for j in range(ch // col_w):
        sl = slice(j * col_w, (j + 1) * col_w)
        p = _dot(u, win_ref[:, sl]).reshape(bb, tt, col_w)
        qkv_scr[:, sl] = _silu(_causal_conv(halo_scr, ncst_ref, p, wconv_ref, width, sl))

    z_scr[...] = _silu(_dot(u, win_ref[:, ch:ch + vw]))

    for h in heads:
        t = qkv_scr[:, q_sl(h)]
        qkv_scr[:, q_sl(h)] = t * (lax.rsqrt(jnp.sum(t * t, axis=-1, keepdims=True) + NORM_EPS)
                                   * dk ** -0.5)
        t = qkv_scr[:, k_sl(h)]
        t = t * lax.rsqrt(jnp.sum(t * t, axis=-1, keepdims=True) + NORM_EPS)
        qkv_scr[:, k_sl(h)] = t
        beta_x = kb_scr[:, o_sl(h)]
        qkv_scr[:, v_sl(h)] = qkv_scr[:, v_sl(h)] * beta_x
        kb_scr[:, o_sl(h)] = t * beta_x

    causal, strict, eye, levels = _tri_masks(n, span)
    row_seq = lax.broadcasted_iota(jnp.int32, (n, 1), 0) // span

    def local_body(gi, carry):
        items = []
        for ci in range(group):
            c = gi * group + ci
            rs = pl.ds(pl.multiple_of(c * n, n), n)
            items += [(c, rs, h) for h in heads]
        decay, a_mat, t_inv = [], [], []
        for c, rs, h in items:
            g_row = gt_scr[c, n_heads + h:n_heads + h + 1, :]
            diff = gx_scr[rs, h * dv:h * dv + n] - g_row
            decay.append(jnp.where(causal, jnp.exp(jnp.where(causal, diff, 0.0)), 0.0))
        for (c, rs, h), dec in zip(items, decay):
            k = qkv_scr[rs, k_sl(h)]
            a_mat.append(jnp.where(strict, _dot_nt(kb_scr[rs, o_sl(h)], k) * dec, 0.0))
        for (c, rs, h), dec in zip(items, decay):
            attn_scr[rs, h * dv:h * dv + n] = (
                _dot_nt(qkv_scr[rs, q_sl(h)], qkv_scr[rs, k_sl(h)]) * dec)
        t_inv = [eye - jnp.where(levels[0], a, 0.0) for a in a_mat]
        for li, m in enumerate(levels[1:]):
            s = 2 << li
            if s % SUBLANES:
                y = [_dot(jnp.where(m, a, 0.0), t) for a, t in zip(a_mat, t_inv)]
                xs = [_dot(t, yy) for t, yy in zip(t_inv, y)]
                t_inv = [t - xx for t, xx in zip(t_inv, xs)]
                continue
            lo = [(b * 2 * s + s, b * 2 * s + 2 * s) for b in range(n // (2 * s))]
            take = lambda x: jnp.concatenate([x[r0:r1] for r0, r1 in lo], axis=0)
            def put(xc, rest):
                parts = []
                for b, (r0, r1) in enumerate(lo):
                    parts += [rest(r0 - s, r0), xc[b * s:(b + 1) * s]]
                return jnp.concatenate(parts, axis=0)
            zeros = jnp.zeros((s, n), F32)
            y = [put(_dot(take(jnp.where(m, a, 0.0)), t), lambda r0, r1: zeros)
                 for a, t in zip(a_mat, t_inv)]
            xs = [_dot(take(t), yy) for t, yy in zip(t_inv, y)]
            t_inv = [put(take(t) - xx, lambda r0, r1, t=t: t[r0:r1]) for t, xx in zip(t_inv, xs)]
        for (c, rs, h), t in zip(items, t_inv):
            e_x = jnp.exp(gx_scr[rs, o_sl(h)])
            rhs = jnp.concatenate([qkv_scr[rs, v_sl(h)], kb_scr[rs, o_sl(h)] * e_x], axis=1)
            sol = _dot(t, rhs)
            u_scr[rs, o_sl(h)] = sol[:, :dv]
            wq_scr[c, 0:n, o_sl(h)] = sol[:, dv:]
            wq_scr[c, n:2 * n, o_sl(h)] = qkv_scr[rs, q_sl(h)] * e_x
        return carry

    lax.fori_loop(0, n_chunks // group, local_body, 0)

    chunks_per_seq = tt // n if not has_state else n_chunks

    def state_body(c, carry):
        if has_state:
            lanes = [(0, s) for s in range(n_seq)]
            rows_of = lambda q: slice(c * n, (c + 1) * n)
            chunk_of = lambda q: c
        else:
            lanes = [(q, 0) for q in range(bb)]
            rows_of = lambda q: slice(q * tt + c * n, q * tt + (c + 1) * n)
            chunk_of = lambda q: q * chunks_per_seq + c
        groups = sorted({q for q, _ in lanes})
        masks = [None if n_seq == 1 else (row_seq == s) for s in range(n_seq)]
        if has_state:
            s_prev = {(h, q, s): s0_ref[c * n_seq + s, h] for h in heads for q, s in lanes}
        else:
            s_prev = {(h, q, s): s_scr[q, h] for h in heads for q, s in lanes}
        def own_rows(q, s, h):
            if n_seq == 1:
                return wq_scr[chunk_of(q), :, o_sl(h)]
            return jnp.concatenate(
                [wq_scr[chunk_of(q), s * span:(s + 1) * span, o_sl(h)],
                 wq_scr[chunk_of(q), n + s * span:n + (s + 1) * span, o_sl(h)]], axis=0)
        ws = {(h, q, s): _dot(own_rows(q, s, h), s_prev[(h, q, s)])
              for h in heads for q, s in lanes}
        v_new, v_dec, o_part, g_last = {}, {}, {}, {}
        for h in heads:
            for q in groups:
                w_s = jnp.concatenate([ws[(h, q, s)][:span] for s in range(n_seq)], axis=0)
                v_new[h, q] = u_scr[rows_of(q), o_sl(h)] - w_s
                o_part[h, q] = jnp.concatenate(
                    [ws[(h, q, s)][span:] for s in range(n_seq)], axis=0)
                gx = gx_scr[rows_of(q), o_sl(h)]
                for s in range(n_seq):
                    g_last[h, q, s] = gx[(s + 1) * span - 1:(s + 1) * span, :]
                g_end = jnp.concatenate(
                    [jnp.broadcast_to(g_last[h, q, s], (span, dv)) for s in range(n_seq)], axis=0)
                v_dec[h, q] = v_new[h, q] * jnp.exp(g_end - gx)
        for h in heads:
            for q in groups:
                oh_scr[rows_of(q), o_sl(h)] = o_part[h, q] + _dot(
                    attn_scr[rows_of(q), h * dv:h * dv + n], v_new[h, q])
        for h in heads:
            for q in groups:
                vs = v_dec[h, q] if n_seq == 1 else jnp.concatenate(
                    [jnp.where(masks[s], v_dec[h, q], 0.0) for s in range(n_seq)], axis=1)
                ktv = _dot_tn(qkv_scr[rows_of(q), k_sl(h)], vs)
                for s in range(n_seq):
                    s_new = (s_prev[(h, q, s)] * jnp.exp(g_last[h, q, s])
                             + ktv[:, s * dv:(s + 1) * dv])
                    if has_state:
                        ns_ref[c * n_seq + s, h] = s_new
                    else:
                        s_scr[q, h] = s_new
        return carry

    for c in range(chunks_per_seq):
        state_body(c, 0)
    if not has_state:
        ns_ref[...] = s_scr[...]

    for h in heads:
        o = oh_scr[:, o_sl(h)]
        o = o * lax.rsqrt(jnp.mean(o * o, axis=-1, keepdims=True) + NORM_EPS) * nw_ref[...]
        oh_scr[:, o_sl(h)] = o * z_scr[:, o_sl(h)]
    out = _dot(oh_scr[...], wout_ref[...])
    _residual_norm(x, out, mod_ref, lng_ref, lnb_ref, o_ref, alpha)


def _gdn_layer(x, mod, conv_state, ssm_state, w_in, w_conv, a_log, dt_bias, norm_w, w_out,
               ln_g, ln_b, *, bb, tt, alpha):
    bsz, seq, d = x.shape
    width, ch = w_conv.shape
    n_heads = a_log.shape[0]
    vw = w_out.shape[0]
    dv = vw // n_heads
    dk = (ch - vw) // 2 // n_heads
    qk = n_heads * dk
    has_state = conv_state is not None
    rows = bb * tt
    seq_len = min(seq, GDN_ROWS)
    n = min(GDN_ROWS, rows)
    n_chunks = rows // n
    assert dk == dv == LANES, "one head per lane tile"
    assert rows % n == 0 and n % seq_len == 0
    assert has_state == (tt == seq), "a carried state needs whole sequences per block"
    group = max(g for g in (8, 4, 2, 1) if n_chunks % g == 0)

    assert w_in.shape[1] == ch + vw + 2 * n_heads and 2 * n_heads <= LANES
    pad = lambda vec: jnp.zeros((1, LANES), F32).at[0, n_heads:2 * n_heads].set(vec)
    in_specs = [pl.BlockSpec((bb, tt, d), lambda b, t: (b, t, 0)),
                pl.BlockSpec((bb, 3, d), lambda b, t: (b, 0, 0))]
    args = [x, mod]
    if has_state:
        in_specs += [pl.BlockSpec((bb, width - 1, ch), lambda b, t: (b, 0, 0)),
                     pl.BlockSpec((bb, n_heads, dk, dv), lambda b, t: (b, 0, 0, 0))]
        args += [conv_state, ssm_state]
    in_specs += [_const_spec(w_in.shape), _const_spec(w_conv.shape),
                 _const_spec((1, LANES)), _const_spec((1, LANES)), _const_spec((1, dv)),
                 _const_spec(w_out.shape), _const_spec((1, d)), _const_spec((1, d))]
    args += [w_in, w_conv, pad(a_log), pad(dt_bias), norm_w.reshape(1, dv), w_out,
             ln_g.reshape(1, d), ln_b.reshape(1, d)]
    scratch = [pltpu.VMEM((bb, HALO, ch), F32),
               pltpu.VMEM((d, LANES), BF16),
               pltpu.VMEM((rows, ch), F32),
               pltpu.VMEM((rows, vw), F32),
               pltpu.VMEM((n_chunks, LANES, n), F32),
               pltpu.VMEM((rows, vw), F32),
               pltpu.VMEM((rows, qk), F32),
               pltpu.VMEM((rows, vw), F32),
               pltpu.VMEM((n_chunks, 2 * n, qk), F32),
               pltpu.VMEM((rows, vw), F32),
               pltpu.VMEM((rows, vw), F32)]
    if not has_state:
        scratch.append(pltpu.VMEM((bb, n_heads, dk, dv), F32))
    kern = functools.partial(_gdn_layer_kernel, has_state=has_state, width=width, n_heads=n_heads,
                             dk=dk, dv=dv, seq_len=seq_len, col_w=min(ch, 512), group=group,
                             alpha=alpha)
    return pl.pallas_call(
        kern,
        grid=(bsz // bb, seq // tt),
        in_specs=in_specs,
        out_specs=[pl.BlockSpec((bb, tt, d), lambda b, t: (b, t, 0)),
                   pl.BlockSpec((bb, width - 1, ch), lambda b, t: (b, 0, 0)),
                   pl.BlockSpec((bb, n_heads, dk, dv), lambda b, t: (b, 0, 0, 0))],
        out_shape=[jax.ShapeDtypeStruct((bsz, seq, d), F32),
                   jax.ShapeDtypeStruct((bsz, width - 1, ch), F32),
                   jax.ShapeDtypeStruct((bsz, n_heads, dk, dv), F32)],
        scratch_shapes=scratch,
        compiler_params=pltpu.CompilerParams(
            dimension_semantics=("arbitrary", "arbitrary"), vmem_limit_bytes=VMEM_LIMIT),
        name="gdn_layer",
    )(*args)


def _block_rows(bsz, seq, target, long_seqs=1):
    if seq * long_seqs >= target:
        return long_seqs, target // long_seqs
    return max(1, min(bsz, target // seq)), seq


def _trunk(x, mod, conv_a, conv_b, ssm_b, ln_g, ln_b, wa, wb, alpha):
    bsz, seq, _ = x.shape
    new_a, new_cb, new_s = [], [], []
    depth = mod.shape[0]
    for l in range(depth):
        i = l // 2
        if l % 2 == 0:
            bb, tt = _block_rows(bsz, seq, 1024)
            x, nb = _conv_layer(x, mod[l], None if conv_a is None else conv_a[i],
                                wa["in"][i], wa["conv"][i], wa["out"][i], ln_g[l], ln_b[l],
                                bb=bb, tt=tt, alpha=alpha)
            new_a.append(nb)
        else:
            bb, tt = _block_rows(bsz, seq, *((512, 2) if conv_b is None else (2 * GDN_ROWS,)))
            x, nb, s = _gdn_layer(x, mod[l], None if conv_b is None else conv_b[i],
                                  None if ssm_b is None else ssm_b[i],
                                  wb["in"][i], wb["conv"][i], wb["a_log"][i],
                                  wb["dt_bias"][i], wb["norm"][i], wb["out"][i], ln_g[l], ln_b[l],
                                  bb=bb, tt=tt, alpha=alpha)
            new_cb.append(nb)
            new_s.append(s)
    return x, jnp.stack(new_a), jnp.stack(new_cb), jnp.stack(new_s)


def kernel(x_prompt, x_sample, state_conv_a, state_conv_b, state_ssm_b, c_prompt, c_sample, w_mod, b_mod, ln_g, ln_b, wa_in, wa_conv, wa_out, wb_in, wb_conv, wb_a_log, wb_dt_bias, wb_norm, wb_out):
    depth, d, _ = w_mod.shape
    alpha = (2 * depth) ** 0.25
    bp = x_prompt.shape[0]

    mod = _modulation(jnp.concatenate([c_prompt, c_sample], axis=0), w_mod, b_mod)
    mod = mod.reshape(depth, -1, 3, d)
    mod_p, mod_s = mod[:, :bp], mod[:, bp:]

    wa = {"in": wa_in.astype(BF16), "conv": wa_conv, "out": wa_out.astype(BF16)}
    wb = {"in": wb_in.astype(BF16), "conv": wb_conv,
          "a_log": wb_a_log, "dt_bias": wb_dt_bias, "norm": wb_norm, "out": wb_out.astype(BF16)}

    y_p, ca_p, cb_p, s_p = _trunk(x_prompt, mod_p, None, None, None, ln_g, ln_b, wa, wb, alpha)
    y_s, ca_s, cb_s, s_s = _trunk(x_sample, mod_s, state_conv_a, state_conv_b, state_ssm_b,
                                  ln_g, ln_b, wa, wb, alpha)
    return (y_p, y_s, ca_p, cb_p, s_p, ca_s, cb_s, s_s)
```

```python
import functools

import jax
import jax.numpy as jnp
from jax import lax
from jax.experimental import pallas as pl
from jax.experimental.pallas import tpu as pltpu

F32 = jnp.float32
BF16 = jnp.bfloat16

LN_EPS = 1e-5
NORM_EPS = 1e-6
SUBLANES = 8
LANES = 128
HALO = SUBLANES
GDN_ROWS = 64
VMEM_LIMIT = 60 * 1024 * 1024


def _dot(a, b):
    return jnp.dot(a.astype(BF16), b.astype(BF16), preferred_element_type=F32)


def _dot_nt(a, b):
    return lax.dot_general(a.astype(BF16), b.astype(BF16), (((1,), (1,)), ((), ())),
                           preferred_element_type=F32)


def _dot_tn(a, b):
    return lax.dot_general(a.astype(BF16), b.astype(BF16), (((0,), (0,)), ((), ())),
                           preferred_element_type=F32)


def _silu(x):
    return x * jax.nn.sigmoid(x)


def _split3(x):
    hi = x.astype(BF16)
    r = x - hi.astype(F32)
    mid = r.astype(BF16)
    lo = (r - mid.astype(F32)).astype(BF16)
    return hi, mid, lo


def _layer_norm(r, g, b):
    mu = jnp.mean(r, axis=-1, keepdims=True)
    d = r - mu
    var = jnp.mean(d * d, axis=-1, keepdims=True)
    return d * lax.rsqrt(var + LN_EPS) * g + b


def _mod_kernel(c_ref, w_ref, b_ref, o_ref):
    cs = _silu(c_ref[...])
    o_ref[0] = _dot(cs, w_ref[0]) + b_ref[0]


def _modulation(c_all, w_mod, b_mod):
    depth, d, n3 = w_mod.shape
    rows = c_all.shape[0]
    nb = d
    return pl.pallas_call(
        _mod_kernel,
        grid=(depth, n3 // nb),
        in_specs=[
            pl.BlockSpec((rows, d), lambda l, j: (0, 0)),
            pl.BlockSpec((1, d, nb), lambda l, j: (l, 0, j)),
            pl.BlockSpec((1, 1, nb), lambda l, j: (l, 0, j)),
        ],
        out_specs=pl.BlockSpec((1, rows, nb), lambda l, j: (l, 0, j)),
        out_shape=jax.ShapeDtypeStruct((depth, rows, n3), F32),
        compiler_params=pltpu.CompilerParams(
            dimension_semantics=("arbitrary", "arbitrary"), vmem_limit_bytes=VMEM_LIMIT),
        name="modulation",
    )(c_all, w_mod, b_mod.reshape(depth, 1, n3))


def _modulate(x_ref, mod_ref):
    x = x_ref[...]
    shift = mod_ref[:, 0:1, :]
    scale = mod_ref[:, 1:2, :]
    return x, x * (1.0 + scale) + shift


def _init_conv_halo(halo_scr, st_ref, width):
    halo_scr[...] = jnp.zeros(halo_scr.shape, F32)
    if st_ref is not None:
        halo_scr[:, HALO - (width - 1):HALO, :] = st_ref[...]


def _causal_conv(halo_scr, nst_ref, p, wconv_ref, width, sl):
    bb, tt, cw = p.shape
    ext = jnp.concatenate([halo_scr[:, :, sl], p], axis=1)
    halo_scr[:, :, sl] = ext[:, tt:, :]
    nst_ref[:, :, sl] = ext[:, HALO + tt - (width - 1):, :]
    e = ext.reshape(bb * (HALO + tt), cw)
    e1 = pltpu.roll(e, 1, axis=0)
    tap = lambda k: wconv_ref[width - 1 - k:width - k, sl]
    y = None
    for m in range((width + 1) // 2):
        pair = e * tap(2 * m)
        if 2 * m + 1 < width:
            pair = pair + e1 * tap(2 * m + 1)
        if m:
            pair = pltpu.roll(pair, 2 * m, axis=0)
        y = pair if y is None else y + pair
    return y.reshape(bb, HALO + tt, cw)[:, HALO:, :].reshape(bb * tt, cw)


def _residual_norm(x, out, mod_ref, lng_ref, lnb_ref, o_ref, alpha):
    bb, tt, d = o_ref.shape
    gate = mod_ref[:, 2:3, :]
    r = alpha * x.reshape(bb, tt, d) + gate * out.reshape(bb, tt, d)
    o_ref[...] = _layer_norm(r, lng_ref[...], lnb_ref[...])


def _const_spec(shape):
    zeros = (0,) * len(shape)
    return pl.BlockSpec(shape, lambda *_: zeros, pipeline_mode=pl.Buffered(1))


def _conv_layer_kernel(*refs, has_state, width, col_w, alpha):
    if has_state:
        (x_ref, mod_ref, st_ref, win_ref, wconv_ref, wout_ref, lng_ref, lnb_ref,
         o_ref, nst_ref, halo_scr, y_scr) = refs
    else:
        (x_ref, mod_ref, win_ref, wconv_ref, wout_ref, lng_ref, lnb_ref,
         o_ref, nst_ref, halo_scr, y_scr) = refs
        st_ref = None
    bb, tt, d = x_ref.shape
    rows = bb * tt
    dc = wconv_ref.shape[1]

    @pl.when(pl.program_id(1) == 0)
    def _():
        _init_conv_halo(halo_scr, st_ref, width)

    x, u = _modulate(x_ref, mod_ref)
    u = u.reshape(rows, d).astype(BF16)

    for j in range(dc // col_w):
        sl = slice(j * col_w, (j + 1) * col_w)
        pc = _dot(u, win_ref[:, dc + j * col_w:dc + (j + 1) * col_w])
        ph = _dot(u, win_ref[:, 2 * dc + j * col_w:2 * dc + (j + 1) * col_w])
        conv = _causal_conv(halo_scr, nst_ref, (pc * ph).reshape(bb, tt, col_w), wconv_ref, width, sl)
        pb = _dot(u, win_ref[:, sl])
        pz = _dot(u, win_ref[:, 3 * dc + j * col_w:3 * dc + (j + 1) * col_w])
        y_scr[:, sl] = (pb * conv * _silu(pz)).astype(BF16)

    out = _dot(y_scr[...], wout_ref[...])
    _residual_norm(x, out, mod_ref, lng_ref, lnb_ref, o_ref, alpha)


def _conv_layer(x, mod, state, w_in, w_conv, w_out, ln_g, ln_b, *, bb, tt, alpha):
    bsz, seq, d = x.shape
    width, dc = w_conv.shape
    has_state = state is not None
    rows = bb * tt
    seq_spec = lambda n: pl.BlockSpec((bb, n, d if n == 3 else dc), lambda b, t: (b, 0, 0))
    in_specs = [pl.BlockSpec((bb, tt, d), lambda b, t: (b, t, 0)), seq_spec(3)]
    args = [x, mod]
    if has_state:
        in_specs.append(seq_spec(width - 1))
        args.append(state)
    in_specs += [_const_spec(w_in.shape), _const_spec(w_conv.shape), _const_spec(w_out.shape),
                 _const_spec((1, d)), _const_spec((1, d))]
    args += [w_in, w_conv, w_out, ln_g.reshape(1, d), ln_b.reshape(1, d)]
    kern = functools.partial(_conv_layer_kernel, has_state=has_state, width=width,
                             col_w=min(dc, 256), alpha=alpha)
    return pl.pallas_call(
        kern,
        grid=(bsz // bb, seq // tt),
        in_specs=in_specs,
        out_specs=[pl.BlockSpec((bb, tt, d), lambda b, t: (b, t, 0)), seq_spec(width - 1)],
        out_shape=[jax.ShapeDtypeStruct((bsz, seq, d), F32),
                   jax.ShapeDtypeStruct((bsz, width - 1, dc), F32)],
        scratch_shapes=[pltpu.VMEM((bb, HALO, dc), F32), pltpu.VMEM((rows, dc), BF16)],
        compiler_params=pltpu.CompilerParams(
            dimension_semantics=("arbitrary", "arbitrary"), vmem_limit_bytes=VMEM_LIMIT),
        name="conv_layer",
    )(*args)


def _tri_masks(n, seq_len):
    i = lax.broadcasted_iota(jnp.int32, (n, n), 0)
    j = lax.broadcasted_iota(jnp.int32, (n, n), 1)
    same = (i // seq_len) == (j // seq_len)
    causal = same & (i >= j)
    strict = same & (i > j)
    eye = (i == j).astype(F32)
    levels = []
    s = 1
    while s < seq_len:
        levels.append(((i // (2 * s)) == (j // (2 * s))) & ((i % (2 * s)) >= s) & ((j % (2 * s)) < s))
        s *= 2
    return causal, strict, eye, levels


def _gdn_layer_kernel(*refs, has_state, width, n_heads, dk, dv, seq_len, col_w, group, alpha):
    if has_state:
        (x_ref, mod_ref, cst_ref, s0_ref, win_ref, wtail_ref, wconv_ref, alog_ref, dtb_ref, nw_ref,
         wout_ref, lng_ref, lnb_ref, o_ref, ncst_ref, ns_ref,
         halo_scr, wtail_scr, qkv_scr, z_scr, gt_scr, gx_scr, kb_scr, u_scr, wq_scr, attn_scr,
         oh_scr) = refs
        s_scr = None
    else:
        (x_ref, mod_ref, win_ref, wtail_ref, wconv_ref, alog_ref, dtb_ref, nw_ref,
         wout_ref, lng_ref, lnb_ref, o_ref, ncst_ref, ns_ref,
         halo_scr, wtail_scr, qkv_scr, z_scr, gt_scr, gx_scr, kb_scr, u_scr, wq_scr, attn_scr,
         oh_scr, s_scr) = refs
        cst_ref = s0_ref = None
    bb, tt, d = x_ref.shape
    rows = bb * tt
    ch = wconv_ref.shape[1]
    qk = n_heads * dk
    vw = n_heads * dv
    n = min(GDN_ROWS, rows)
    span = min(seq_len, n)
    n_seq = n // span
    n_chunks = rows // n
    heads = range(n_heads)
    q_sl = lambda h: slice(h * dk, (h + 1) * dk)
    k_sl = lambda h: slice(qk + h * dk, qk + (h + 1) * dk)
    v_sl = lambda h: slice(2 * qk + h * dv, 2 * qk + (h + 1) * dv)
    o_sl = lambda h: slice(h * dv, (h + 1) * dv)

    @pl.when(pl.program_id(1) == 0)
    def _():
        _init_conv_halo(halo_scr, cst_ref, width)
        if s_scr is not None:
            s_scr[...] = jnp.zeros(s_scr.shape, F32)

    @pl.when((pl.program_id(0) == 0) & (pl.program_id(1) == 0))
    def _():
        wtail_scr[...] = jnp.zeros(wtail_scr.shape, BF16)
        wtail_scr[:, 0:wtail_ref.shape[1]] = wtail_ref[...].astype(BF16)

    x, u = _modulate(x_ref, mod_ref)
    u = u.reshape(rows, d).astype(BF16)

    ab = _dot(u, wtail_scr[...])
    lane = lax.broadcasted_iota(jnp.int32, (rows, LANES), 1)
    a = ab + dtb_ref[...]
    softplus = jnp.maximum(a, 0.0) + jnp.log(1.0 + jnp.exp(-jnp.abs(a)))
    g = jnp.where((lane >= n_heads) & (lane < 2 * n_heads),
                  -jnp.exp(alog_ref[...]) * softplus, 0.0)
    beta = jax.nn.sigmoid(ab)
    ri = lax.broadcasted_iota(jnp.int32, (rows, rows), 0)
    rj = lax.broadcasted_iota(jnp.int32, (rows, rows), 1)
    tri = (((ri // span) == (rj // span)) & (ri >= rj)).astype(BF16)
    g_hi, g_mid, g_lo = _split3(g)
    g_cum = (jnp.dot(tri, g_hi, preferred_element_type=F32)
             + jnp.dot(tri, g_mid, preferred_element_type=F32)
             + jnp.dot(tri, g_lo, preferred_element_type=F32))
    for c in range(n_chunks):
        gt_scr[c] = g_cum[c * n:(c + 1) * n, :].T
    for h in heads:
        gx_scr[:, o_sl(h)] = jnp.broadcast_to(g_cum[:, n_heads + h:n_heads + h + 1], (rows, dk))
        kb_scr[:, o_sl(h)] = jnp.broadcast_to(beta[:, h:h + 1], (rows, dk))

    for j in range(ch // col_w):
        sl = slice(j * col_w, (j + 1) * col_w)
        p = _dot(u, win_ref[:, sl]).reshape(bb, tt, col_w)
        qkv_scr[:, sl] = _silu(_causal_conv(halo_scr, ncst_ref, p, wconv_ref, width, sl))

    z_scr[...] = _silu(_dot(u, win_ref[:, ch:ch + vw]))

    for h in heads:
        t = qkv_scr[:, q_sl(h)]
        qkv_scr[:, q_sl(h)] = t * (lax.rsqrt(jnp.sum(t * t, axis=-1, keepdims=True) + NORM_EPS)
                                   * dk ** -0.5)
        t = qkv_scr[:, k_sl(h)]
        t = t * lax.rsqrt(jnp.sum(t * t, axis=-1, keepdims=True) + NORM_EPS)
        qkv_scr[:, k_sl(h)] = t
        beta_x = kb_scr[:, o_sl(h)]
        qkv_scr[:, v_sl(h)] = qkv_scr[:, v_sl(h)] * beta_x
        kb_scr[:, o_sl(h)] = t * beta_x

    causal, strict, eye, levels = _tri_masks(n, span)
    row_seq = lax.broadcasted_iota(jnp.int32, (n, 1), 0) // span

    def local_body(gi, carry):
        items = []
        for ci in range(group):
            c = gi * group + ci
            rs = pl.ds(pl.multiple_of(c * n, n), n)
            items += [(c, rs, h) for h in heads]
        decay, a_mat, t_inv = [], [], []
        for c, rs, h in items:
            g_row = gt_scr[c, n_heads + h:n_heads + h + 1, :]
            diff = gx_scr[rs, h * dv:h * dv + n] - g_row
            decay.append(jnp.where(causal, jnp.exp(jnp.where(causal, diff, 0.0)), 0.0))
        for (c, rs, h), dec in zip(items, decay):
            k = qkv_scr[rs, k_sl(h)]
            a_mat.append(jnp.where(strict, _dot_nt(kb_scr[rs, o_sl(h)], k) * dec, 0.0))
        for (c, rs, h), dec in zip(items, decay):
            attn_scr[rs, h * dv:h * dv + n] = (
                _dot_nt(qkv_scr[rs, q_sl(h)], qkv_scr[rs, k_sl(h)]) * dec)
        t_inv = [eye - jnp.where(levels[0], a, 0.0) for a in a_mat]
        for li, m in enumerate(levels[1:]):
            s = 2 << li
            if s % SUBLANES:
                y = [_dot(jnp.where(m, a, 0.0), t) for a, t in zip(a_mat, t_inv)]
                xs = [_dot(t, yy) for t, yy in zip(t_inv, y)]
                t_inv = [t - xx for t, xx in zip(t_inv, xs)]
                continue
            lo = [(b * 2 * s + s, b * 2 * s + 2 * s) for b in range(n // (2 * s))]
            take = lambda x: jnp.concatenate([x[r0:r1] for r0, r1 in lo], axis=0)
            def put(xc, rest):
                parts = []
                for b, (r0, r1) in enumerate(lo):
                    parts += [rest(r0 - s, r0), xc[b * s:(b + 1) * s]]
                return jnp.concatenate(parts, axis=0)
            zeros = jnp.zeros((s, n), F32)
            y = [put(_dot(take(jnp.where(m, a, 0.0)), t), lambda r0, r1: zeros)
                 for a, t in zip(a_mat, t_inv)]
            xs = [_dot(take(t), yy) for t, yy in zip(t_inv, y)]
            t_inv = [put(take(t) - xx, lambda r0, r1, t=t: t[r0:r1]) for t, xx in zip(t_inv, xs)]
        for (c, rs, h), t in zip(items, t_inv):
            e_x = jnp.exp(gx_scr[rs, o_sl(h)])
            rhs = jnp.concatenate([qkv_scr[rs, v_sl(h)], kb_scr[rs, o_sl(h)] * e_x], axis=1)
            sol = _dot(t, rhs)
            u_scr[rs, o_sl(h)] = sol[:, :dv]
            wq_scr[c, 0:n, o_sl(h)] = sol[:, dv:]
            wq_scr[c, n:2 * n, o_sl(h)] = qkv_scr[rs, q_sl(h)] * e_x
        return carry

    lax.fori_loop(0, n_chunks // group, local_body, 0)

    chunks_per_seq = tt // n if not has_state else n_chunks

    def state_body(c, carry):
        if has_state:
            lanes = [(0, s) for s in range(n_seq)]
            rows_of = lambda q: slice(c * n, (c + 1) * n)
            chunk_of = lambda q: c
        else:
            lanes = [(q, 0) for q in range(bb)]
            rows_of = lambda q: slice(q * tt + c * n, q * tt + (c + 1) * n)
            chunk_of = lambda q: q * chunks_per_seq + c
        groups = sorted({q for q, _ in lanes})
        masks = [None if n_seq == 1 else (row_seq == s) for s in range(n_seq)]
        if has_state:
            s_prev = {(h, q, s): s0_ref[c * n_seq + s, h] for h in heads for q, s in lanes}
        else:
            s_prev = {(h, q, s): s_scr[q, h] for h in heads for q, s in lanes}
        def own_rows(q, s, h):
            if n_seq == 1:
                return wq_scr[chunk_of(q), :, o_sl(h)]
            return jnp.concatenate(
                [wq_scr[chunk_of(q), s * span:(s + 1) * span, o_sl(h)],
                 wq_scr[chunk_of(q), n + s * span:n + (s + 1) * span, o_sl(h)]], axis=0)
        ws = {(h, q, s): _dot(own_rows(q, s, h), s_prev[(h, q, s)])
              for h in heads for q, s in lanes}
        v_new, v_dec, o_part, g_last = {}, {}, {}, {}
        for h in heads:
            for q in groups:
                w_s = jnp.concatenate([ws[(h, q, s)][:span] for s in range(n_seq)], axis=0)
                v_new[h, q] = u_scr[rows_of(q), o_sl(h)] - w_s
                o_part[h, q] = jnp.concatenate(
                    [ws[(h, q, s)][span:] for s in range(n_seq)], axis=0)
                gx = gx_scr[rows_of(q), o_sl(h)]
                for s in range(n_seq):
                    g_last[h, q, s] = gx[(s + 1) * span - 1:(s + 1) * span, :]
                g_end = jnp.concatenate(
                    [jnp.broadcast_to(g_last[h, q, s], (span, dv)) for s in range(n_seq)], axis=0)
                v_dec[h, q] = v_new[h, q] * jnp.exp(g_end - gx)
        for h in heads:
            for q in groups:
                oh_scr[rows_of(q), o_sl(h)] = o_part[h, q] + _dot(
                    attn_scr[rows_of(q), h * dv:h * dv + n], v_new[h, q])
        for h in heads:
            for q in groups:
                vs = v_dec[h, q] if n_seq == 1 else jnp.concatenate(
                    [jnp.where(masks[s], v_dec[h, q], 0.0) for s in range(n_seq)], axis=1)
                ktv = _dot_tn(qkv_scr[rows_of(q), k_sl(h)], vs)
                for s in range(n_seq):
                    s_new = (s_prev[(h, q, s)] * jnp.exp(g_last[h, q, s])
                             + ktv[:, s * dv:(s + 1) * dv])
                    if has_state:
                        ns_ref[c * n_seq + s, h] = s_new
                    else:
                        s_scr[q, h] = s_new
        return carry

    for c in range(chunks_per_seq):
        state_body(c, 0)
    if not has_state:
        ns_ref[...] = s_scr[...]

    for h in heads:
        o = oh_scr[:, o_sl(h)]
        o = o * lax.rsqrt(jnp.mean(o * o, axis=-1, keepdims=True) + NORM_EPS) * nw_ref[...]
        oh_scr[:, o_sl(h)] = o * z_scr[:, o_sl(h)]
    out = _dot(oh_scr[...], wout_ref[...])
    _residual_norm(x, out, mod_ref, lng_ref, lnb_ref, o_ref, alpha)


def _gdn_layer(x, mod, conv_state, ssm_state, w_in, w_tail, w_conv, a_log, dt_bias, norm_w, w_out,
               ln_g, ln_b, *, bb, tt, alpha):
    bsz, seq, d = x.shape
    width, ch = w_conv.shape
    n_heads = a_log.shape[0]
    vw = w_out.shape[0]
    dv = vw // n_heads
    dk = (ch - vw) // 2 // n_heads
    qk = n_heads * dk
    has_state = conv_state is not None
    rows = bb * tt
    seq_len = min(seq, GDN_ROWS)
    n = min(GDN_ROWS, rows)
    n_chunks = rows // n
    assert dk == dv == LANES, "one head per lane tile"
    assert rows % n == 0 and n % seq_len == 0
    assert has_state == (tt == seq), "a carried state needs whole sequences per block"
    group = max(g for g in (8, 4, 2, 1) if n_chunks % g == 0)

    assert w_in.shape[1] == ch + vw and w_tail.shape[1] == 2 * n_heads <= LANES
    pad = lambda vec: jnp.zeros((1, LANES), F32).at[0, n_heads:2 * n_heads].set(vec)
    in_specs = [pl.BlockSpec((bb, tt, d), lambda b, t: (b, t, 0)),
                pl.BlockSpec((bb, 3, d), lambda b, t: (b, 0, 0))]
    args = [x, mod]
    if has_state:
        in_specs += [pl.BlockSpec((bb, width - 1, ch), lambda b, t: (b, 0, 0)),
                     pl.BlockSpec((bb, n_heads, dk, dv), lambda b, t: (b, 0, 0, 0))]
        args += [conv_state, ssm_state]
    in_specs += [_const_spec(w_in.shape), _const_spec(w_tail.shape), _const_spec(w_conv.shape),
                 _const_spec((1, LANES)), _const_spec((1, LANES)), _const_spec((1, dv)),
                 _const_spec(w_out.shape), _const_spec((1, d)), _const_spec((1, d))]
    args += [w_in, w_tail, w_conv, pad(a_log), pad(dt_bias), norm_w.reshape(1, dv), w_out,
             ln_g.reshape(1, d), ln_b.reshape(1, d)]
    scratch = [pltpu.VMEM((bb, HALO, ch), F32),
               pltpu.VMEM((d, LANES), BF16),
               pltpu.VMEM((rows, ch), F32),
               pltpu.VMEM((rows, vw), F32),
               pltpu.VMEM((n_chunks, LANES, n), F32),
               pltpu.VMEM((rows, vw), F32),
               pltpu.VMEM((rows, qk), F32),
               pltpu.VMEM((rows, vw), F32),
               pltpu.VMEM((n_chunks, 2 * n, qk), F32),
               pltpu.VMEM((rows, vw), F32),
               pltpu.VMEM((rows, vw), F32)]
    if not has_state:
        scratch.append(pltpu.VMEM((bb, n_heads, dk, dv), F32))
    kern = functools.partial(_gdn_layer_kernel, has_state=has_state, width=width, n_heads=n_heads,
                             dk=dk, dv=dv, seq_len=seq_len, col_w=min(ch, 512), group=group,
                             alpha=alpha)
    return pl.pallas_call(
        kern,
        grid=(bsz // bb, seq // tt),
        in_specs=in_specs,
        out_specs=[pl.BlockSpec((bb, tt, d), lambda b, t: (b, t, 0)),
                   pl.BlockSpec((bb, width - 1, ch), lambda b, t: (b, 0, 0)),
                   pl.BlockSpec((bb, n_heads, dk, dv), lambda b, t: (b, 0, 0, 0))],
        out_shape=[jax.ShapeDtypeStruct((bsz, seq, d), F32),
                   jax.ShapeDtypeStruct((bsz, width - 1, ch), F32),
                   jax.ShapeDtypeStruct((bsz, n_heads, dk, dv), F32)],
        scratch_shapes=scratch,
        compiler_params=pltpu.CompilerParams(
            dimension_semantics=("arbitrary", "arbitrary"), vmem_limit_bytes=VMEM_LIMIT),
        name="gdn_layer",
    )(*args)


def _block_rows(bsz, seq, target, long_seqs=1):
    if seq * long_seqs >= target:
        return long_seqs, target // long_seqs
    return max(1, min(bsz, target // seq)), seq


def _trunk(x, mod, conv_a, conv_b, ssm_b, ln_g, ln_b, wa, wb, alpha):
    bsz, seq, _ = x.shape
    new_a, new_cb, new_s = [], [], []
    depth = mod.shape[0]
    for l in range(depth):
        i = l // 2
        if l % 2 == 0:
            bb, tt = _block_rows(bsz, seq, 1024)
            x, nb = _conv_layer(x, mod[l], None if conv_a is None else conv_a[i],
                                wa["in"][i], wa["conv"][i], wa["out"][i], ln_g[l], ln_b[l],
                                bb=bb, tt=tt, alpha=alpha)
            new_a.append(nb)
        else:
            bb, tt = _block_rows(bsz, seq, *((512, 2) if conv_b is None else (2 * GDN_ROWS,)))
            x, nb, s = _gdn_layer(x, mod[l], None if conv_b is None else conv_b[i],
                                  None if ssm_b is None else ssm_b[i],
                                  wb["in"][i], wb["tail"][i], wb["conv"][i], wb["a_log"][i],
                                  wb["dt_bias"][i], wb["norm"][i], wb["out"][i], ln_g[l], ln_b[l],
                                  bb=bb, tt=tt, alpha=alpha)
            new_cb.append(nb)
            new_s.append(s)
    return x, jnp.stack(new_a), jnp.stack(new_cb), jnp.stack(new_s)


def kernel(x_prompt, x_sample, state_conv_a, state_conv_b, state_ssm_b, c_prompt, c_sample, w_mod, b_mod, ln_g, ln_b, wa_in, wa_conv, wa_out, wb_in, wb_conv, wb_a_log, wb_dt_bias, wb_norm, wb_out):
    depth, d, _ = w_mod.shape
    alpha = (2 * depth) ** 0.25
    bp = x_prompt.shape[0]

    mod = _modulation(jnp.concatenate([c_prompt, c_sample], axis=0), w_mod, b_mod)
    mod = mod.reshape(depth, -1, 3, d)
    mod_p, mod_s = mod[:, :bp], mod[:, bp:]

    wa = {"in": wa_in.astype(BF16), "conv": wa_conv, "out": wa_out.astype(BF16)}
    n_main = wb_conv.shape[2] + wb_out.shape[1]
    wb = {"in": wb_in[:, :, :n_main].astype(BF16), "tail": wb_in[:, :, n_main:], "conv": wb_conv,
          "a_log": wb_a_log, "dt_bias": wb_dt_bias, "norm": wb_norm, "out": wb_out.astype(BF16)}

    y_p, ca_p, cb_p, s_p = _trunk(x_prompt, mod_p, None, None, None, ln_g, ln_b, wa, wb, alpha)
    y_s, ca_s, cb_s, s_s = _trunk(x_sample, mod_s, state_conv_a, state_conv_b, state_ssm_b,
                                  ln_g, ln_b, wa, wb, alpha)
    return (y_p, y_s, ca_p, cb_p, s_p, ca_s, cb_s, s_s)
```

```python
import functools

import jax
import jax.numpy as jnp
from jax import lax
from jax.experimental import pallas as pl
from jax.experimental.pallas import tpu as pltpu

F32 = jnp.float32
BF16 = jnp.bfloat16

LN_EPS = 1e-5
NORM_EPS = 1e-6
SUBLANES = 8
LANES = 128
HALO = SUBLANES
GDN_ROWS = 64
VMEM_LIMIT = 60 * 1024 * 1024


def _dot(a, b):
    return jnp.dot(a.astype(BF16), b.astype(BF16), preferred_element_type=F32)


def _dot_nt(a, b):
    return lax.dot_general(a.astype(BF16), b.astype(BF16), (((1,), (1,)), ((), ())),
                           preferred_element_type=F32)


def _dot_tn(a, b):
    return lax.dot_general(a.astype(BF16), b.astype(BF16), (((0,), (0,)), ((), ())),
                           preferred_element_type=F32)


def _silu(x):
    return x * jax.nn.sigmoid(x)


def _split3(x):
    hi = x.astype(BF16)
    r = x - hi.astype(F32)
    mid = r.astype(BF16)
    lo = (r - mid.astype(F32)).astype(BF16)
    return hi, mid, lo


def _layer_norm(r, g, b):
    mu = jnp.mean(r, axis=-1, keepdims=True)
    d = r - mu
    var = jnp.mean(d * d, axis=-1, keepdims=True)
    return d * lax.rsqrt(var + LN_EPS) * g + b


def _mod_kernel(c_ref, w_ref, b_ref, o_ref):
    cs = _silu(c_ref[...])
    o_ref[0] = _dot(cs, w_ref[0]) + b_ref[0]


def _modulation(c_all, w_mod, b_mod):
    depth, d, n3 = w_mod.shape
    rows = c_all.shape[0]
    nb = n3
    return pl.pallas_call(
        _mod_kernel,
        grid=(depth, n3 // nb),
        in_specs=[
            pl.BlockSpec((rows, d), lambda l, j: (0, 0)),
            pl.BlockSpec((1, d, nb), lambda l, j: (l, 0, j)),
            pl.BlockSpec((1, 1, nb), lambda l, j: (l, 0, j)),
        ],
        out_specs=pl.BlockSpec((1, rows, nb), lambda l, j: (l, 0, j)),
        out_shape=jax.ShapeDtypeStruct((depth, rows, n3), F32),
        compiler_params=pltpu.CompilerParams(
            dimension_semantics=("arbitrary", "arbitrary"), vmem_limit_bytes=VMEM_LIMIT),
        name="modulation",
    )(c_all, w_mod, b_mod.reshape(depth, 1, n3))


def _modulate(x_ref, mod_ref):
    x = x_ref[...]
    shift = mod_ref[:, 0:1, :]
    scale = mod_ref[:, 1:2, :]
    return x, x * (1.0 + scale) + shift


def _init_conv_halo(halo_scr, st_ref, width):
    halo_scr[...] = jnp.zeros(halo_scr.shape, F32)
    if st_ref is not None:
        halo_scr[:, HALO - (width - 1):HALO, :] = st_ref[...]


def _causal_conv(halo_scr, nst_ref, p, wconv_ref, width, sl):
    bb, tt, cw = p.shape
    ext = jnp.concatenate([halo_scr[:, :, sl], p], axis=1)
    halo_scr[:, :, sl] = ext[:, tt:, :]
    nst_ref[:, :, sl] = ext[:, HALO + tt - (width - 1):, :]
    e = ext.reshape(bb * (HALO + tt), cw)
    e1 = pltpu.roll(e, 1, axis=0)
    tap = lambda k: wconv_ref[width - 1 - k:width - k, sl]
    y = None
    for m in range((width + 1) // 2):
        pair = e * tap(2 * m)
        if 2 * m + 1 < width:
            pair = pair + e1 * tap(2 * m + 1)
        if m:
            pair = pltpu.roll(pair, 2 * m, axis=0)
        y = pair if y is None else y + pair
    return y.reshape(bb, HALO + tt, cw)[:, HALO:, :].reshape(bb * tt, cw)


def _residual_norm(x, out, mod_ref, lng_ref, lnb_ref, o_ref, alpha):
    bb, tt, d = o_ref.shape
    gate = mod_ref[:, 2:3, :]
    r = alpha * x.reshape(bb, tt, d) + gate * out.reshape(bb, tt, d)
    o_ref[...] = _layer_norm(r, lng_ref[...], lnb_ref[...])


def _const_spec(shape):
    zeros = (0,) * len(shape)
    return pl.BlockSpec(shape, lambda *_: zeros, pipeline_mode=pl.Buffered(1))


def _conv_layer_kernel(*refs, has_state, width, col_w, alpha):
    if has_state:
        (x_ref, mod_ref, st_ref, win_ref, wconv_ref, wout_ref, lng_ref, lnb_ref,
         o_ref, nst_ref, halo_scr, y_scr) = refs
    else:
        (x_ref, mod_ref, win_ref, wconv_ref, wout_ref, lng_ref, lnb_ref,
         o_ref, nst_ref, halo_scr, y_scr) = refs
        st_ref = None
    bb, tt, d = x_ref.shape
    rows = bb * tt
    dc = wconv_ref.shape[1]

    @pl.when(pl.program_id(1) == 0)
    def _():
        _init_conv_halo(halo_scr, st_ref, width)

    x, u = _modulate(x_ref, mod_ref)
    u = u.reshape(rows, d).astype(BF16)

    for j in range(dc // col_w):
        sl = slice(j * col_w, (j + 1) * col_w)
        pc = _dot(u, win_ref[:, dc + j * col_w:dc + (j + 1) * col_w])
        ph = _dot(u, win_ref[:, 2 * dc + j * col_w:2 * dc + (j + 1) * col_w])
        conv = _causal_conv(halo_scr, nst_ref, (pc * ph).reshape(bb, tt, col_w), wconv_ref, width, sl)
        pb = _dot(u, win_ref[:, sl])
        pz = _dot(u, win_ref[:, 3 * dc + j * col_w:3 * dc + (j + 1) * col_w])
        y_scr[:, sl] = (pb * conv * _silu(pz)).astype(BF16)

    out = _dot(y_scr[...], wout_ref[...])
    _residual_norm(x, out, mod_ref, lng_ref, lnb_ref, o_ref, alpha)


def _conv_layer(x, mod, state, w_in, w_conv, w_out, ln_g, ln_b, *, bb, tt, alpha):
    bsz, seq, d = x.shape
    width, dc = w_conv.shape
    has_state = state is not None
    rows = bb * tt
    seq_spec = lambda n: pl.BlockSpec((bb, n, d if n == 3 else dc), lambda b, t: (b, 0, 0))
    in_specs = [pl.BlockSpec((bb, tt, d), lambda b, t: (b, t, 0)), seq_spec(3)]
    args = [x, mod]
    if has_state:
        in_specs.append(seq_spec(width - 1))
        args.append(state)
    in_specs += [_const_spec(w_in.shape), _const_spec(w_conv.shape), _const_spec(w_out.shape),
                 _const_spec((1, d)), _const_spec((1, d))]
    args += [w_in, w_conv, w_out, ln_g.reshape(1, d), ln_b.reshape(1, d)]
    kern = functools.partial(_conv_layer_kernel, has_state=has_state, width=width,
                             col_w=min(dc, 256), alpha=alpha)
    return pl.pallas_call(
        kern,
        grid=(bsz // bb, seq // tt),
        in_specs=in_specs,
        out_specs=[pl.BlockSpec((bb, tt, d), lambda b, t: (b, t, 0)), seq_spec(width - 1)],
        out_shape=[jax.ShapeDtypeStruct((bsz, seq, d), F32),
                   jax.ShapeDtypeStruct((bsz, width - 1, dc), F32)],
        scratch_shapes=[pltpu.VMEM((bb, HALO, dc), F32), pltpu.VMEM((rows, dc), BF16)],
        compiler_params=pltpu.CompilerParams(
            dimension_semantics=("arbitrary", "arbitrary"), vmem_limit_bytes=VMEM_LIMIT),
        name="conv_layer",
    )(*args)


def _tri_masks(n, seq_len):
    i = lax.broadcasted_iota(jnp.int32, (n, n), 0)
    j = lax.broadcasted_iota(jnp.int32, (n, n), 1)
    same = (i // seq_len) == (j // seq_len)
    causal = same & (i >= j)
    strict = same & (i > j)
    eye = (i == j).astype(F32)
    levels = []
    s = 1
    while s < seq_len:
        levels.append(((i // (2 * s)) == (j // (2 * s))) & ((i % (2 * s)) >= s) & ((j % (2 * s)) < s))
        s *= 2
    return causal, strict, eye, levels


def _gdn_layer_kernel(*refs, has_state, width, n_heads, dk, dv, seq_len, col_w, group, alpha):
    if has_state:
        (x_ref, mod_ref, cst_ref, s0_ref, win_ref, wconv_ref, alog_ref, dtb_ref, nw_ref,
         wout_ref, lng_ref, lnb_ref, o_ref, ncst_ref, ns_ref,
         halo_scr, wtail_scr, qkv_scr, z_scr, gt_scr, gx_scr, kb_scr, u_scr, wq_scr, attn_scr,
         oh_scr) = refs
        s_scr = None
    else:
        (x_ref, mod_ref, win_ref, wconv_ref, alog_ref, dtb_ref, nw_ref,
         wout_ref, lng_ref, lnb_ref, o_ref, ncst_ref, ns_ref,
         halo_scr, wtail_scr, qkv_scr, z_scr, gt_scr, gx_scr, kb_scr, u_scr, wq_scr, attn_scr,
         oh_scr, s_scr) = refs
        cst_ref = s0_ref = None
    bb, tt, d = x_ref.shape
    rows = bb * tt
    ch = wconv_ref.shape[1]
    qk = n_heads * dk
    vw = n_heads * dv
    n = min(GDN_ROWS, rows)
    span = min(seq_len, n)
    n_seq = n // span
    n_chunks = rows // n
    heads = range(n_heads)
    q_sl = lambda h: slice(h * dk, (h + 1) * dk)
    k_sl = lambda h: slice(qk + h * dk, qk + (h + 1) * dk)
    v_sl = lambda h: slice(2 * qk + h * dv, 2 * qk + (h + 1) * dv)
    o_sl = lambda h: slice(h * dv, (h + 1) * dv)

    @pl.when(pl.program_id(1) == 0)
    def _():
        _init_conv_halo(halo_scr, cst_ref, width)
        if s_scr is not None:
            s_scr[...] = jnp.zeros(s_scr.shape, F32)

    @pl.when((pl.program_id(0) == 0) & (pl.program_id(1) == 0))
    def _():
        tail = win_ref.shape[1] - (ch + vw)
        wtail_scr[...] = jnp.zeros(wtail_scr.shape, BF16)
        wtail_scr[:, 0:tail] = win_ref[:, ch + vw:]

    x, u = _modulate(x_ref, mod_ref)
    u = u.reshape(rows, d).astype(BF16)

    ab = _dot(u, wtail_scr[...])
    lane = lax.broadcasted_iota(jnp.int32, (rows, LANES), 1)
    a = ab + dtb_ref[...]
    softplus = jnp.maximum(a, 0.0) + jnp.log(1.0 + jnp.exp(-jnp.abs(a)))
    g = jnp.where((lane >= n_heads) & (lane < 2 * n_heads),
                  -jnp.exp(alog_ref[...]) * softplus, 0.0)
    beta = jax.nn.sigmoid(ab)
    ri = lax.broadcasted_iota(jnp.int32, (rows, rows), 0)
    rj = lax.broadcasted_iota(jnp.int32, (rows, rows), 1)
    tri = (((ri // span) == (rj // span)) & (ri >= rj)).astype(BF16)
    g_hi, g_mid, g_lo = _split3(g)
    g_cum = (jnp.dot(tri, g_hi, preferred_element_type=F32)
             + jnp.dot(tri, g_mid, preferred_element_type=F32)
             + jnp.dot(tri, g_lo, preferred_element_type=F32))
    for c in range(n_chunks):
        gt_scr[c] = g_cum[c * n:(c + 1) * n, :].T
    for h in heads:
        gx_scr[:, o_sl(h)] = jnp.broadcast_to(g_cum[:, n_heads + h:n_heads + h + 1], (rows, dk))
        kb_scr[:, o_sl(h)] = jnp.broadcast_to(beta[:, h:h + 1], (rows, dk))

    for j in range(ch // col_w):
        sl = slice(j * col_w, (j + 1) * col_w)
        p = _dot(u, win_ref[:, sl]).reshape(bb, tt, col_w)
        qkv_scr[:, sl] = _silu(_causal_conv(halo_scr, ncst_ref, p, wconv_ref, width, sl))

    z_scr[...] = _silu(_dot(u, win_ref[:, ch:ch + vw]))

    for h in heads:
        t = qkv_scr[:, q_sl(h)]
        qkv_scr[:, q_sl(h)] = t * (lax.rsqrt(jnp.sum(t * t, axis=-1, keepdims=True) + NORM_EPS)
                                   * dk ** -0.5)
        t = qkv_scr[:, k_sl(h)]
        t = t * lax.rsqrt(jnp.sum(t * t, axis=-1, keepdims=True) + NORM_EPS)
        qkv_scr[:, k_sl(h)] = t
        beta_x = kb_scr[:, o_sl(h)]
        qkv_scr[:, v_sl(h)] = qkv_scr[:, v_sl(h)] * beta_x
        kb_scr[:, o_sl(h)] = t * beta_x

    causal, strict, eye, levels = _tri_masks(n, span)
    row_seq = lax.broadcasted_iota(jnp.int32, (n, 1), 0) // span

    def local_body(gi, carry):
        items = []
        for ci in range(group):
            c = gi * group + ci
            rs = pl.ds(pl.multiple_of(c * n, n), n)
            items += [(c, rs, h) for h in heads]
        decay, a_mat, t_inv = [], [], []
        for c, rs, h in items:
            g_row = gt_scr[c, n_heads + h:n_heads + h + 1, :]
            diff = gx_scr[rs, h * dv:h * dv + n] - g_row
            decay.append(jnp.where(causal, jnp.exp(jnp.where(causal, diff, 0.0)), 0.0))
        for (c, rs, h), dec in zip(items, decay):
            k = qkv_scr[rs, k_sl(h)]
            a_mat.append(jnp.where(strict, _dot_nt(kb_scr[rs, o_sl(h)], k) * dec, 0.0))
        for (c, rs, h), dec in zip(items, decay):
            attn_scr[rs, h * dv:h * dv + n] = (
                _dot_nt(qkv_scr[rs, q_sl(h)], qkv_scr[rs, k_sl(h)]) * dec)
        t_inv = [eye - jnp.where(levels[0], a, 0.0) for a in a_mat]
        for li, m in enumerate(levels[1:]):
            s = 2 << li
            if s % SUBLANES:
                y = [_dot(jnp.where(m, a, 0.0), t) for a, t in zip(a_mat, t_inv)]
                xs = [_dot(t, yy) for t, yy in zip(t_inv, y)]
                t_inv = [t - xx for t, xx in zip(t_inv, xs)]
                continue
            lo = [(b * 2 * s + s, b * 2 * s + 2 * s) for b in range(n // (2 * s))]
            take = lambda x: jnp.concatenate([x[r0:r1] for r0, r1 in lo], axis=0)
            def put(xc, rest):
                parts = []
                for b, (r0, r1) in enumerate(lo):
                    parts += [rest(r0 - s, r0), xc[b * s:(b + 1) * s]]
                return jnp.concatenate(parts, axis=0)
            zeros = jnp.zeros((s, n), F32)
            y = [put(_dot(take(jnp.where(m, a, 0.0)), t), lambda r0, r1: zeros)
                 for a, t in zip(a_mat, t_inv)]
            xs = [_dot(take(t), yy) for t, yy in zip(t_inv, y)]
            t_inv = [put(take(t) - xx, lambda r0, r1, t=t: t[r0:r1]) for t, xx in zip(t_inv, xs)]
        for (c, rs, h), t in zip(items, t_inv):
            e_x = jnp.exp(gx_scr[rs, o_sl(h)])
            rhs = jnp.concatenate([qkv_scr[rs, v_sl(h)], kb_scr[rs, o_sl(h)] * e_x], axis=1)
            sol = _dot(t, rhs)
            u_scr[rs, o_sl(h)] = sol[:, :dv]
            wq_scr[c, 0:n, o_sl(h)] = sol[:, dv:]
            wq_scr[c, n:2 * n, o_sl(h)] = qkv_scr[rs, q_sl(h)] * e_x
        return carry

    lax.fori_loop(0, n_chunks // group, local_body, 0)

    chunks_per_seq = tt // n if not has_state else n_chunks

    def state_body(c, carry):
        if has_state:
            lanes = [(0, s) for s in range(n_seq)]
            rows_of = lambda q: slice(c * n, (c + 1) * n)
            chunk_of = lambda q: c
        else:
            lanes = [(q, 0) for q in range(bb)]
            rows_of = lambda q: slice(q * tt + c * n, q * tt + (c + 1) * n)
            chunk_of = lambda q: q * chunks_per_seq + c
        groups = sorted({q for q, _ in lanes})
        masks = [None if n_seq == 1 else (row_seq == s) for s in range(n_seq)]
        if has_state:
            s_prev = {(h, q, s): s0_ref[c * n_seq + s, h] for h in heads for q, s in lanes}
        else:
            s_prev = {(h, q, s): s_scr[q, h] for h in heads for q, s in lanes}
        def own_rows(q, s, h):
            if n_seq == 1:
                return wq_scr[chunk_of(q), :, o_sl(h)]
            return jnp.concatenate(
                [wq_scr[chunk_of(q), s * span:(s + 1) * span, o_sl(h)],
                 wq_scr[chunk_of(q), n + s * span:n + (s + 1) * span, o_sl(h)]], axis=0)
        ws = {(h, q, s): _dot(own_rows(q, s, h), s_prev[(h, q, s)])
              for h in heads for q, s in lanes}
        v_new, v_dec, o_part, g_last = {}, {}, {}, {}
        for h in heads:
            for q in groups:
                w_s = jnp.concatenate([ws[(h, q, s)][:span] for s in range(n_seq)], axis=0)
                v_new[h, q] = u_scr[rows_of(q), o_sl(h)] - w_s
                o_part[h, q] = jnp.concatenate(
                    [ws[(h, q, s)][span:] for s in range(n_seq)], axis=0)
                gx = gx_scr[rows_of(q), o_sl(h)]
                for s in range(n_seq):
                    g_last[h, q, s] = gx[(s + 1) * span - 1:(s + 1) * span, :]
                g_end = jnp.concatenate(
                    [jnp.broadcast_to(g_last[h, q, s], (span, dv)) for s in range(n_seq)], axis=0)
                v_dec[h, q] = v_new[h, q] * jnp.exp(g_end - gx)
        for h in heads:
            for q in groups:
                oh_scr[rows_of(q), o_sl(h)] = o_part[h, q] + _dot(
                    attn_scr[rows_of(q), h * dv:h * dv + n], v_new[h, q])
        for h in heads:
            for q in groups:
                vs = v_dec[h, q] if n_seq == 1 else jnp.concatenate(
                    [jnp.where(masks[s], v_dec[h, q], 0.0) for s in range(n_seq)], axis=1)
                ktv = _dot_tn(qkv_scr[rows_of(q), k_sl(h)], vs)
                for s in range(n_seq):
                    s_new = (s_prev[(h, q, s)] * jnp.exp(g_last[h, q, s])
                             + ktv[:, s * dv:(s + 1) * dv])
                    if has_state:
                        ns_ref[c * n_seq + s, h] = s_new
                    else:
                        s_scr[q, h] = s_new
        return carry

    for c in range(chunks_per_seq):
        state_body(c, 0)
    if not has_state:
        ns_ref[...] = s_scr[...]

    for h in heads:
        o = oh_scr[:, o_sl(h)]
        o = o * lax.rsqrt(jnp.mean(o * o, axis=-1, keepdims=True) + NORM_EPS) * nw_ref[...]
        oh_scr[:, o_sl(h)] = o * z_scr[:, o_sl(h)]
    out = _dot(oh_scr[...], wout_ref[...])
    _residual_norm(x, out, mod_ref, lng_ref, lnb_ref, o_ref, alpha)


def _gdn_layer(x, mod, conv_state, ssm_state, w_in, w_conv, a_log, dt_bias, norm_w, w_out,
               ln_g, ln_b, *, bb, tt, alpha):
    bsz, seq, d = x.shape
    width, ch = w_conv.shape
    n_heads = a_log.shape[0]
    vw = w_out.shape[0]
    dv = vw // n_heads
    dk = (ch - vw) // 2 // n_heads
    qk = n_heads * dk
    has_state = conv_state is not None
    rows = bb * tt
    seq_len = min(seq, GDN_ROWS)
    n = min(GDN_ROWS, rows)
    n_chunks = rows // n
    assert dk == dv == LANES, "one head per lane tile"
    assert rows % n == 0 and n % seq_len == 0
    assert has_state == (tt == seq), "a carried state needs whole sequences per block"
    group = max(g for g in (8, 4, 2, 1) if n_chunks % g == 0)

    assert w_in.shape[1] == ch + vw + 2 * n_heads and 2 * n_heads <= LANES
    pad = lambda vec: jnp.zeros((1, LANES), F32).at[0, n_heads:2 * n_heads].set(vec)
    in_specs = [pl.BlockSpec((bb, tt, d), lambda b, t: (b, t, 0)),
                pl.BlockSpec((bb, 3, d), lambda b, t: (b, 0, 0))]
    args = [x, mod]
    if has_state:
        in_specs += [pl.BlockSpec((bb, width - 1, ch), lambda b, t: (b, 0, 0)),
                     pl.BlockSpec((bb, n_heads, dk, dv), lambda b, t: (b, 0, 0, 0))]
        args += [conv_state, ssm_state]
    in_specs += [_const_spec(w_in.shape), _const_spec(w_conv.shape),
                 _const_spec((1, LANES)), _const_spec((1, LANES)), _const_spec((1, dv)),
                 _const_spec(w_out.shape), _const_spec((1, d)), _const_spec((1, d))]
    args += [w_in, w_conv, pad(a_log), pad(dt_bias), norm_w.reshape(1, dv), w_out,
             ln_g.reshape(1, d), ln_b.reshape(1, d)]
    scratch = [pltpu.VMEM((bb, HALO, ch), F32),
               pltpu.VMEM((d, LANES), BF16),
               pltpu.VMEM((rows, ch), F32),
               pltpu.VMEM((rows, vw), F32),
               pltpu.VMEM((n_chunks, LANES, n), F32),
               pltpu.VMEM((rows, vw), F32),
               pltpu.VMEM((rows, qk), F32),
               pltpu.VMEM((rows, vw), F32),
               pltpu.VMEM((n_chunks, 2 * n, qk), F32),
               pltpu.VMEM((rows, vw), F32),
               pltpu.VMEM((rows, vw), F32)]
    if not has_state:
        scratch.append(pltpu.VMEM((bb, n_heads, dk, dv), F32))
    kern = functools.partial(_gdn_layer_kernel, has_state=has_state, width=width, n_heads=n_heads,
                             dk=dk, dv=dv, seq_len=seq_len, col_w=min(ch, 512), group=group,
                             alpha=alpha)
    return pl.pallas_call(
        kern,
        grid=(bsz // bb, seq // tt),
        in_specs=in_specs,
        out_specs=[pl.BlockSpec((bb, tt, d), lambda b, t: (b, t, 0)),
                   pl.BlockSpec((bb, width - 1, ch), lambda b, t: (b, 0, 0)),
                   pl.BlockSpec((bb, n_heads, dk, dv), lambda b, t: (b, 0, 0, 0))],
        out_shape=[jax.ShapeDtypeStruct((bsz, seq, d), F32),
                   jax.ShapeDtypeStruct((bsz, width - 1, ch), F32),
                   jax.ShapeDtypeStruct((bsz, n_heads, dk, dv), F32)],
        scratch_shapes=scratch,
        compiler_params=pltpu.CompilerParams(
            dimension_semantics=("arbitrary", "arbitrary"), vmem_limit_bytes=VMEM_LIMIT),
        name="gdn_layer",
    )(*args)


def _block_rows(bsz, seq, target, long_seqs=1):
    if seq * long_seqs >= target:
        return long_seqs, target // long_seqs
    return max(1, min(bsz, target // seq)), seq


def _trunk(x, mod, conv_a, conv_b, ssm_b, ln_g, ln_b, wa, wb, alpha):
    bsz, seq, _ = x.shape
    new_a, new_cb, new_s = [], [], []
    depth = mod.shape[0]
    for l in range(depth):
        i = l // 2
        if l % 2 == 0:
            bb, tt = _block_rows(bsz, seq, 1024 if conv_a is None else 512)
            x, nb = _conv_layer(x, mod[l], None if conv_a is None else conv_a[i],
                                wa["in"][i], wa["conv"][i], wa["out"][i], ln_g[l], ln_b[l],
                                bb=bb, tt=tt, alpha=alpha)
            new_a.append(nb)
        else:
            bb, tt = _block_rows(bsz, seq, *((512, 2) if conv_b is None else (2 * GDN_ROWS,)))
            x, nb, s = _gdn_layer(x, mod[l], None if conv_b is None else conv_b[i],
                                  None if ssm_b is None else ssm_b[i],
                                  wb["in"][i], wb["conv"][i], wb["a_log"][i],
                                  wb["dt_bias"][i], wb["norm"][i], wb["out"][i], ln_g[l], ln_b[l],
                                  bb=bb, tt=tt, alpha=alpha)
            new_cb.append(nb)
            new_s.append(s)
    return x, jnp.stack(new_a), jnp.stack(new_cb), jnp.stack(new_s)


def kernel(x_prompt, x_sample, state_conv_a, state_conv_b, state_ssm_b, c_prompt, c_sample, w_mod, b_mod, ln_g, ln_b, wa_in, wa_conv, wa_out, wb_in, wb_conv, wb_a_log, wb_dt_bias, wb_norm, wb_out):
    depth, d, _ = w_mod.shape
    alpha = (2 * depth) ** 0.25
    bp = x_prompt.shape[0]

    mod = _modulation(jnp.concatenate([c_prompt, c_sample], axis=0), w_mod, b_mod)
    mod = mod.reshape(depth, -1, 3, d)
    mod_p, mod_s = mod[:, :bp], mod[:, bp:]

    wa = {"in": wa_in.astype(BF16), "conv": wa_conv, "out": wa_out.astype(BF16)}
    wb = {"in": wb_in.astype(BF16), "conv": wb_conv,
          "a_log": wb_a_log, "dt_bias": wb_dt_bias, "norm": wb_norm, "out": wb_out.astype(BF16)}

    y_p, ca_p, cb_p, s_p = _trunk(x_prompt, mod_p, None, None, None, ln_g, ln_b, wa, wb, alpha)
    y_s, ca_s, cb_s, s_s = _trunk(x_sample, mod_s, state_conv_a, state_conv_b, state_ssm_b,
                                  ln_g, ln_b, wa, wb, alpha)
    return (y_p, y_s, ca_p, cb_p, s_p, ca_s, cb_s, s_s)
```

```python
import functools

import jax
import jax.numpy as jnp
from jax import lax
from jax.experimental import pallas as pl
from jax.experimental.pallas import tpu as pltpu

F32 = jnp.float32
BF16 = jnp.bfloat16

LN_EPS = 1e-5
NORM_EPS = 1e-6
SUBLANES = 8
LANES = 128
HALO = SUBLANES
GDN_ROWS = 64
VMEM_LIMIT = 60 * 1024 * 1024


def _dot(a, b):
    return jnp.dot(a.astype(BF16), b.astype(BF16), preferred_element_type=F32)


def _dot_nt(a, b):
    return lax.dot_general(a.astype(BF16), b.astype(BF16), (((1,), (1,)), ((), ())),
                           preferred_element_type=F32)


def _dot_tn(a, b):
    return lax.dot_general(a.astype(BF16), b.astype(BF16), (((0,), (0,)), ((), ())),
                           preferred_element_type=F32)


def _silu(x):
    return x * jax.nn.sigmoid(x)


def _split3(x):
    hi = x.astype(BF16)
    r = x - hi.astype(F32)
    mid = r.astype(BF16)
    lo = (r - mid.astype(F32)).astype(BF16)
    return hi, mid, lo


def _layer_norm(r, g, b):
    mu = jnp.mean(r, axis=-1, keepdims=True)
    d = r - mu
    var = jnp.mean(d * d, axis=-1, keepdims=True)
    return d * lax.rsqrt(var + LN_EPS) * g + b


def _mod_kernel(c_ref, w_ref, b_ref, o_ref):
    cs = _silu(c_ref[...])
    o_ref[0] = _dot(cs, w_ref[0]) + b_ref[0]


def _modulation(c_all, w_mod, b_mod):
    depth, d, n3 = w_mod.shape
    rows = c_all.shape[0]
    nb = n3
    return pl.pallas_call(
        _mod_kernel,
        grid=(depth, n3 // nb),
        in_specs=[
            pl.BlockSpec((rows, d), lambda l, j: (0, 0)),
            pl.BlockSpec((1, d, nb), lambda l, j: (l, 0, j)),
            pl.BlockSpec((1, 1, nb), lambda l, j: (l, 0, j)),
        ],
        out_specs=pl.BlockSpec((1, rows, nb), lambda l, j: (l, 0, j)),
        out_shape=jax.ShapeDtypeStruct((depth, rows, n3), F32),
        compiler_params=pltpu.CompilerParams(
            dimension_semantics=("arbitrary", "arbitrary"), vmem_limit_bytes=VMEM_LIMIT),
        name="modulation",
    )(c_all, w_mod, b_mod.reshape(depth, 1, n3))


def _modulate(x_ref, mod_ref):
    x = x_ref[...]
    shift = mod_ref[:, 0:1, :]
    scale = mod_ref[:, 1:2, :]
    return x, x * (1.0 + scale) + shift


def _init_conv_halo(halo_scr, st_ref, width):
    halo_scr[...] = jnp.zeros(halo_scr.shape, F32)
    if st_ref is not None:
        halo_scr[:, HALO - (width - 1):HALO, :] = st_ref[...]


def _causal_conv(halo_scr, nst_ref, p, wconv_ref, width, sl):
    bb, tt, cw = p.shape
    ext = jnp.concatenate([halo_scr[:, :, sl], p], axis=1)
    halo_scr[:, :, sl] = ext[:, tt:, :]
    nst_ref[:, :, sl] = ext[:, HALO + tt - (width - 1):, :]
    e = ext.reshape(bb * (HALO + tt), cw)
    e1 = pltpu.roll(e, 1, axis=0)
    tap = lambda k: wconv_ref[width - 1 - k:width - k, sl]
    y = None
    for m in range((width + 1) // 2):
        pair = e * tap(2 * m)
        if 2 * m + 1 < width:
            pair = pair + e1 * tap(2 * m + 1)
        if m:
            pair = pltpu.roll(pair, 2 * m, axis=0)
        y = pair if y is None else y + pair
    return y.reshape(bb, HALO + tt, cw)[:, HALO:, :].reshape(bb * tt, cw)


def _residual_norm(x, out, mod_ref, lng_ref, lnb_ref, o_ref, alpha):
    bb, tt, d = o_ref.shape
    gate = mod_ref[:, 2:3, :]
    r = alpha * x.reshape(bb, tt, d) + gate * out.reshape(bb, tt, d)
    o_ref[...] = _layer_norm(r, lng_ref[...], lnb_ref[...])


def _const_spec(shape):
    zeros = (0,) * len(shape)
    return pl.BlockSpec(shape, lambda *_: zeros, pipeline_mode=pl.Buffered(1))


def _conv_layer_kernel(*refs, has_state, width, col_w, alpha):
    if has_state:
        (x_ref, mod_ref, st_ref, win_ref, wconv_ref, wout_ref, lng_ref, lnb_ref,
         o_ref, nst_ref, halo_scr, y_scr) = refs
    else:
        (x_ref, mod_ref, win_ref, wconv_ref, wout_ref, lng_ref, lnb_ref,
         o_ref, nst_ref, halo_scr, y_scr) = refs
        st_ref = None
    bb, tt, d = x_ref.shape
    rows = bb * tt
    dc = wconv_ref.shape[1]

    @pl.when(pl.program_id(1) == 0)
    def _():
        _init_conv_halo(halo_scr, st_ref, width)

    x, u = _modulate(x_ref, mod_ref)
    u = u.reshape(rows, d).astype(BF16)

    for j in range(dc // col_w):
        sl = slice(j * col_w, (j + 1) * col_w)
        pc = _dot(u, win_ref[:, dc + j * col_w:dc + (j + 1) * col_w])
        ph = _dot(u, win_ref[:, 2 * dc + j * col_w:2 * dc + (j + 1) * col_w])
        conv = _causal_conv(halo_scr, nst_ref, (pc * ph).reshape(bb, tt, col_w), wconv_ref, width, sl)
        pb = _dot(u, win_ref[:, sl])
        pz = _dot(u, win_ref[:, 3 * dc + j * col_w:3 * dc + (j + 1) * col_w])
        y_scr[:, sl] = (pb * conv * _silu(pz)).astype(BF16)

    out = _dot(y_scr[...], wout_ref[...])
    _residual_norm(x, out, mod_ref, lng_ref, lnb_ref, o_ref, alpha)


def _conv_layer(x, mod, state, w_in, w_conv, w_out, ln_g, ln_b, *, bb, tt, alpha):
    bsz, seq, d = x.shape
    width, dc = w_conv.shape
    has_state = state is not None
    rows = bb * tt
    seq_spec = lambda n: pl.BlockSpec((bb, n, d if n == 3 else dc), lambda b, t: (b, 0, 0))
    in_specs = [pl.BlockSpec((bb, tt, d), lambda b, t: (b, t, 0)), seq_spec(3)]
    args = [x, mod]
    if has_state:
        in_specs.append(seq_spec(width - 1))
        args.append(state)
    in_specs += [_const_spec(w_in.shape), _const_spec(w_conv.shape), _const_spec(w_out.shape),
                 _const_spec((1, d)), _const_spec((1, d))]
    args += [w_in, w_conv, w_out, ln_g.reshape(1, d), ln_b.reshape(1, d)]
    kern = functools.partial(_conv_layer_kernel, has_state=has_state, width=width,
                             col_w=min(dc, 256), alpha=alpha)
    return pl.pallas_call(
        kern,
        grid=(bsz // bb, seq // tt),
        in_specs=in_specs,
        out_specs=[pl.BlockSpec((bb, tt, d), lambda b, t: (b, t, 0)), seq_spec(width - 1)],
        out_shape=[jax.ShapeDtypeStruct((bsz, seq, d), F32),
                   jax.ShapeDtypeStruct((bsz, width - 1, dc), F32)],
        scratch_shapes=[pltpu.VMEM((bb, HALO, dc), F32), pltpu.VMEM((rows, dc), BF16)],
        compiler_params=pltpu.CompilerParams(
            dimension_semantics=("arbitrary", "arbitrary"), vmem_limit_bytes=VMEM_LIMIT),
        name="conv_layer",
    )(*args)


def _tri_masks(n, seq_len):
    i = lax.broadcasted_iota(jnp.int32, (n, n), 0)
    j = lax.broadcasted_iota(jnp.int32, (n, n), 1)
    same = (i // seq_len) == (j // seq_len)
    causal = same & (i >= j)
    strict = same & (i > j)
    eye = (i == j).astype(F32)
    levels = []
    s = 1
    while s < seq_len:
        levels.append(((i // (2 * s)) == (j // (2 * s))) & ((i % (2 * s)) >= s) & ((j % (2 * s)) < s))
        s *= 2
    return causal, strict, eye, levels


def _gdn_layer_kernel(*refs, has_state, width, n_heads, dk, dv, seq_len, col_w, group, alpha):
    if has_state:
        (x_ref, mod_ref, cst_ref, s0_ref, win_ref, wconv_ref, alog_ref, dtb_ref, nw_ref,
         wout_ref, lng_ref, lnb_ref, o_ref, ncst_ref, ns_ref,
         halo_scr, wtail_scr, qkv_scr, z_scr, gt_scr, gx_scr, kb_scr, u_scr, wq_scr, attn_scr,
         oh_scr) = refs
        s_scr = None
    else:
        (x_ref, mod_ref, win_ref, wconv_ref, alog_ref, dtb_ref, nw_ref,
         wout_ref, lng_ref, lnb_ref, o_ref, ncst_ref, ns_ref,
         halo_scr, wtail_scr, qkv_scr, z_scr, gt_scr, gx_scr, kb_scr, u_scr, wq_scr, attn_scr,
         oh_scr, s_scr) = refs
        cst_ref = s0_ref = None
    bb, tt, d = x_ref.shape
    rows = bb * tt
    ch = wconv_ref.shape[1]
    qk = n_heads * dk
    vw = n_heads * dv
    n = min(GDN_ROWS, rows)
    span = min(seq_len, n)
    n_seq = n // span
    n_chunks = rows // n
    heads = range(n_heads)
    q_sl = lambda h: slice(h * dk, (h + 1) * dk)
    k_sl = lambda h: slice(qk + h * dk, qk + (h + 1) * dk)
    v_sl = lambda h: slice(2 * qk + h * dv, 2 * qk + (h + 1) * dv)
    o_sl = lambda h: slice(h * dv, (h + 1) * dv)

    @pl.when(pl.program_id(1) == 0)
    def _():
        _init_conv_halo(halo_scr, cst_ref, width)
        if s_scr is not None:
            s_scr[...] = jnp.zeros(s_scr.shape, F32)

    @pl.when((pl.program_id(0) == 0) & (pl.program_id(1) == 0))
    def _():
        tail = win_ref.shape[1] - (ch + vw)
        wtail_scr[...] = jnp.zeros(wtail_scr.shape, BF16)
        wtail_scr[:, 0:tail] = win_ref[:, ch + vw:]

    x, u = _modulate(x_ref, mod_ref)
    u = u.reshape(rows, d).astype(BF16)

    ab = _dot(u, wtail_scr[...])
    lane = lax.broadcasted_iota(jnp.int32, (rows, LANES), 1)
    a = ab + dtb_ref[...]
    softplus = jnp.maximum(a, 0.0) + jnp.log(1.0 + jnp.exp(-jnp.abs(a)))
    g = jnp.where((lane >= n_heads) & (lane < 2 * n_heads),
                  -jnp.exp(alog_ref[...]) * softplus, 0.0)
    beta = jax.nn.sigmoid(ab)
    ri = lax.broadcasted_iota(jnp.int32, (rows, rows), 0)
    rj = lax.broadcasted_iota(jnp.int32, (rows, rows), 1)
    tri = (((ri // span) == (rj // span)) & (ri >= rj)).astype(BF16)
    g_hi, g_mid, g_lo = _split3(g)
    g_cum = (jnp.dot(tri, g_hi, preferred_element_type=F32)
             + jnp.dot(tri, g_mid, preferred_element_type=F32)
             + jnp.dot(tri, g_lo, preferred_element_type=F32))
    for c in range(n_chunks):
        gt_scr[c] = g_cum[c * n:(c + 1) * n, :].T
    for h in heads:
        gx_scr[:, o_sl(h)] = jnp.broadcast_to(g_cum[:, n_heads + h:n_heads + h + 1], (rows, dk))
        kb_scr[:, o_sl(h)] = jnp.broadcast_to(beta[:, h:h + 1], (rows, dk))

    for j in range(ch // col_w):
        sl = slice(j * col_w, (j + 1) * col_w)
        p = _dot(u, win_ref[:, sl]).reshape(bb, tt, col_w)
        qkv_scr[:, sl] = _silu(_causal_conv(halo_scr, ncst_ref, p, wconv_ref, width, sl))

    z_scr[...] = _silu(_dot(u, win_ref[:, ch:ch + vw]))

    for h in heads:
        t = qkv_scr[:, q_sl(h)]
        qkv_scr[:, q_sl(h)] = t * (lax.rsqrt(jnp.sum(t * t, axis=-1, keepdims=True) + NORM_EPS)
                                   * dk ** -0.5)
        t = qkv_scr[:, k_sl(h)]
        t = t * lax.rsqrt(jnp.sum(t * t, axis=-1, keepdims=True) + NORM_EPS)
        qkv_scr[:, k_sl(h)] = t
        beta_x = kb_scr[:, o_sl(h)]
        qkv_scr[:, v_sl(h)] = qkv_scr[:, v_sl(h)] * beta_x
        kb_scr[:, o_sl(h)] = t * beta_x

    causal, strict, eye, levels = _tri_masks(n, span)
    row_seq = lax.broadcasted_iota(jnp.int32, (n, 1), 0) // span

    def local_body(gi, carry):
        items = []
        for ci in range(group):
            c = gi * group + ci
            rs = pl.ds(pl.multiple_of(c * n, n), n)
            items += [(c, rs, h) for h in heads]
        decay, a_mat, t_inv = [], [], []
        for c, rs, h in items:
            g_row = gt_scr[c, n_heads + h:n_heads + h + 1, :]
            diff = gx_scr[rs, h * dv:h * dv + n] - g_row
            decay.append(jnp.where(causal, jnp.exp(jnp.where(causal, diff, 0.0)), 0.0))
        for (c, rs, h), dec in zip(items, decay):
            k = qkv_scr[rs, k_sl(h)]
            a_mat.append(jnp.where(strict, _dot_nt(kb_scr[rs, o_sl(h)], k) * dec, 0.0))
        for (c, rs, h), dec in zip(items, decay):
            attn_scr[rs, h * dv:h * dv + n] = (
                _dot_nt(qkv_scr[rs, q_sl(h)], qkv_scr[rs, k_sl(h)]) * dec)
        t_inv = [eye - jnp.where(levels[0], a, 0.0) for a in a_mat]
        for li, m in enumerate(levels[1:]):
            s = 2 << li
            if s % SUBLANES:
                y = [_dot(jnp.where(m, a, 0.0), t) for a, t in zip(a_mat, t_inv)]
                xs = [_dot(t, yy) for t, yy in zip(t_inv, y)]
                t_inv = [t - xx for t, xx in zip(t_inv, xs)]
                continue
            lo = [(b * 2 * s + s, b * 2 * s + 2 * s) for b in range(n // (2 * s))]
            take = lambda x: jnp.concatenate([x[r0:r1] for r0, r1 in lo], axis=0)
            def put(xc, rest):
                parts = []
                for b, (r0, r1) in enumerate(lo):
                    parts += [rest(r0 - s, r0), xc[b * s:(b + 1) * s]]
                return jnp.concatenate(parts, axis=0)
            zeros = jnp.zeros((s, n), F32)
            y = [put(_dot(take(jnp.where(m, a, 0.0)), t), lambda r0, r1: zeros)
                 for a, t in zip(a_mat, t_inv)]
            xs = [_dot(take(t), yy) for t, yy in zip(t_inv, y)]
            t_inv = [put(take(t) - xx, lambda r0, r1, t=t: t[r0:r1]) for t, xx in zip(t_inv, xs)]
        for (c, rs, h), t in zip(items, t_inv):
            e_x = jnp.exp(gx_scr[rs, o_sl(h)])
            rhs = jnp.concatenate([qkv_scr[rs, v_sl(h)], kb_scr[rs, o_sl(h)] * e_x], axis=1)
            sol = _dot(t, rhs)
            u_scr[rs, o_sl(h)] = sol[:, :dv]
            wq_scr[c, 0:n, o_sl(h)] = sol[:, dv:]
            wq_scr[c, n:2 * n, o_sl(h)] = qkv_scr[rs, q_sl(h)] * e_x
        return carry

    lax.fori_loop(0, n_chunks // group, local_body, 0)

    chunks_per_seq = tt // n if not has_state else n_chunks

    def state_body(c, carry):
        if has_state:
            lanes = [(0, s) for s in range(n_seq)]
            rows_of = lambda q: slice(c * n, (c + 1) * n)
            chunk_of = lambda q: c
        else:
            lanes = [(q, 0) for q in range(bb)]
            rows_of = lambda q: slice(q * tt + c * n, q * tt + (c + 1) * n)
            chunk_of = lambda q: q * chunks_per_seq + c
        groups = sorted({q for q, _ in lanes})
        masks = [None if n_seq == 1 else (row_seq == s) for s in range(n_seq)]
        if has_state:
            s_prev = {(h, q, s): s0_ref[c * n_seq + s, h] for h in heads for q, s in lanes}
        else:
            s_prev = {(h, q, s): s_scr[q, h] if c == 0 else live_state[h, q]
                      for h in heads for q, s in lanes}
        def own_rows(q, s, h):
            if n_seq == 1:
                return wq_scr[chunk_of(q), :, o_sl(h)]
            return jnp.concatenate(
                [wq_scr[chunk_of(q), s * span:(s + 1) * span, o_sl(h)],
                 wq_scr[chunk_of(q), n + s * span:n + (s + 1) * span, o_sl(h)]], axis=0)
        ws = {(h, q, s): _dot(own_rows(q, s, h), s_prev[(h, q, s)])
              for h in heads for q, s in lanes}
        v_new, v_dec, o_part, g_last = {}, {}, {}, {}
        for h in heads:
            for q in groups:
                w_s = jnp.concatenate([ws[(h, q, s)][:span] for s in range(n_seq)], axis=0)
                v_new[h, q] = u_scr[rows_of(q), o_sl(h)] - w_s
                o_part[h, q] = jnp.concatenate(
                    [ws[(h, q, s)][span:] for s in range(n_seq)], axis=0)
                gx = gx_scr[rows_of(q), o_sl(h)]
                for s in range(n_seq):
                    g_last[h, q, s] = gx[(s + 1) * span - 1:(s + 1) * span, :]
                g_end = jnp.concatenate(
                    [jnp.broadcast_to(g_last[h, q, s], (span, dv)) for s in range(n_seq)], axis=0)
                v_dec[h, q] = v_new[h, q] * jnp.exp(g_end - gx)
        for h in heads:
            for q in groups:
                oh_scr[rows_of(q), o_sl(h)] = o_part[h, q] + _dot(
                    attn_scr[rows_of(q), h * dv:h * dv + n], v_new[h, q])
        for h in heads:
            for q in groups:
                vs = v_dec[h, q] if n_seq == 1 else jnp.concatenate(
                    [jnp.where(masks[s], v_dec[h, q], 0.0) for s in range(n_seq)], axis=1)
                ktv = _dot_tn(qkv_scr[rows_of(q), k_sl(h)], vs)
                for s in range(n_seq):
                    s_new = (s_prev[(h, q, s)] * jnp.exp(g_last[h, q, s])
                             + ktv[:, s * dv:(s + 1) * dv])
                    if has_state:
                        ns_ref[c * n_seq + s, h] = s_new
                    elif c == chunks_per_seq - 1:
                        s_scr[q, h] = s_new
                    else:
                        live_state[h, q] = s_new
        return carry

    live_state = {}
    for c in range(chunks_per_seq):
        state_body(c, 0)
    if not has_state:
        ns_ref[...] = s_scr[...]

    for h in heads:
        o = oh_scr[:, o_sl(h)]
        o = o * lax.rsqrt(jnp.mean(o * o, axis=-1, keepdims=True) + NORM_EPS) * nw_ref[...]
        oh_scr[:, o_sl(h)] = o * z_scr[:, o_sl(h)]
    out = _dot(oh_scr[...], wout_ref[...])
    _residual_norm(x, out, mod_ref, lng_ref, lnb_ref, o_ref, alpha)


def _gdn_layer(x, mod, conv_state, ssm_state, w_in, w_conv, a_log, dt_bias, norm_w, w_out,
               ln_g, ln_b, *, bb, tt, alpha):
    bsz, seq, d = x.shape
    width, ch = w_conv.shape
    n_heads = a_log.shape[0]
    vw = w_out.shape[0]
    dv = vw // n_heads
    dk = (ch - vw) // 2 // n_heads
    qk = n_heads * dk
    has_state = conv_state is not None
    rows = bb * tt
    seq_len = min(seq, GDN_ROWS)
    n = min(GDN_ROWS, rows)
    n_chunks = rows // n
    assert dk == dv == LANES, "one head per lane tile"
    assert rows % n == 0 and n % seq_len == 0
    assert has_state == (tt == seq), "a carried state needs whole sequences per block"
    group = max(g for g in (8, 4, 2, 1) if n_chunks % g == 0)

    assert w_in.shape[1] == ch + vw + 2 * n_heads and 2 * n_heads <= LANES
    pad = lambda vec: jnp.zeros((1, LANES), F32).at[0, n_heads:2 * n_heads].set(vec)
    in_specs = [pl.BlockSpec((bb, tt, d), lambda b, t: (b, t, 0)),
                pl.BlockSpec((bb, 3, d), lambda b, t: (b, 0, 0))]
    args = [x, mod]
    if has_state:
        in_specs += [pl.BlockSpec((bb, width - 1, ch), lambda b, t: (b, 0, 0)),
                     pl.BlockSpec((bb, n_heads, dk, dv), lambda b, t: (b, 0, 0, 0))]
        args += [conv_state, ssm_state]
    in_specs += [_const_spec(w_in.shape), _const_spec(w_conv.shape),
                 _const_spec((1, LANES)), _const_spec((1, LANES)), _const_spec((1, dv)),
                 _const_spec(w_out.shape), _const_spec((1, d)), _const_spec((1, d))]
    args += [w_in, w_conv, pad(a_log), pad(dt_bias), norm_w.reshape(1, dv), w_out,
             ln_g.reshape(1, d), ln_b.reshape(1, d)]
    scratch = [pltpu.VMEM((bb, HALO, ch), F32),
               pltpu.VMEM((d, LANES), BF16),
               pltpu.VMEM((rows, ch), F32),
               pltpu.VMEM((rows, vw), F32),
               pltpu.VMEM((n_chunks, LANES, n), F32),
               pltpu.VMEM((rows, vw), F32),
               pltpu.VMEM((rows, qk), F32),
               pltpu.VMEM((rows, vw), F32),
               pltpu.VMEM((n_chunks, 2 * n, qk), F32),
               pltpu.VMEM((rows, vw), F32),
               pltpu.VMEM((rows, vw), F32)]
    if not has_state:
        scratch.append(pltpu.VMEM((bb, n_heads, dk, dv), F32))
    kern = functools.partial(_gdn_layer_kernel, has_state=has_state, width=width, n_heads=n_heads,
                             dk=dk, dv=dv, seq_len=seq_len, col_w=min(ch, 512), group=group,
                             alpha=alpha)
    return pl.pallas_call(
        kern,
        grid=(bsz // bb, seq // tt),
        in_specs=in_specs,
        out_specs=[pl.BlockSpec((bb, tt, d), lambda b, t: (b, t, 0)),
                   pl.BlockSpec((bb, width - 1, ch), lambda b, t: (b, 0, 0)),
                   pl.BlockSpec((bb, n_heads, dk, dv), lambda b, t: (b, 0, 0, 0))],
        out_shape=[jax.ShapeDtypeStruct((bsz, seq, d), F32),
                   jax.ShapeDtypeStruct((bsz, width - 1, ch), F32),
                   jax.ShapeDtypeStruct((bsz, n_heads, dk, dv), F32)],
        scratch_shapes=scratch,
        compiler_params=pltpu.CompilerParams(
            dimension_semantics=("arbitrary", "arbitrary"), vmem_limit_bytes=VMEM_LIMIT),
        name="gdn_layer",
    )(*args)


def _block_rows(bsz, seq, target, long_seqs=1):
    if seq * long_seqs >= target:
        return long_seqs, target // long_seqs
    return max(1, min(bsz, target // seq)), seq


def _trunk(x, mod, conv_a, conv_b, ssm_b, ln_g, ln_b, wa, wb, alpha):
    bsz, seq, _ = x.shape
    new_a, new_cb, new_s = [], [], []
    depth = mod.shape[0]
    for l in range(depth):
        i = l // 2
        if l % 2 == 0:
            bb, tt = _block_rows(bsz, seq, 1024 if conv_a is None else 512)
            x, nb = _conv_layer(x, mod[l], None if conv_a is None else conv_a[i],
                                wa["in"][i], wa["conv"][i], wa["out"][i], ln_g[l], ln_b[l],
                                bb=bb, tt=tt, alpha=alpha)
            new_a.append(nb)
        else:
            bb, tt = _block_rows(bsz, seq, *((512, 2) if conv_b is None else (2 * GDN_ROWS,)))
            x, nb, s = _gdn_layer(x, mod[l], None if conv_b is None else conv_b[i],
                                  None if ssm_b is None else ssm_b[i],
                                  wb["in"][i], wb["conv"][i], wb["a_log"][i],
                                  wb["dt_bias"][i], wb["norm"][i], wb["out"][i], ln_g[l], ln_b[l],
                                  bb=bb, tt=tt, alpha=alpha)
            new_cb.append(nb)
            new_s.append(s)
    return x, jnp.stack(new_a), jnp.stack(new_cb), jnp.stack(new_s)


def kernel(x_prompt, x_sample, state_conv_a, state_conv_b, state_ssm_b, c_prompt, c_sample, w_mod, b_mod, ln_g, ln_b, wa_in, wa_conv, wa_out, wb_in, wb_conv, wb_a_log, wb_dt_bias, wb_norm, wb_out):
    depth, d, _ = w_mod.shape
    alpha = (2 * depth) ** 0.25
    bp = x_prompt.shape[0]

    mod = _modulation(jnp.concatenate([c_prompt, c_sample], axis=0), w_mod, b_mod)
    mod = mod.reshape(depth, -1, 3, d)
    mod_p, mod_s = mod[:, :bp], mod[:, bp:]

    wa = {"in": wa_in.astype(BF16), "conv": wa_conv, "out": wa_out.astype(BF16)}
    wb = {"in": wb_in.astype(BF16), "conv": wb_conv,
          "a_log": wb_a_log, "dt_bias": wb_dt_bias, "norm": wb_norm, "out": wb_out.astype(BF16)}

    y_p, ca_p, cb_p, s_p = _trunk(x_prompt, mod_p, None, None, None, ln_g, ln_b, wa, wb, alpha)
    y_s, ca_s, cb_s, s_s = _trunk(x_sample, mod_s, state_conv_a, state_conv_b, state_ssm_b,
                                  ln_g, ln_b, wa, wb, alpha)
    return (y_p, y_s, ca_p, cb_p, s_p, ca_s, cb_s, s_s)
```

```python
import functools

import jax
import jax.numpy as jnp
from jax import lax
from jax.experimental import pallas as pl
from jax.experimental.pallas import tpu as pltpu

F32 = jnp.float32
BF16 = jnp.bfloat16

LN_EPS = 1e-5
NORM_EPS = 1e-6
SUBLANES = 8
LANES = 128
HALO = SUBLANES
GDN_ROWS = 64
VMEM_LIMIT = 60 * 1024 * 1024


def _dot(a, b):
    return jnp.dot(a.astype(BF16), b.astype(BF16), preferred_element_type=F32)


def _dot_nt(a, b):
    return lax.dot_general(a.astype(BF16), b.astype(BF16), (((1,), (1,)), ((), ())),
                           preferred_element_type=F32)


def _dot_tn(a, b):
    return lax.dot_general(a.astype(BF16), b.astype(BF16), (((0,), (0,)), ((), ())),
                           preferred_element_type=F32)


def _silu(x):
    return x * jax.nn.sigmoid(x)


def _split3(x):
    hi = x.astype(BF16)
    r = x - hi.astype(F32)
    mid = r.astype(BF16)
    lo = (r - mid.astype(F32)).astype(BF16)
    return hi, mid, lo


def _layer_norm(r, g, b):
    mu = jnp.mean(r, axis=-1, keepdims=True)
    d = r - mu
    var = jnp.mean(d * d, axis=-1, keepdims=True)
    return d * lax.rsqrt(var + LN_EPS) * g + b


def _mod_kernel(c_ref, w_ref, b_ref, o_ref):
    cs = _silu(c_ref[...])
    o_ref[0] = _dot(cs, w_ref[0]) + b_ref[0]


def _modulation(c_all, w_mod, b_mod):
    depth, d, n3 = w_mod.shape
    rows = c_all.shape[0]
    nb = n3
    return pl.pallas_call(
        _mod_kernel,
        grid=(depth, n3 // nb),
        in_specs=[
            pl.BlockSpec((rows, d), lambda l, j: (0, 0)),
            pl.BlockSpec((1, d, nb), lambda l, j: (l, 0, j)),
            pl.BlockSpec((1, 1, nb), lambda l, j: (l, 0, j)),
        ],
        out_specs=pl.BlockSpec((1, rows, nb), lambda l, j: (l, 0, j)),
        out_shape=jax.ShapeDtypeStruct((depth, rows, n3), F32),
        compiler_params=pltpu.CompilerParams(
            dimension_semantics=("arbitrary", "arbitrary"), vmem_limit_bytes=VMEM_LIMIT),
        name="modulation",
    )(c_all, w_mod, b_mod.reshape(depth, 1, n3))


def _modulate(x_ref, mod_ref):
    x = x_ref[...]
    shift = mod_ref[:, 0:1, :]
    scale = mod_ref[:, 1:2, :]
    return x, x * (1.0 + scale) + shift


def _init_conv_halo(halo_scr, st_ref, width):
    halo_scr[...] = jnp.zeros(halo_scr.shape, F32)
    if st_ref is not None:
        halo_scr[:, HALO - (width - 1):HALO, :] = st_ref[...]


def _causal_conv(halo_scr, nst_ref, p, wconv_ref, width, sl):
    bb, tt, cw = p.shape
    ext = jnp.concatenate([halo_scr[:, :, sl], p], axis=1)
    halo_scr[:, :, sl] = ext[:, tt:, :]
    nst_ref[:, :, sl] = ext[:, HALO + tt - (width - 1):, :]
    e = ext.reshape(bb * (HALO + tt), cw)
    e1 = pltpu.roll(e, 1, axis=0)
    tap = lambda k: wconv_ref[width - 1 - k:width - k, sl]
    y = None
    for m in range((width + 1) // 2):
        pair = e * tap(2 * m)
        if 2 * m + 1 < width:
            pair = pair + e1 * tap(2 * m + 1)
        if m:
            pair = pltpu.roll(pair, 2 * m, axis=0)
        y = pair if y is None else y + pair
    return y.reshape(bb, HALO + tt, cw)[:, HALO:, :].reshape(bb * tt, cw)


def _residual_norm(x, out, mod_ref, lng_ref, lnb_ref, o_ref, alpha):
    bb, tt, d = o_ref.shape
    gate = mod_ref[:, 2:3, :]
    r = alpha * x.reshape(bb, tt, d) + gate * out.reshape(bb, tt, d)
    o_ref[...] = _layer_norm(r, lng_ref[...], lnb_ref[...])


def _const_spec(shape):
    zeros = (0,) * len(shape)
    return pl.BlockSpec(shape, lambda *_: zeros, pipeline_mode=pl.Buffered(1))


def _conv_layer_kernel(*refs, has_state, width, col_w, alpha):
    if has_state:
        (x_ref, mod_ref, st_ref, win_ref, wconv_ref, wout_ref, lng_ref, lnb_ref,
         o_ref, nst_ref, halo_scr, y_scr) = refs
    else:
        (x_ref, mod_ref, win_ref, wconv_ref, wout_ref, lng_ref, lnb_ref,
         o_ref, nst_ref, halo_scr, y_scr) = refs
        st_ref = None
    bb, tt, d = x_ref.shape
    rows = bb * tt
    dc = wconv_ref.shape[1]

    @pl.when(pl.program_id(1) == 0)
    def _():
        _init_conv_halo(halo_scr, st_ref, width)

    x, u = _modulate(x_ref, mod_ref)
    u = u.reshape(rows, d).astype(BF16)

    for j in range(dc // col_w):
        sl = slice(j * col_w, (j + 1) * col_w)
        pc = _dot(u, win_ref[:, dc + j * col_w:dc + (j + 1) * col_w])
        ph = _dot(u, win_ref[:, 2 * dc + j * col_w:2 * dc + (j + 1) * col_w])
        conv = _causal_conv(halo_scr, nst_ref, (pc * ph).reshape(bb, tt, col_w), wconv_ref, width, sl)
        pb = _dot(u, win_ref[:, sl])
        pz = _dot(u, win_ref[:, 3 * dc + j * col_w:3 * dc + (j + 1) * col_w])
        y_scr[:, sl] = (pb * conv * _silu(pz)).astype(BF16)

    out = _dot(y_scr[...], wout_ref[...])
    _residual_norm(x, out, mod_ref, lng_ref, lnb_ref, o_ref, alpha)


def _conv_layer(x, mod, state, w_in, w_conv, w_out, ln_g, ln_b, *, bb, tt, alpha):
    bsz, seq, d = x.shape
    width, dc = w_conv.shape
    has_state = state is not None
    rows = bb * tt
    seq_spec = lambda n: pl.BlockSpec((bb, n, d if n == 3 else dc), lambda b, t: (b, 0, 0))
    in_specs = [pl.BlockSpec((bb, tt, d), lambda b, t: (b, t, 0)), seq_spec(3)]
    args = [x, mod]
    if has_state:
        in_specs.append(seq_spec(width - 1))
        args.append(state)
    in_specs += [_const_spec(w_in.shape), _const_spec(w_conv.shape), _const_spec(w_out.shape),
                 _const_spec((1, d)), _const_spec((1, d))]
    args += [w_in, w_conv, w_out, ln_g.reshape(1, d), ln_b.reshape(1, d)]
    kern = functools.partial(_conv_layer_kernel, has_state=has_state, width=width,
                             col_w=min(dc, 256), alpha=alpha)
    return pl.pallas_call(
        kern,
        grid=(bsz // bb, seq // tt),
        in_specs=in_specs,
        out_specs=[pl.BlockSpec((bb, tt, d), lambda b, t: (b, t, 0)), seq_spec(width - 1)],
        out_shape=[jax.ShapeDtypeStruct((bsz, seq, d), F32),
                   jax.ShapeDtypeStruct((bsz, width - 1, dc), F32)],
        scratch_shapes=[pltpu.VMEM((bb, HALO, dc), F32), pltpu.VMEM((rows, dc), BF16)],
        compiler_params=pltpu.CompilerParams(
            dimension_semantics=("arbitrary", "arbitrary"), vmem_limit_bytes=VMEM_LIMIT),
        name="conv_layer",
    )(*args)


def _tri_masks(n, seq_len):
    i = lax.broadcasted_iota(jnp.int32, (n, n), 0)
    j = lax.broadcasted_iota(jnp.int32, (n, n), 1)
    same = (i // seq_len) == (j // seq_len)
    causal = same & (i >= j)
    strict = same & (i > j)
    eye = (i == j).astype(F32)
    levels = []
    s = 1
    while s < seq_len:
        levels.append(((i // (2 * s)) == (j // (2 * s))) & ((i % (2 * s)) >= s) & ((j % (2 * s)) < s))
        s *= 2
    return causal, strict, eye, levels


def _gdn_layer_kernel(*refs, has_state, width, n_heads, dk, dv, seq_len, col_w, group, alpha):
    if has_state:
        (x_ref, mod_ref, cst_ref, s0_ref, win_ref, wconv_ref, alog_ref, dtb_ref, nw_ref,
         wout_ref, lng_ref, lnb_ref, o_ref, ncst_ref, ns_ref,
         halo_scr, wtail_scr, qkv_scr, z_scr, gt_scr, gx_scr, kb_scr, u_scr, wq_scr, attn_scr,
         oh_scr) = refs
        s_scr = None
    else:
        (x_ref, mod_ref, win_ref, wconv_ref, alog_ref, dtb_ref, nw_ref,
         wout_ref, lng_ref, lnb_ref, o_ref, ncst_ref, ns_ref,
         halo_scr, wtail_scr, qkv_scr, z_scr, gt_scr, gx_scr, kb_scr, u_scr, wq_scr, attn_scr,
         oh_scr, s_scr) = refs
        cst_ref = s0_ref = None
    bb, tt, d = x_ref.shape
    rows = bb * tt
    ch = wconv_ref.shape[1]
    qk = n_heads * dk
    vw = n_heads * dv
    n = min(GDN_ROWS, rows)
    span = min(seq_len, n)
    n_seq = n // span
    n_chunks = rows // n
    folded = not has_state
    heads = range(n_heads)
    q_sl = lambda h: slice(h * dk, (h + 1) * dk)
    k_sl = lambda h: slice(qk + h * dk, qk + (h + 1) * dk)
    v_sl = lambda h: slice(2 * qk + h * dv, 2 * qk + (h + 1) * dv)
    o_sl = lambda h: slice(h * dv, (h + 1) * dv)

    @pl.when(pl.program_id(1) == 0)
    def _():
        _init_conv_halo(halo_scr, cst_ref, width)
        if s_scr is not None:
            s_scr[...] = jnp.zeros(s_scr.shape, F32)

    @pl.when((pl.program_id(0) == 0) & (pl.program_id(1) == 0))
    def _():
        tail = win_ref.shape[1] - (ch + vw)
        wtail_scr[...] = jnp.zeros(wtail_scr.shape, BF16)
        wtail_scr[:, 0:tail] = win_ref[:, ch + vw:]

    x, u = _modulate(x_ref, mod_ref)
    u = u.reshape(rows, d).astype(BF16)

    ab = _dot(u, wtail_scr[...])
    lane = lax.broadcasted_iota(jnp.int32, (rows, LANES), 1)
    a = ab + dtb_ref[...]
    softplus = jnp.maximum(a, 0.0) + jnp.log(1.0 + jnp.exp(-jnp.abs(a)))
    g = jnp.where((lane >= n_heads) & (lane < 2 * n_heads),
                  -jnp.exp(alog_ref[...]) * softplus, 0.0)
    beta = jax.nn.sigmoid(ab)
    ri = lax.broadcasted_iota(jnp.int32, (rows, rows), 0)
    rj = lax.broadcasted_iota(jnp.int32, (rows, rows), 1)
    tri = (((ri // span) == (rj // span)) & (ri >= rj)).astype(BF16)
    g_hi, g_mid, g_lo = _split3(g)
    g_cum = (jnp.dot(tri, g_hi, preferred_element_type=F32)
             + jnp.dot(tri, g_mid, preferred_element_type=F32)
             + jnp.dot(tri, g_lo, preferred_element_type=F32))
    for c in range(n_chunks):
        gt_scr[c] = g_cum[c * n:(c + 1) * n, :].T
    for h in heads:
        gx_scr[:, o_sl(h)] = jnp.broadcast_to(g_cum[:, n_heads + h:n_heads + h + 1], (rows, dk))
        kb_scr[:, o_sl(h)] = jnp.broadcast_to(beta[:, h:h + 1], (rows, dk))

    for j in range(ch // col_w):
        sl = slice(j * col_w, (j + 1) * col_w)
        p = _dot(u, win_ref[:, sl]).reshape(bb, tt, col_w)
        qkv_scr[:, sl] = _silu(_causal_conv(halo_scr, ncst_ref, p, wconv_ref, width, sl))

    z_scr[...] = _silu(_dot(u, win_ref[:, ch:ch + vw]))

    for h in heads:
        t = qkv_scr[:, q_sl(h)]
        qkv_scr[:, q_sl(h)] = t * (lax.rsqrt(jnp.sum(t * t, axis=-1, keepdims=True) + NORM_EPS)
                                   * dk ** -0.5)
        t = qkv_scr[:, k_sl(h)]
        t = t * lax.rsqrt(jnp.sum(t * t, axis=-1, keepdims=True) + NORM_EPS)
        qkv_scr[:, k_sl(h)] = t
        beta_x = kb_scr[:, o_sl(h)]
        qkv_scr[:, v_sl(h)] = qkv_scr[:, v_sl(h)] * beta_x
        kb_scr[:, o_sl(h)] = t * beta_x

    causal, strict, eye, levels = _tri_masks(n, span)
    row_seq = lax.broadcasted_iota(jnp.int32, (n, 1), 0) // span

    def local_body(gi, carry):
        items = []
        for ci in range(group):
            c = gi * group + ci
            rs = pl.ds(pl.multiple_of(c * n, n), n)
            items += [(c, rs, h) for h in heads]
        decay, a_mat, t_inv = [], [], []
        for c, rs, h in items:
            g_row = gt_scr[c, n_heads + h:n_heads + h + 1, :]
            diff = gx_scr[rs, h * dv:h * dv + n] - g_row
            decay.append(jnp.where(causal, jnp.exp(jnp.where(causal, diff, 0.0)), 0.0))
        for (c, rs, h), dec in zip(items, decay):
            k = qkv_scr[rs, k_sl(h)]
            a_mat.append(jnp.where(strict, _dot_nt(kb_scr[rs, o_sl(h)], k) * dec, 0.0))
        attn_v = []
        for (c, rs, h), dec in zip(items, decay):
            attn_v.append(_dot_nt(qkv_scr[rs, q_sl(h)], qkv_scr[rs, k_sl(h)]) * dec)
            if not folded:
                attn_scr[rs, h * dv:h * dv + n] = attn_v[-1]
        t_inv = [eye - jnp.where(levels[0], a, 0.0) for a in a_mat]
        for li, m in enumerate(levels[1:]):
            s = 2 << li
            if s % SUBLANES:
                y = [_dot(jnp.where(m, a, 0.0), t) for a, t in zip(a_mat, t_inv)]
                xs = [_dot(t, yy) for t, yy in zip(t_inv, y)]
                t_inv = [t - xx for t, xx in zip(t_inv, xs)]
                continue
            lo = [(b * 2 * s + s, b * 2 * s + 2 * s) for b in range(n // (2 * s))]
            take = lambda x: jnp.concatenate([x[r0:r1] for r0, r1 in lo], axis=0)
            def put(xc, rest):
                parts = []
                for b, (r0, r1) in enumerate(lo):
                    parts += [rest(r0 - s, r0), xc[b * s:(b + 1) * s]]
                return jnp.concatenate(parts, axis=0)
            zeros = jnp.zeros((s, n), F32)
            y = [put(_dot(take(jnp.where(m, a, 0.0)), t), lambda r0, r1: zeros)
                 for a, t in zip(a_mat, t_inv)]
            xs = [_dot(take(t), yy) for t, yy in zip(t_inv, y)]
            t_inv = [put(take(t) - xx, lambda r0, r1, t=t: t[r0:r1]) for t, xx in zip(t_inv, xs)]
        for (c, rs, h), t, attn in zip(items, t_inv, attn_v):
            gx = gx_scr[rs, o_sl(h)]
            e_x = jnp.exp(gx)
            rhs = jnp.concatenate([qkv_scr[rs, v_sl(h)], kb_scr[rs, o_sl(h)] * e_x], axis=1)
            sol = _dot(t, rhs)
            if folded:
                kd = qkv_scr[rs, k_sl(h)] * jnp.exp(gx[n - 1:n, :] - gx)
                nm = _dot_tn(kd, sol)
                au = _dot(attn, sol)
                wq_scr[c, 0:dk, o_sl(h)] = nm[:, :dv]
                wq_scr[c, dk:2 * dk, o_sl(h)] = nm[:, dv:]
                u_scr[rs, o_sl(h)] = au[:, :dv]
                attn_scr[rs, o_sl(h)] = qkv_scr[rs, q_sl(h)] * e_x - au[:, dv:]
                continue
            u_scr[rs, o_sl(h)] = sol[:, :dv]
            wq_scr[c, 0:n, o_sl(h)] = sol[:, dv:]
            wq_scr[c, n:2 * n, o_sl(h)] = qkv_scr[rs, q_sl(h)] * e_x
        return carry

    lax.fori_loop(0, n_chunks // group, local_body, 0)

    chunks_per_seq = tt // n if not has_state else n_chunks

    def state_body(c, carry):
        if folded:
            keys = [(h, q) for h in heads for q in range(bb)]
            ck = lambda q: q * chunks_per_seq + c
            end = lambda q: q * tt + (c + 1) * n
            s_prev = {(h, q): s_scr[q, h] for h, q in keys}
            ms = {(h, q): _dot(wq_scr[ck(q), dk:2 * dk, o_sl(h)], s_prev[h, q]) for h, q in keys}
            for h, q in keys:
                g_last = gx_scr[end(q) - 1:end(q), o_sl(h)]
                s_scr[q, h] = (s_prev[h, q] * jnp.exp(g_last) + wq_scr[ck(q), 0:dk, o_sl(h)]
                               - ms[h, q])
            for h, q in keys:
                r = slice(end(q) - n, end(q))
                oh_scr[r, o_sl(h)] = _dot(attn_scr[r, o_sl(h)], s_prev[h, q]) + u_scr[r, o_sl(h)]
            return carry
        if has_state:
            lanes = [(0, s) for s in range(n_seq)]
            rows_of = lambda q: slice(c * n, (c + 1) * n)
            chunk_of = lambda q: c
        else:
            lanes = [(q, 0) for q in range(bb)]
            rows_of = lambda q: slice(q * tt + c * n, q * tt + (c + 1) * n)
            chunk_of = lambda q: q * chunks_per_seq + c
        groups = sorted({q for q, _ in lanes})
        masks = [None if n_seq == 1 else (row_seq == s) for s in range(n_seq)]
        if has_state:
            s_prev = {(h, q, s): s0_ref[c * n_seq + s, h] for h in heads for q, s in lanes}
        else:
            s_prev = {(h, q, s): s_scr[q, h] for h in heads for q, s in lanes}
        def own_rows(q, s, h):
            if n_seq == 1:
                return wq_scr[chunk_of(q), :, o_sl(h)]
            return jnp.concatenate(
                [wq_scr[chunk_of(q), s * span:(s + 1) * span, o_sl(h)],
                 wq_scr[chunk_of(q), n + s * span:n + (s + 1) * span, o_sl(h)]], axis=0)
        ws = {(h, q, s): _dot(own_rows(q, s, h), s_prev[(h, q, s)])
              for h in heads for q, s in lanes}
        v_new, v_dec, o_part, g_last = {}, {}, {}, {}
        for h in heads:
            for q in groups:
                w_s = jnp.concatenate([ws[(h, q, s)][:span] for s in range(n_seq)], axis=0)
                v_new[h, q] = u_scr[rows_of(q), o_sl(h)] - w_s
                o_part[h, q] = jnp.concatenate(
                    [ws[(h, q, s)][span:] for s in range(n_seq)], axis=0)
                gx = gx_scr[rows_of(q), o_sl(h)]
                for s in range(n_seq):
                    g_last[h, q, s] = gx[(s + 1) * span - 1:(s + 1) * span, :]
                g_end = jnp.concatenate(
                    [jnp.broadcast_to(g_last[h, q, s], (span, dv)) for s in range(n_seq)], axis=0)
                v_dec[h, q] = v_new[h, q] * jnp.exp(g_end - gx)
        for h in heads:
            for q in groups:
                oh_scr[rows_of(q), o_sl(h)] = o_part[h, q] + _dot(
                    attn_scr[rows_of(q), h * dv:h * dv + n], v_new[h, q])
        for h in heads:
            for q in groups:
                vs = v_dec[h, q] if n_seq == 1 else jnp.concatenate(
                    [jnp.where(masks[s], v_dec[h, q], 0.0) for s in range(n_seq)], axis=1)
                ktv = _dot_tn(qkv_scr[rows_of(q), k_sl(h)], vs)
                for s in range(n_seq):
                    s_new = (s_prev[(h, q, s)] * jnp.exp(g_last[h, q, s])
                             + ktv[:, s * dv:(s + 1) * dv])
                    if has_state:
                        ns_ref[c * n_seq + s, h] = s_new
                    else:
                        s_scr[q, h] = s_new
        return carry

    for c in range(chunks_per_seq):
        state_body(c, 0)
    if not has_state:
        ns_ref[...] = s_scr[...]

    for h in heads:
        o = oh_scr[:, o_sl(h)]
        o = o * lax.rsqrt(jnp.mean(o * o, axis=-1, keepdims=True) + NORM_EPS) * nw_ref[...]
        oh_scr[:, o_sl(h)] = o * z_scr[:, o_sl(h)]
    out = _dot(oh_scr[...], wout_ref[...])
    _residual_norm(x, out, mod_ref, lng_ref, lnb_ref, o_ref, alpha)


def _gdn_layer(x, mod, conv_state, ssm_state, w_in, w_conv, a_log, dt_bias, norm_w, w_out,
               ln_g, ln_b, *, bb, tt, alpha):
    bsz, seq, d = x.shape
    width, ch = w_conv.shape
    n_heads = a_log.shape[0]
    vw = w_out.shape[0]
    dv = vw // n_heads
    dk = (ch - vw) // 2 // n_heads
    qk = n_heads * dk
    has_state = conv_state is not None
    rows = bb * tt
    seq_len = min(seq, GDN_ROWS)
    n = min(GDN_ROWS, rows)
    n_chunks = rows // n
    assert dk == dv == LANES, "one head per lane tile"
    assert rows % n == 0 and n % seq_len == 0
    assert has_state == (tt == seq), "a carried state needs whole sequences per block"
    group = max(g for g in (8, 4, 2, 1) if n_chunks % g == 0)

    assert w_in.shape[1] == ch + vw + 2 * n_heads and 2 * n_heads <= LANES
    pad = lambda vec: jnp.zeros((1, LANES), F32).at[0, n_heads:2 * n_heads].set(vec)
    in_specs = [pl.BlockSpec((bb, tt, d), lambda b, t: (b, t, 0)),
                pl.BlockSpec((bb, 3, d), lambda b, t: (b, 0, 0))]
    args = [x, mod]
    if has_state:
        in_specs += [pl.BlockSpec((bb, width - 1, ch), lambda b, t: (b, 0, 0)),
                     pl.BlockSpec((bb, n_heads, dk, dv), lambda b, t: (b, 0, 0, 0))]
        args += [conv_state, ssm_state]
    in_specs += [_const_spec(w_in.shape), _const_spec(w_conv.shape),
                 _const_spec((1, LANES)), _const_spec((1, LANES)), _const_spec((1, dv)),
                 _const_spec(w_out.shape), _const_spec((1, d)), _const_spec((1, d))]
    args += [w_in, w_conv, pad(a_log), pad(dt_bias), norm_w.reshape(1, dv), w_out,
             ln_g.reshape(1, d), ln_b.reshape(1, d)]
    scratch = [pltpu.VMEM((bb, HALO, ch), F32),
               pltpu.VMEM((d, LANES), BF16),
               pltpu.VMEM((rows, ch), F32),
               pltpu.VMEM((rows, vw), F32),
               pltpu.VMEM((n_chunks, LANES, n), F32),
               pltpu.VMEM((rows, vw), F32),
               pltpu.VMEM((rows, qk), F32),
               pltpu.VMEM((rows, vw), F32),
               pltpu.VMEM((n_chunks, 2 * n if has_state else 2 * dk, qk), F32),
               pltpu.VMEM((rows, vw), F32),
               pltpu.VMEM((rows, vw), F32)]
    if not has_state:
        scratch.append(pltpu.VMEM((bb, n_heads, dk, dv), F32))
    kern = functools.partial(_gdn_layer_kernel, has_state=has_state, width=width, n_heads=n_heads,
                             dk=dk, dv=dv, seq_len=seq_len, col_w=min(ch, 512), group=group,
                             alpha=alpha)
    return pl.pallas_call(
        kern,
        grid=(bsz // bb, seq // tt),
        in_specs=in_specs,
        out_specs=[pl.BlockSpec((bb, tt, d), lambda b, t: (b, t, 0)),
                   pl.BlockSpec((bb, width - 1, ch), lambda b, t: (b, 0, 0)),
                   pl.BlockSpec((bb, n_heads, dk, dv), lambda b, t: (b, 0, 0, 0))],
        out_shape=[jax.ShapeDtypeStruct((bsz, seq, d), F32),
                   jax.ShapeDtypeStruct((bsz, width - 1, ch), F32),
                   jax.ShapeDtypeStruct((bsz, n_heads, dk, dv), F32)],
        scratch_shapes=scratch,
        compiler_params=pltpu.CompilerParams(
            dimension_semantics=("arbitrary", "arbitrary"), vmem_limit_bytes=VMEM_LIMIT),
        name="gdn_layer",
    )(*args)


def _block_rows(bsz, seq, target, long_seqs=1):
    if seq * long_seqs >= target:
        return long_seqs, target // long_seqs
    return max(1, min(bsz, target // seq)), seq


def _trunk(x, mod, conv_a, conv_b, ssm_b, ln_g, ln_b, wa, wb, alpha):
    bsz, seq, _ = x.shape
    new_a, new_cb, new_s = [], [], []
    depth = mod.shape[0]
    for l in range(depth):
        i = l // 2
        if l % 2 == 0:
            bb, tt = _block_rows(bsz, seq, 1024 if conv_a is None else 512)
            x, nb = _conv_layer(x, mod[l], None if conv_a is None else conv_a[i],
                                wa["in"][i], wa["conv"][i], wa["out"][i], ln_g[l], ln_b[l],
                                bb=bb, tt=tt, alpha=alpha)
            new_a.append(nb)
        else:
            bb, tt = _block_rows(bsz, seq, *((512, 2) if conv_b is None else (2 * GDN_ROWS,)))
            x, nb, s = _gdn_layer(x, mod[l], None if conv_b is None else conv_b[i],
                                  None if ssm_b is None else ssm_b[i],
                                  wb["in"][i], wb["conv"][i], wb["a_log"][i],
                                  wb["dt_bias"][i], wb["norm"][i], wb["out"][i], ln_g[l], ln_b[l],
                                  bb=bb, tt=tt, alpha=alpha)
            new_cb.append(nb)
            new_s.append(s)
    return x, jnp.stack(new_a), jnp.stack(new_cb), jnp.stack(new_s)


def kernel(x_prompt, x_sample, state_conv_a, state_conv_b, state_ssm_b, c_prompt, c_sample, w_mod, b_mod, ln_g, ln_b, wa_in, wa_conv, wa_out, wb_in, wb_conv, wb_a_log, wb_dt_bias, wb_norm, wb_out):
    depth, d, _ = w_mod.shape
    alpha = (2 * depth) ** 0.25
    bp = x_prompt.shape[0]

    mod = _modulation(jnp.concatenate([c_prompt, c_sample], axis=0), w_mod, b_mod)
    mod = mod.reshape(depth, -1, 3, d)
    mod_p, mod_s = mod[:, :bp], mod[:, bp:]

    wa = {"in": wa_in.astype(BF16), "conv": wa_conv, "out": wa_out.astype(BF16)}
    wb = {"in": wb_in.astype(BF16), "conv": wb_conv,
          "a_log": wb_a_log, "dt_bias": wb_dt_bias, "norm": wb_norm, "out": wb_out.astype(BF16)}

    y_p, ca_p, cb_p, s_p = _trunk(x_prompt, mod_p, None, None, None, ln_g, ln_b, wa, wb, alpha)
    y_s, ca_s, cb_s, s_s = _trunk(x_sample, mod_s, state_conv_a, state_conv_b, state_ssm_b,
                                  ln_g, ln_b, wa, wb, alpha)
    return (y_p, y_s, ca_p, cb_p, s_p, ca_s, cb_s, s_s)
```
